```python
import math
import jax, jax.numpy as jnp
from jax import lax
import numpy as np

D_MODEL = 1024
BATCH = 2
SEQ = 8192
DEPTH = 2

LA_HEADS = 4
LA_DK = 128
LA_DV = 128
LA_QK = LA_HEADS * LA_DK
LA_V = LA_HEADS * LA_DV
CONV_K = 4
CHUNK = 64
DIFF_HEADS = 4
DIFF_DH = 64
DIFF_DV = 2 * DIFF_DH
DIFF_QK = DIFF_HEADS * 2 * DIFF_DH
DIFF_V = DIFF_HEADS * DIFF_DV
Q_BLOCK = 128
NUM_BUCKETS = 32
MAX_DISTANCE = 128
D_MIX = LA_V + DIFF_V
SPLIT_SIZES = [3 * LA_QK if False else 2 * LA_QK + LA_V, LA_V, LA_HEADS, LA_HEADS, DIFF_QK, DIFF_QK, DIFF_V]
SPLIT_IDX = [int(s) for s in np.cumsum(SPLIT_SIZES)[:-1]]
IN_DIM = int(sum(SPLIT_SIZES))
LA_CONV_DIM = 2 * LA_QK + LA_V
D_FF = -(-(8 * D_MODEL // 3) // 256) * 256
NORM_EPS = 1e-6

kernel_name = "hybrid_gdn_diffattn_parallel_heads"


def rmsnorm(x, w, eps=NORM_EPS):
    xf = x.astype(jnp.float32)
    y = xf * lax.rsqrt(jnp.mean(xf * xf, axis=-1, keepdims=True) + eps) * w.astype(jnp.float32)
    return y.astype(x.dtype)


def l2norm(x, eps=1e-6):
    return x * lax.rsqrt(jnp.sum(x * x, axis=-1, keepdims=True) + eps)


def causal_conv(x, w):
    T = x.shape[1]
    K = w.shape[0]
    xp = jnp.pad(x, ((0, 0), (K - 1, 0), (0, 0)))
    out = xp[:, 0:T] * w[0]
    for i in range(1, K):
        out = out + xp[:, i:i + T] * w[i]
    return out


def gated_delta_rule(q, k, v, g, beta):
    f32 = jnp.float32
    B, T, H, Dk = q.shape
    Dv = v.shape[-1]
    n = T // CHUNK
    q = l2norm(q.astype(f32)) * (Dk ** -0.5)
    k = l2norm(k.astype(f32))
    v = v.astype(f32)

    def chunk(t):
        return t.reshape(B, n, CHUNK, H, -1).transpose(1, 0, 3, 2, 4)

    q, k, v = chunk(q), chunk(k), chunk(v)
    beta = beta.astype(f32).reshape(B, n, CHUNK, H).transpose(1, 0, 3, 2)
    g = jnp.cumsum(g.astype(f32).reshape(B, n, CHUNK, H).transpose(1, 0, 3, 2), axis=-1)

    causal = jnp.tril(jnp.ones((CHUNK, CHUNK), dtype=bool))
    strict = jnp.tril(jnp.ones((CHUNK, CHUNK), dtype=bool), k=-1)
    gdiff = g[..., :, None] - g[..., None, :]
    decay = jnp.where(causal, jnp.exp(jnp.where(causal, gdiff, 0.0)), 0.0)

    k_beta = k * beta[..., None]
    L = jnp.where(strict, jnp.einsum('nbhcd,nbhsd->nbhcs', k_beta, k) * decay, 0.0)
    A = L + jnp.eye(CHUNK, dtype=f32)
    rhs = jnp.concatenate([v * beta[..., None], k_beta * jnp.exp(g)[..., None]], axis=-1)
    sol = lax.linalg.triangular_solve(A, rhs, left_side=True, lower=True, unit_diagonal=True)
    u, w = sol[..., :Dv], sol[..., Dv:]
    intra = jnp.where(causal, jnp.einsum('nbhcd,nbhsd->nbhcs', q, k) * decay, 0.0)

    def step(S, inp):
        q_i, k_i, u_i, w_i, g_i, a_i = inp
        v_new = u_i - jnp.einsum('bhcd,bhde->bhce', w_i, S)
        o = (jnp.einsum('bhcd,bhde->bhce', q_i * jnp.exp(g_i)[..., None], S)
             + jnp.einsum('bhcs,bhse->bhce', a_i, v_new))
        g_last = g_i[..., -1]
        S = (S * jnp.exp(g_last)[..., None, None]
             + jnp.einsum('bhcd,bhce->bhde', k_i * jnp.exp(g_last[..., None] - g_i)[..., None], v_new))
        return S, o

    S0 = jnp.zeros((B, H, Dk, Dv), dtype=f32)
    _, o = lax.scan(step, S0, (q, k, u, w, g, intra))
    return o.transpose(1, 0, 3, 2, 4).reshape(B, T, H, Dv)


def t5_bucket(rel):
    n = jnp.maximum(rel, 0)
    max_exact = NUM_BUCKETS // 2
    nf = jnp.maximum(n, 1).astype(jnp.float32)
    large = max_exact + (jnp.log(nf / max_exact) / math.log(MAX_DISTANCE / max_exact)
                         * (NUM_BUCKETS - max_exact)).astype(jnp.int32)
    large = jnp.minimum(large, NUM_BUCKETS - 1)
    return jnp.where(n < max_exact, n, large)


def diff_attention(q, k, v, lam, rel_bias):
    B, T, H, _, dh = q.shape
    nb = T // Q_BLOCK
    q = q * (dh ** -0.5)
    qb = q.reshape(B, nb, Q_BLOCK, H, 2, dh).transpose(1, 0, 2, 3, 4, 5)
    kpos = jnp.arange(T)

    def block(args):
        q_blk, i = args
        qpos = i * Q_BLOCK + jnp.arange(Q_BLOCK)
        rel = qpos[:, None] - kpos[None, :]
        bias = rel_bias[t5_bucket(rel)].astype(jnp.float32)
        s = jnp.einsum('bqhmd,bkhmd->bhmqk', q_blk, k).astype(jnp.float32)
        s = s + bias.transpose(2, 0, 1)[None, :, None]
        s = jnp.where(rel >= 0, s, -jnp.inf)
        p = jax.nn.softmax(s, axis=-1)
        p = p[:, :, 0] - lam * p[:, :, 1]
        return jnp.einsum('bhqk,bkhe->bqhe', p.astype(v.dtype), v)

    o = lax.map(block, (qb, jnp.arange(nb)))
    return o.transpose(1, 0, 2, 3, 4).reshape(B, T, H, 2 * dh)


def setup_inputs(seed: int = 0) -> dict:
    key = jax.random.key(seed)
    ks = jax.random.split(key, 20)
    f32 = jnp.float32
    nrm = lambda k, s, sc: jax.random.normal(k, s, f32) * sc
    x = jax.random.normal(ks[0], (BATCH, SEQ, D_MODEL), f32)
    attn_norm_w = 1.0 + nrm(ks[1], (DEPTH, D_MODEL), 0.1)
    w_in = nrm(ks[2], (DEPTH, D_MODEL, IN_DIM), D_MODEL ** -0.5)
    conv_w = nrm(ks[3], (DEPTH, CONV_K, LA_CONV_DIM), CONV_K ** -0.5)
    a_log = jnp.log(jax.random.uniform(ks[4], (DEPTH, LA_HEADS), f32, 1.0, 16.0))
    dt = jnp.exp(jax.random.uniform(ks[5], (DEPTH, LA_HEADS), f32, math.log(1e-3), math.log(1e-1)))
    dt_bias = jnp.log(jnp.expm1(dt))
    la_norm_w = 1.0 + nrm(ks[6], (DEPTH, LA_DV), 0.1)
    lambda_q1 = nrm(ks[7], (DEPTH, DIFF_DH), 0.1)
    lambda_k1 = nrm(ks[8], (DEPTH, DIFF_DH), 0.1)
    lambda_q2 = nrm(ks[9], (DEPTH, DIFF_DH), 0.1)
    lambda_k2 = nrm(ks[10], (DEPTH, DIFF_DH), 0.1)
    diff_norm_w = 1.0 + nrm(ks[11], (DEPTH, DIFF_DV), 0.1)
    rel_bias = nrm(ks[12], (NUM_BUCKETS, DIFF_HEADS), 0.5)
    w_out = nrm(ks[13], (DEPTH, D_MIX, D_MODEL), D_MIX ** -0.5)
    ffn_norm_w = 1.0 + nrm(ks[14], (DEPTH, D_MODEL), 0.1)
    w_gate_up = nrm(ks[15], (DEPTH, D_MODEL, 2 * D_FF), D_MODEL ** -0.5)
    w_down = nrm(ks[16], (DEPTH, D_FF, D_MODEL), D_FF ** -0.5)
    final_norm_w = 1.0 + nrm(ks[17], (D_MODEL,), 0.1)
    return {"x": x, "attn_norm_w": attn_norm_w, "w_in": w_in, "conv_w": conv_w,
            "a_log": a_log, "dt_bias": dt_bias, "la_norm_w": la_norm_w,
            "lambda_q1": lambda_q1, "lambda_k1": lambda_k1, "lambda_q2": lambda_q2,
            "lambda_k2": lambda_k2, "diff_norm_w": diff_norm_w, "rel_bias": rel_bias,
            "w_out": w_out, "ffn_norm_w": ffn_norm_w, "w_gate_up": w_gate_up,
            "w_down": w_down, "final_norm_w": final_norm_w}


def reference(x, attn_norm_w, w_in, conv_w, a_log, dt_bias, la_norm_w, lambda_q1, lambda_k1,
              lambda_q2, lambda_k2, diff_norm_w, rel_bias, w_out, ffn_norm_w, w_gate_up,
              w_down, final_norm_w):
    f32 = jnp.float32
    B, T, _ = x.shape
    for l in range(DEPTH):
        h = rmsnorm(x, attn_norm_w[l])
        proj = h @ w_in[l]
        qkv_la, z_la, b_la, a_la, q_d, k_d, v_d = jnp.split(proj, SPLIT_IDX, axis=-1)

        qkv_la = jax.nn.silu(causal_conv(qkv_la, conv_w[l]))
        q_la, k_la, v_la = jnp.split(qkv_la, [LA_QK, 2 * LA_QK], axis=-1)
        q_la = q_la.reshape(B, T, LA_HEADS, LA_DK)
        k_la = k_la.reshape(B, T, LA_HEADS, LA_DK)
        v_la = v_la.reshape(B, T, LA_HEADS, LA_DV)
        beta = jax.nn.sigmoid(b_la.astype(f32))
        g = -jnp.exp(a_log[l].astype(f32)) * jax.nn.softplus(a_la.astype(f32) + dt_bias[l].astype(f32))
        o_la = gated_delta_rule(q_la, k_la, v_la, g, beta).astype(x.dtype)
        o_la = rmsnorm(o_la, la_norm_w[l]) * jax.nn.silu(z_la.reshape(B, T, LA_HEADS, LA_DV))
        o_la = o_la.reshape(B, T, LA_V)

        lam_init = 0.8 - 0.6 * math.exp(-0.3 * l)
        lam = (jnp.exp(jnp.sum(lambda_q1[l].astype(f32) * lambda_k1[l].astype(f32)))
               - jnp.exp(jnp.sum(lambda_q2[l].astype(f32) * lambda_k2[l].astype(f32))) + lam_init)
        o_d = diff_attention(q_d.reshape(B, T, DIFF_HEADS, 2, DIFF_DH),
                             k_d.reshape(B, T, DIFF_HEADS, 2, DIFF_DH),
                             v_d.reshape(B, T, DIFF_HEADS, DIFF_DV), lam, rel_bias)
        o_d = rmsnorm(o_d, diff_norm_w[l], eps=1e-5) * (1.0 - lam_init)
        o_d = o_d.reshape(B, T, DIFF_V)

        x = x + jnp.concatenate([o_la, o_d], axis=-1) @ w_out[l]

        h = rmsnorm(x, ffn_norm_w[l])
        gate, up = jnp.split(h @ w_gate_up[l], 2, axis=-1)
        x = x + (jax.nn.silu(gate) * up) @ w_down[l]
    return rmsnorm(x, final_norm_w)
```

```python
import functools
import math

import jax
import jax.numpy as jnp
from jax import lax
from jax.experimental import pallas as pl
from jax.experimental.pallas import tpu as pltpu

LA_HEADS = 4
LA_DK = 128
LA_DV = 128
LA_QK = LA_HEADS * LA_DK
LA_V = LA_HEADS * LA_DV
CONV_K = 4
DIFF_HEADS = 4
DIFF_DH = 64
DIFF_DV = 2 * DIFF_DH
DIFF_QK = DIFF_HEADS * 2 * DIFF_DH
DIFF_V = DIFF_HEADS * DIFF_DV
NUM_BUCKETS = 32
MAX_DISTANCE = 128
NORM_EPS = 1e-6
DIFF_NORM_EPS = 1e-5
L2_EPS = 1e-6

LANES = 128
SUBLANES = 8
BA_PAD = LANES
LA_WIDTH = 2 * LA_QK + 2 * LA_V + BA_PAD
NEG_BIG = -1e30

VMEM_LIMIT = 52 * 1024 * 1024

F32 = jnp.float32
BF16 = jnp.bfloat16
HIGHEST = lax.Precision.HIGHEST


def _dot(a, b, precision=None):
    return jnp.dot(a, b, preferred_element_type=F32, precision=precision)


def _dot_nt(a, b, precision=None):
    return lax.dot_general(a, b, (((1,), (1,)), ((), ())),
                           preferred_element_type=F32, precision=precision)


def _dot_tn(a, b, precision=None):
    return lax.dot_general(a, b, (((0,), (0,)), ((), ())),
                           preferred_element_type=F32, precision=precision)


def _rms(x, w, eps):
    return x * lax.rsqrt(jnp.mean(x * x, axis=-1, keepdims=True) + eps) * w


def _silu(x):
    return x * jax.nn.sigmoid(x)


def _softplus(x):
    return jnp.maximum(x, 0.0) + jnp.log1p(jnp.exp(-jnp.abs(x)))


def _in_proj_kernel(x_ref, nw_ref, wla_ref, wd_ref, la_ref, d_ref):
    h = _rms(x_ref[...], nw_ref[...], NORM_EPS).astype(BF16)
    la_ref[...] = _dot(h, wla_ref[...])
    d_ref[...] = _dot(h, wd_ref[...]).astype(BF16)


def _in_proj(xf, nw, wla, wd, tm):
    M, D = xf.shape
    return pl.pallas_call(
        _in_proj_kernel,
        grid=(M // tm,),
        in_specs=[
            pl.BlockSpec((tm, D), lambda i: (i, 0)),
            pl.BlockSpec((1, D), lambda i: (0, 0)),
            pl.BlockSpec(wla.shape, lambda i: (0, 0)),
            pl.BlockSpec(wd.shape, lambda i: (0, 0)),
        ],
        out_specs=[
            pl.BlockSpec((tm, wla.shape[1]), lambda i: (i, 0)),
            pl.BlockSpec((tm, wd.shape[1]), lambda i: (i, 0)),
        ],
        out_shape=[
            jax.ShapeDtypeStruct((M, wla.shape[1]), F32),
            jax.ShapeDtypeStruct((M, wd.shape[1]), BF16),
        ],
        compiler_params=pltpu.CompilerParams(
            dimension_semantics=("parallel",), vmem_limit_bytes=VMEM_LIMIT),
        name="in_proj",
    )(xf, nw.reshape(1, D), wla, wd)


def _unit_lower_inverse(L, C):
    row = lax.broadcasted_iota(jnp.int32, (C, C), 0)
    col = lax.broadcasted_iota(jnp.int32, (C, C), 1)
    eye = (row == col).astype(F32)
    tinv = None
    s = 1
    while s < C:
        same_pair = (row // (2 * s)) == (col // (2 * s))
        low_left = same_pair & ((row // s) % 2 == 1) & ((col // s) % 2 == 0)
        off = jnp.where(low_left, L, 0.0)
        if tinv is None:
            tinv = eye - off
        else:
            tinv = tinv - _dot(tinv, _dot(off, tinv, HIGHEST), HIGHEST)
        s *= 2
    return tinv


def _gdn_kernel(qkv_ref, z_ref, ba_ref, convw_ref, avec_ref, dtvec_ref, nw_ref,
                o_ref, xbuf, state, y_s, g_s, gt_s, beta_s, *, TB, C):
    t = pl.program_id(1)
    W = 2 * LA_QK + LA_V

    @pl.when(t == 0)
    def _():
        xbuf[0:SUBLANES, :] = jnp.zeros((SUBLANES, W), F32)
        state[...] = jnp.zeros(state.shape, F32)

    xbuf[SUBLANES:SUBLANES + TB, :] = qkv_ref[...]
    conv = convw_ref[CONV_K - 1:CONV_K, :] * xbuf[SUBLANES:SUBLANES + TB, :]
    for i in range(CONV_K - 1):
        off = SUBLANES - (CONV_K - 1) + i
        conv = conv + convw_ref[i:i + 1, :] * xbuf[off:off + TB, :]
    xbuf[0:SUBLANES, :] = xbuf[TB:TB + SUBLANES, :]
    y = _silu(conv)

    for h in range(LA_HEADS):
        qs = slice(h * LA_DK, (h + 1) * LA_DK)
        ks = slice(LA_QK + h * LA_DK, LA_QK + (h + 1) * LA_DK)
        q = y[:, qs]
        k = y[:, ks]
        y_s[:, qs] = q * lax.rsqrt(jnp.sum(q * q, -1, keepdims=True) + L2_EPS) * (LA_DK ** -0.5)
        y_s[:, ks] = k * lax.rsqrt(jnp.sum(k * k, -1, keepdims=True) + L2_EPS)
    y_s[:, 2 * LA_QK:] = y[:, 2 * LA_QK:]

    ba = ba_ref[...]
    beta_s[...] = jax.nn.sigmoid(ba)
    ba_lane = lax.broadcasted_iota(jnp.int32, (1, BA_PAD), 1)
    is_g = (ba_lane >= LA_HEADS) & (ba_lane < 2 * LA_HEADS)
    g_raw = -jnp.where(is_g, jnp.exp(avec_ref[...]), 0.0) * _softplus(ba + dtvec_ref[...])

    row = lax.broadcasted_iota(jnp.int32, (C, C), 0)
    col = lax.broadcasted_iota(jnp.int32, (C, C), 1)
    causal = row >= col
    strict = row > col
    tril = causal.astype(F32)
    for c in range(TB // C):
        g_s[c * C:(c + 1) * C, :] = _dot(tril, g_raw[c * C:(c + 1) * C, :], HIGHEST)
    for j in range(TB // LANES):
        gt_s[:, j * LANES:(j + 1) * LANES] = g_s[j * LANES:(j + 1) * LANES, :].T

    for c in range(TB // C):
        rs = slice(c * C, (c + 1) * C)
        for h in range(LA_HEADS):
            qs = slice(h * LA_DK, (h + 1) * LA_DK)
            ks = slice(LA_QK + h * LA_DK, LA_QK + (h + 1) * LA_DK)
            vs = slice(2 * LA_QK + h * LA_DV, 2 * LA_QK + (h + 1) * LA_DV)
            qn = y_s[rs, qs]
            kn = y_s[rs, ks]
            v = y_s[rs, vs]
            gb = jnp.broadcast_to(g_s[rs, LA_HEADS + h:LA_HEADS + h + 1], (C, LANES))
            beta = jnp.broadcast_to(beta_s[rs, h:h + 1], (C, LANES))
            g_row = gt_s[LA_HEADS + h:LA_HEADS + h + 1, rs]

            gdiff = gb[:, :C] - g_row
            decay = jnp.where(causal, jnp.exp(jnp.where(causal, gdiff, 0.0)), 0.0)
            kb = kn * beta
            kq = _dot_nt(jnp.concatenate([kb, qn], axis=0).astype(BF16), kn.astype(BF16))
            L = jnp.where(strict, kq[:C] * decay, 0.0)
            intra = kq[C:] * decay
            tinv = _unit_lower_inverse(L, C)
            eg = jnp.exp(gb)
            rhs = jnp.concatenate([v * beta, kb * eg], axis=1)
            sol = _dot(tinv, rhs, HIGHEST)
            u = sol[:, :LA_DV]
            w = sol[:, LA_DV:]

            S = state[h]
            wq = jnp.concatenate([w, qn * eg], axis=0).astype(BF16)
            r2 = _dot(wq, S.astype(BF16))
            v_new = u - r2[:C]
            o = r2[C:] + _dot(intra.astype(BF16), v_new.astype(BF16))
            g_last = gb[C - 1:C, :]
            kd = kn * jnp.exp(g_last - gb)
            state[h] = S * jnp.exp(g_last) + _dot_tn(kd.astype(BF16), v_new.astype(BF16))

            zg = _silu(z_ref[rs, h * LA_DV:(h + 1) * LA_DV])
            o_ref[rs, h * LA_DV:(h + 1) * LA_DV] = (_rms(o, nw_ref[...], NORM_EPS) * zg).astype(BF16)


def _gdn(la, convw, avec, dtvec, nw, B, T, TB, C):
    M = B * T
    nT = T // TB
    W = 2 * LA_QK + LA_V
    kern = functools.partial(_gdn_kernel, TB=TB, C=C)
    return pl.pallas_call(
        kern,
        grid=(B, nT),
        in_specs=[
            pl.BlockSpec((TB, W), lambda b, t: (b * nT + t, 0)),
            pl.BlockSpec((TB, LA_V), lambda b, t: (b * nT + t, W // LA_V)),
            pl.BlockSpec((TB, BA_PAD), lambda b, t: (b * nT + t, (W + LA_V) // BA_PAD)),
            pl.BlockSpec((CONV_K, W), lambda b, t: (0, 0)),
            pl.BlockSpec((1, BA_PAD), lambda b, t: (0, 0)),
            pl.BlockSpec((1, BA_PAD), lambda b, t: (0, 0)),
            pl.BlockSpec((1, LA_DV), lambda b, t: (0, 0)),
        ],
        out_specs=pl.BlockSpec((TB, LA_V), lambda b, t: (b * nT + t, 0)),
        out_shape=jax.ShapeDtypeStruct((M, LA_V), BF16),
        scratch_shapes=[
            pltpu.VMEM((TB + SUBLANES, W), F32),
            pltpu.VMEM((LA_HEADS, LA_DK, LA_DV), F32),
            pltpu.VMEM((TB, W), F32),
            pltpu.VMEM((TB, BA_PAD), F32),
            pltpu.VMEM((LANES, TB), F32),
            pltpu.VMEM((TB, BA_PAD), F32),
        ],
        compiler_params=pltpu.CompilerParams(
            dimension_semantics=("parallel", "arbitrary"), vmem_limit_bytes=VMEM_LIMIT),
        name="gdn",
    )(la, la, la, convw, avec, dtvec, nw.reshape(1, LA_DV))


def _attn_kernel(lam_ref, q_ref, k_ref, v_ref, bias_ref, nw_ref, o_ref, *, tq, lam_init):
    i = pl.program_id(2)
    tk = tq
    q = q_ref[0]
    lane = lax.broadcasted_iota(jnp.int32, q.shape, 1)
    zero = jnp.zeros_like(q)
    q2 = jnp.concatenate([jnp.where(lane < DIFF_DH, q, zero),
                          jnp.where(lane >= DIFF_DH, q, zero)], axis=0)

    def step(j, carry, bias):
        m, l, acc = carry
        start = pl.multiple_of(j * tk, tk)
        kj = k_ref[0, pl.ds(start, tk), :]
        vj = v_ref[0, pl.ds(start, tk), :]
        s = _dot_nt(q2, kj)
        if bias is not None:
            s = (s.reshape(2, tq, tk) + bias[None]).reshape(2 * tq, tk)
        m_new = jnp.maximum(m, jnp.max(s, axis=-1, keepdims=True))
        alpha = jnp.exp(m - m_new)
        p = jnp.exp(s - m_new)
        l = alpha * l + jnp.sum(p, axis=-1, keepdims=True)
        acc = alpha * acc + _dot(p.astype(BF16), vj)
        return m_new, l, acc

    carry = (jnp.full((2 * tq, 1), NEG_BIG, F32), jnp.zeros((2 * tq, 1), F32),
             jnp.zeros((2 * tq, DIFF_DV), F32))
    carry = lax.fori_loop(0, jnp.maximum(i - 1, 0), lambda j, c: step(j, c, None), carry)
    carry = lax.fori_loop(jnp.maximum(i - 1, 0), i, lambda j, c: step(j, c, bias_ref[0, 1]), carry)
    m, l, acc = step(i, carry, bias_ref[0, 0])

    lam = (jnp.exp(jnp.sum(lam_ref[0:1, :] * lam_ref[1:2, :], axis=-1, keepdims=True))
           - jnp.exp(jnp.sum(lam_ref[2:3, :] * lam_ref[3:4, :], axis=-1, keepdims=True)) + lam_init)
    o = acc[:tq] / l[:tq] - lam * (acc[tq:] / l[tq:])
    o_ref[0] = (_rms(o, nw_ref[...], DIFF_NORM_EPS) * (1.0 - lam_init)).astype(BF16)


def _attn(dd, lams, bias_tab, nw, B, T, tq, lam_init):
    nq = T // tq
    dd3 = dd.reshape(B, T, dd.shape[1])
    kern = functools.partial(_attn_kernel, tq=tq, lam_init=lam_init)
    out = pl.pallas_call(
        kern,
        grid=(B, DIFF_HEADS, nq),
        in_specs=[
            pl.BlockSpec((4, DIFF_DH), lambda b, h, i: (0, 0)),
            pl.BlockSpec((1, tq, LANES), lambda b, h, i: (b, i, h)),
            pl.BlockSpec((1, T, LANES), lambda b, h, i: (b, 0, DIFF_HEADS + h)),
            pl.BlockSpec((1, T, LANES), lambda b, h, i: (b, 0, 2 * DIFF_HEADS + h)),
            pl.BlockSpec((1, 2, tq, tq), lambda b, h, i: (h, 0, 0, 0)),
            pl.BlockSpec((1, DIFF_DV), lambda b, h, i: (0, 0)),
        ],
        out_specs=pl.BlockSpec((1, tq, DIFF_DV), lambda b, h, i: (b, i, h)),
        out_shape=jax.ShapeDtypeStruct((B, T, DIFF_V), BF16),
        compiler_params=pltpu.CompilerParams(
            dimension_semantics=("parallel", "parallel", "arbitrary"),
            vmem_limit_bytes=VMEM_LIMIT),
        name="diff_attn",
    )(lams, dd3, dd3, dd3, bias_tab, nw.reshape(1, DIFF_DV))
    return out.reshape(B * T, DIFF_V)


def _t5_bucket(rel):
    n = jnp.maximum(rel, 0)
    max_exact = NUM_BUCKETS // 2
    nf = jnp.maximum(n, 1).astype(F32)
    large = max_exact + (jnp.log(nf / max_exact) / math.log(MAX_DISTANCE / max_exact)
                         * (NUM_BUCKETS - max_exact)).astype(jnp.int32)
    large = jnp.minimum(large, NUM_BUCKETS - 1)
    return jnp.where(n < max_exact, n, large)


def _bias_blocks(rel_bias, tq):
    assert tq + 1 >= MAX_DISTANCE
    qpos = jnp.arange(tq)[:, None]
    kpos = jnp.arange(tq)[None, :]
    far = rel_bias[NUM_BUCKETS - 1].astype(F32)
    blocks = []
    for d in (0, tq):
        rel = qpos - kpos + d
        b = rel_bias[_t5_bucket(rel)].astype(F32) - far
        b = jnp.where((rel >= 0)[..., None], b, NEG_BIG)
        blocks.append(b.transpose(2, 0, 1))
    return jnp.stack(blocks, axis=1)


def _ffn_kernel(x_ref, ola_ref, od_ref, wout_ref, nw_ref, wg_ref, wu_ref, wd_ref, fnw_ref,
                out_ref, h_s, acc_s, *, final_norm):
    f = pl.program_id(1)

    @pl.when(f == 0)
    def _():
        x1 = (x_ref[...] + _dot(ola_ref[...], wout_ref[0:LA_V, :])
              + _dot(od_ref[...], wout_ref[LA_V:LA_V + DIFF_V, :]))
        acc_s[...] = x1
        h_s[...] = _rms(x1, nw_ref[...], NORM_EPS).astype(BF16)

    h = h_s[...]
    a = (_silu(_dot(h, wg_ref[...])) * _dot(h, wu_ref[...])).astype(BF16)
    acc_s[...] += _dot(a, wd_ref[...])

    @pl.when(f == pl.num_programs(1) - 1)
    def _():
        y = acc_s[...]
        if final_norm:
            y = _rms(y, fnw_ref[...], NORM_EPS)
        out_ref[...] = y


def _ffn(xf, o_la, o_d, wout, nw, wgu, wdn, fnw, tm, tf, final_norm):
    M, D = xf.shape
    F = wdn.shape[0]
    nf = F // tf
    kern = functools.partial(_ffn_kernel, final_norm=final_norm)
    return pl.pallas_call(
        kern,
        grid=(M // tm, nf),
        in_specs=[
            pl.BlockSpec((tm, D), lambda i, f: (i, 0)),
            pl.BlockSpec((tm, LA_V), lambda i, f: (i, 0)),
            pl.BlockSpec((tm, DIFF_V), lambda i, f: (i, 0)),
            pl.BlockSpec(wout.shape, lambda i, f: (0, 0)),
            pl.BlockSpec((1, D), lambda i, f: (0, 0)),
            pl.BlockSpec((D, tf), lambda i, f: (0, f)),
            pl.BlockSpec((D, tf), lambda i, f: (0, nf + f)),
            pl.BlockSpec((tf, D), lambda i, f: (f, 0)),
            pl.BlockSpec((1, D), lambda i, f: (0, 0)),
        ],
        out_specs=pl.BlockSpec((tm, D), lambda i, f: (i, 0)),
        out_shape=jax.ShapeDtypeStruct((M, D), F32),
        scratch_shapes=[pltpu.VMEM((tm, D), BF16), pltpu.VMEM((tm, D), F32)],
        compiler_params=pltpu.CompilerParams(
            dimension_semantics=("parallel", "arbitrary"), vmem_limit_bytes=VMEM_LIMIT),
        name="out_proj_ffn",
    )(xf, o_la, o_d, wout, nw.reshape(1, D), wgu, wgu, wdn, fnw.reshape(1, D))


def _pick(n, pref):
    return pref if n % pref == 0 else n


def kernel(x, attn_norm_w, w_in, conv_w, a_log, dt_bias, la_norm_w, lambda_q1, lambda_k1,
           lambda_q2, lambda_k2, diff_norm_w, rel_bias, w_out, ffn_norm_w, w_gate_up,
           w_down, final_norm_w):
    B, T, D = x.shape
    depth = w_in.shape[0]
    M = B * T
    d_ff = w_down.shape[1]
    tm_proj = _pick(M, 512)
    tm_ffn = _pick(M, 1024)
    tf = _pick(d_ff, 256)
    tb = _pick(T, 256)
    chunk = 64
    tq = _pick(T, 256)

    n_la = 2 * LA_QK + 2 * LA_V
    bias_tab = _bias_blocks(rel_bias, tq)
    lane_pad = BA_PAD - 2 * LA_HEADS

    xf = x.reshape(M, D)
    for l in range(depth):
        w = w_in[l]
        wla = jnp.concatenate(
            [w[:, :n_la + 2 * LA_HEADS], jnp.zeros((D, lane_pad), w.dtype)], axis=1).astype(BF16)
        qscale = jnp.concatenate([jnp.full((DIFF_QK,), DIFF_DH ** -0.5, F32),
                                  jnp.ones((DIFF_QK + DIFF_V,), F32)])
        wd = (w[:, n_la + 2 * LA_HEADS:] * qscale).astype(BF16)
        la, dd = _in_proj(xf, attn_norm_w[l], wla, wd, tm_proj)

        avec = jnp.concatenate([jnp.zeros((LA_HEADS,), F32), a_log[l].astype(F32),
                                jnp.zeros((lane_pad,), F32)]).reshape(1, BA_PAD)
        dtvec = jnp.concatenate([jnp.zeros((LA_HEADS,), F32), dt_bias[l].astype(F32),
                                 jnp.zeros((lane_pad,), F32)]).reshape(1, BA_PAD)
        o_la = _gdn(la, conv_w[l], avec, dtvec, la_norm_w[l], B, T, tb, chunk)

        lam_init = 0.8 - 0.6 * math.exp(-0.3 * l)
        lams = jnp.stack([lambda_q1[l], lambda_k1[l], lambda_q2[l], lambda_k2[l]]).astype(F32)
        o_d = _attn(dd, lams, bias_tab, diff_norm_w[l], B, T, tq, lam_init)

        xf = _ffn(xf, o_la, o_d, w_out[l].astype(BF16), ffn_norm_w[l],
                  w_gate_up[l].astype(BF16), w_down[l].astype(BF16), final_norm_w,
                  tm_ffn, tf, final_norm=(l == depth - 1))
    return xf.reshape(B, T, D)
```

```python
import functools
import math

import jax
import jax.numpy as jnp
from jax import lax
from jax.experimental import pallas as pl
from jax.experimental.pallas import tpu as pltpu

LA_HEADS = 4
LA_DK = 128
LA_DV = 128
LA_QK = LA_HEADS * LA_DK
LA_V = LA_HEADS * LA_DV
CONV_K = 4
DIFF_HEADS = 4
DIFF_DH = 64
DIFF_DV = 2 * DIFF_DH
DIFF_QK = DIFF_HEADS * 2 * DIFF_DH
DIFF_V = DIFF_HEADS * DIFF_DV
NUM_BUCKETS = 32
MAX_DISTANCE = 128
NORM_EPS = 1e-6
DIFF_NORM_EPS = 1e-5
L2_EPS = 1e-6

LANES = 128
SUBLANES = 8
BA_PAD = LANES
LA_WIDTH = 2 * LA_QK + 2 * LA_V + BA_PAD
NEG_BIG = -1e30

VMEM_LIMIT = 52 * 1024 * 1024

F32 = jnp.float32
BF16 = jnp.bfloat16
HIGHEST = lax.Precision.HIGHEST


def _dot(a, b, precision=None):
    return jnp.dot(a, b, preferred_element_type=F32, precision=precision)


def _dot_nt(a, b, precision=None):
    return lax.dot_general(a, b, (((1,), (1,)), ((), ())),
                           preferred_element_type=F32, precision=precision)


def _dot_tn(a, b, precision=None):
    return lax.dot_general(a, b, (((0,), (0,)), ((), ())),
                           preferred_element_type=F32, precision=precision)


def _rms(x, w, eps):
    return x * lax.rsqrt(jnp.mean(x * x, axis=-1, keepdims=True) + eps) * w


def _silu(x):
    return x * jax.nn.sigmoid(x)


def _softplus(x):
    return jnp.maximum(x, 0.0) + jnp.log1p(jnp.exp(-jnp.abs(x)))


def _in_proj_kernel(x_ref, nw_ref, wla_ref, wd_ref, dscale_ref, la_ref, d_ref):
    h = _rms(x_ref[...], nw_ref[...], NORM_EPS).astype(BF16)
    la_ref[...] = _dot(h, wla_ref[...])
    d_ref[...] = (_dot(h, wd_ref[...]) * dscale_ref[...]).astype(BF16)


def _in_proj(xf, nw, wla, wd, dscale, tm):
    M, D = xf.shape
    return pl.pallas_call(
        _in_proj_kernel,
        grid=(M // tm,),
        in_specs=[
            pl.BlockSpec((tm, D), lambda i: (i, 0)),
            pl.BlockSpec((1, D), lambda i: (0, 0)),
            pl.BlockSpec(wla.shape, lambda i: (0, 0)),
            pl.BlockSpec(wd.shape, lambda i: (0, 0)),
            pl.BlockSpec(dscale.shape, lambda i: (0, 0)),
        ],
        out_specs=[
            pl.BlockSpec((tm, wla.shape[1]), lambda i: (i, 0)),
            pl.BlockSpec((tm, wd.shape[1]), lambda i: (i, 0)),
        ],
        out_shape=[
            jax.ShapeDtypeStruct((M, wla.shape[1]), F32),
            jax.ShapeDtypeStruct((M, wd.shape[1]), BF16),
        ],
        compiler_params=pltpu.CompilerParams(
            dimension_semantics=("parallel",), vmem_limit_bytes=VMEM_LIMIT),
        name="in_proj",
    )(xf, nw.reshape(1, D), wla, wd, dscale)


def _unit_lower_inverse_minus_eye(Ls, C):
    row = lax.broadcasted_iota(jnp.int32, (C, C), 0)
    col = lax.broadcasted_iota(jnp.int32, (C, C), 1)
    xs = None
    s = 1
    while s < C:
        same_pair = (row // (2 * s)) == (col // (2 * s))
        low_left = same_pair & ((row // s) % 2 == 1) & ((col // s) % 2 == 0)
        offs = [jnp.where(low_left, L, 0.0) for L in Ls]
        if xs is None:
            xs = [-m for m in offs]
        else:
            xb = [x.astype(BF16) for x in xs]
            ys = [m + _dot(m.astype(BF16), x) for m, x in zip(offs, xb)]
            xs = [x - y - _dot(x16, y.astype(BF16)) for x, x16, y in zip(xs, xb, ys)]
        s *= 2
    return xs


def _gdn_kernel(qkv_ref, z_ref, ba_ref, convw_ref, avec_ref, dtvec_ref, nw_ref,
                o_ref, xbuf, state, y_s, g_s, gt_s, beta_s, *, TB, C):
    t = pl.program_id(1)
    W = 2 * LA_QK + LA_V

    @pl.when(t == 0)
    def _():
        xbuf[0:SUBLANES, :] = jnp.zeros((SUBLANES, W), F32)
        state[...] = jnp.zeros(state.shape, F32)

    xbuf[SUBLANES:SUBLANES + TB, :] = qkv_ref[...]
    conv = convw_ref[CONV_K - 1:CONV_K, :] * xbuf[SUBLANES:SUBLANES + TB, :]
    for i in range(CONV_K - 1):
        off = SUBLANES - (CONV_K - 1) + i
        conv = conv + convw_ref[i:i + 1, :] * xbuf[off:off + TB, :]
    xbuf[0:SUBLANES, :] = xbuf[TB:TB + SUBLANES, :]
    y = _silu(conv)

    for h in range(LA_HEADS):
        qs = slice(h * LA_DK, (h + 1) * LA_DK)
        ks = slice(LA_QK + h * LA_DK, LA_QK + (h + 1) * LA_DK)
        q = y[:, qs]
        k = y[:, ks]
        y_s[:, qs] = q * lax.rsqrt(jnp.sum(q * q, -1, keepdims=True) + L2_EPS) * (LA_DK ** -0.5)
        y_s[:, ks] = k * lax.rsqrt(jnp.sum(k * k, -1, keepdims=True) + L2_EPS)
    y_s[:, 2 * LA_QK:] = y[:, 2 * LA_QK:]

    ba = ba_ref[...]
    beta_s[...] = jax.nn.sigmoid(ba)
    ba_lane = lax.broadcasted_iota(jnp.int32, (1, BA_PAD), 1)
    is_g = (ba_lane >= LA_HEADS) & (ba_lane < 2 * LA_HEADS)
    g_raw = -jnp.where(is_g, jnp.exp(avec_ref[...]), 0.0) * _softplus(ba + dtvec_ref[...])

    row = lax.broadcasted_iota(jnp.int32, (C, C), 0)
    col = lax.broadcasted_iota(jnp.int32, (C, C), 1)
    causal = row >= col
    strict = row > col
    tril = causal.astype(F32)
    nC = TB // C
    for c in range(nC):
        g_s[c * C:(c + 1) * C, :] = _dot(tril, g_raw[c * C:(c + 1) * C, :], HIGHEST)
    for j in range(TB // LANES):
        gt_s[:, j * LANES:(j + 1) * LANES] = g_s[j * LANES:(j + 1) * LANES, :].T

    probs = [(c, h) for c in range(nC) for h in range(LA_HEADS)]
    Ls, intras, rhss, qgs, kds, egls = [], [], [], [], [], []
    for c, h in probs:
        rs = slice(c * C, (c + 1) * C)
        qn = y_s[rs, h * LA_DK:(h + 1) * LA_DK]
        kn = y_s[rs, LA_QK + h * LA_DK:LA_QK + (h + 1) * LA_DK]
        v = y_s[rs, 2 * LA_QK + h * LA_DV:2 * LA_QK + (h + 1) * LA_DV]
        gb = jnp.broadcast_to(g_s[rs, LA_HEADS + h:LA_HEADS + h + 1], (C, LANES))
        beta = jnp.broadcast_to(beta_s[rs, h:h + 1], (C, LANES))
        g_row = gt_s[LA_HEADS + h:LA_HEADS + h + 1, rs]
        gdiff = gb[:, :C] - g_row
        decay = jnp.where(causal, jnp.exp(jnp.where(causal, gdiff, 0.0)), 0.0)
        kb = kn * beta
        kq = _dot_nt(jnp.concatenate([kb, qn], axis=0).astype(BF16), kn.astype(BF16))
        Ls.append(jnp.where(strict, kq[:C] * decay, 0.0))
        intras.append((kq[C:] * decay).astype(BF16))
        eg = jnp.exp(gb)
        rhss.append(jnp.concatenate([v * beta, kb * eg], axis=1))
        qgs.append((qn * eg).astype(BF16))
        g_last = gb[C - 1:C, :]
        kds.append((kn * jnp.exp(g_last - gb)).astype(BF16))
        egls.append(jnp.exp(g_last))

    xs = _unit_lower_inverse_minus_eye(Ls, C)
    sols = [r + _dot(x.astype(BF16), r.astype(BF16)) for x, r in zip(xs, rhss)]

    S = [state[h] for h in range(LA_HEADS)]
    for c in range(nC):
        ps = [c * LA_HEADS + h for h in range(LA_HEADS)]
        r2 = [_dot(jnp.concatenate([sols[p][:, LA_DV:].astype(BF16), qgs[p]], axis=0),
                   S[h].astype(BF16)) for h, p in enumerate(ps)]
        vn = [(sols[p][:, :LA_DV] - r2[h][:C]).astype(BF16) for h, p in enumerate(ps)]
        S = [S[h] * egls[p] + _dot_tn(kds[p], vn[h]) for h, p in enumerate(ps)]
        rs = slice(c * C, (c + 1) * C)
        for h, p in enumerate(ps):
            o = r2[h][C:] + _dot(intras[p], vn[h])
            zg = _silu(z_ref[rs, h * LA_DV:(h + 1) * LA_DV])
            o_ref[rs, h * LA_DV:(h + 1) * LA_DV] = (_rms(o, nw_ref[...], NORM_EPS) * zg).astype(BF16)
    for h in range(LA_HEADS):
        state[h] = S[h]


def _gdn(la, convw, avec, dtvec, nw, B, T, TB, C):
    M = B * T
    nT = T // TB
    W = 2 * LA_QK + LA_V
    kern = functools.partial(_gdn_kernel, TB=TB, C=C)
    return pl.pallas_call(
        kern,
        grid=(B, nT),
        in_specs=[
            pl.BlockSpec((TB, W), lambda b, t: (b * nT + t, 0)),
            pl.BlockSpec((TB, LA_V), lambda b, t: (b * nT + t, W // LA_V)),
            pl.BlockSpec((TB, BA_PAD), lambda b, t: (b * nT + t, (W + LA_V) // BA_PAD)),
            pl.BlockSpec((CONV_K, W), lambda b, t: (0, 0)),
            pl.BlockSpec((1, BA_PAD), lambda b, t: (0, 0)),
            pl.BlockSpec((1, BA_PAD), lambda b, t: (0, 0)),
            pl.BlockSpec((1, LA_DV), lambda b, t: (0, 0)),
        ],
        out_specs=pl.BlockSpec((TB, LA_V), lambda b, t: (b * nT + t, 0)),
        out_shape=jax.ShapeDtypeStruct((M, LA_V), BF16),
        scratch_shapes=[
            pltpu.VMEM((TB + SUBLANES, W), F32),
            pltpu.VMEM((LA_HEADS, LA_DK, LA_DV), F32),
            pltpu.VMEM((TB, W), F32),
            pltpu.VMEM((TB, BA_PAD), F32),
            pltpu.VMEM((LANES, TB), F32),
            pltpu.VMEM((TB, BA_PAD), F32),
        ],
        compiler_params=pltpu.CompilerParams(
            dimension_semantics=("parallel", "arbitrary"), vmem_limit_bytes=VMEM_LIMIT),
        name="gdn",
    )(la, la, la, convw, avec, dtvec, nw.reshape(1, LA_DV))


def _attn_kernel(lam_ref, tab_ref, q_ref, k_ref, v_ref, nw_ref, o_ref, bias_s, *, tq, lam_init):
    i = pl.program_id(2)

    @pl.when(i == 0)
    def _():
        for e in range(2):
            x = jnp.broadcast_to(tab_ref[0, e], (tq, 2 * tq))
            bias_s[e] = pltpu.roll(x, 0, 1, stride=1, stride_axis=0)[:, :tq]

    q = q_ref[0]
    lane = lax.broadcasted_iota(jnp.int32, q.shape, 1)
    zero = jnp.zeros_like(q)
    q2 = jnp.concatenate([jnp.where(lane < DIFF_DH, q, zero),
                          jnp.where(lane >= DIFF_DH, q, zero)], axis=0)

    def step(start, nk, carry, bias):
        m, l, acc = carry
        kj = k_ref[0, pl.ds(start, nk), :]
        vj = v_ref[0, pl.ds(start, nk), :]
        s = _dot_nt(q2, kj)
        if bias is not None:
            s = (s.reshape(2, tq, nk) + bias[None]).reshape(2 * tq, nk)
        m_new = jnp.maximum(m, jnp.max(s, axis=-1, keepdims=True))
        alpha = jnp.exp2(m - m_new)
        p = jnp.exp2(s - m_new)
        l = alpha * l + jnp.sum(p, axis=-1, keepdims=True)
        acc = alpha * acc + _dot(p.astype(BF16), vj)
        return m_new, l, acc

    carry = (jnp.full((2 * tq, 1), NEG_BIG, F32), jnp.zeros((2 * tq, 1), F32),
             jnp.zeros((2 * tq, DIFF_DV), F32))
    n_far = jnp.maximum(i - 1, 0)
    carry = lax.fori_loop(
        0, n_far, lambda j, c: step(pl.multiple_of(j * tq, tq), tq, c, None), carry)
    first = jnp.where(i == 0, bias_s[0], bias_s[1])
    second = jnp.where(i == 0, NEG_BIG, bias_s[0])
    m, l, acc = step(pl.multiple_of(n_far * tq, tq), 2 * tq, carry,
                     jnp.concatenate([first, second], axis=1))

    lam = (jnp.exp(jnp.sum(lam_ref[0:1, :] * lam_ref[1:2, :], axis=-1, keepdims=True))
           - jnp.exp(jnp.sum(lam_ref[2:3, :] * lam_ref[3:4, :], axis=-1, keepdims=True)) + lam_init)
    o = acc[:tq] / l[:tq] - lam * (acc[tq:] / l[tq:])
    o_ref[0] = (_rms(o, nw_ref[...], DIFF_NORM_EPS) * (1.0 - lam_init)).astype(BF16)


def _attn(dd, lams, bias_tab, nw, B, T, tq, lam_init):
    assert T >= 2 * tq and T % tq == 0
    nq = T // tq
    dd3 = dd.reshape(B, T, dd.shape[1])
    kern = functools.partial(_attn_kernel, tq=tq, lam_init=lam_init)
    out = pl.pallas_call(
        kern,
        grid=(B, DIFF_HEADS, nq),
        in_specs=[
            pl.BlockSpec((4, DIFF_DH), lambda b, h, i: (0, 0)),
            pl.BlockSpec((1, 2, 1, 2 * tq), lambda b, h, i: (h, 0, 0, 0)),
            pl.BlockSpec((1, tq, LANES), lambda b, h, i: (b, i, h)),
            pl.BlockSpec((1, T, LANES), lambda b, h, i: (b, 0, DIFF_HEADS + h)),
            pl.BlockSpec((1, T, LANES), lambda b, h, i: (b, 0, 2 * DIFF_HEADS + h)),
            pl.BlockSpec((1, DIFF_DV), lambda b, h, i: (0, 0)),
        ],
        out_specs=pl.BlockSpec((1, tq, DIFF_DV), lambda b, h, i: (b, i, h)),
        out_shape=jax.ShapeDtypeStruct((B, T, DIFF_V), BF16),
        scratch_shapes=[pltpu.VMEM((2, tq, tq), F32)],
        compiler_params=pltpu.CompilerParams(
            dimension_semantics=("parallel", "parallel", "arbitrary"),
            vmem_limit_bytes=VMEM_LIMIT),
        name="diff_attn",
    )(lams, bias_tab, dd3, dd3, dd3, nw.reshape(1, DIFF_DV))
    return out.reshape(B * T, DIFF_V)


def _t5_bucket(rel):
    n = jnp.maximum(rel, 0)
    max_exact = NUM_BUCKETS // 2
    nf = jnp.maximum(n, 1).astype(F32)
    large = max_exact + (jnp.log(nf / max_exact) / math.log(MAX_DISTANCE / max_exact)
                         * (NUM_BUCKETS - max_exact)).astype(jnp.int32)
    large = jnp.minimum(large, NUM_BUCKETS - 1)
    return jnp.where(n < max_exact, n, large)


def _bias_tables(rel_bias, tq):
    assert tq + 1 >= MAX_DISTANCE
    m = jnp.arange(2 * tq)
    far = rel_bias[NUM_BUCKETS - 1].astype(F32)
    tabs = []
    for d in (0, tq):
        rel = jnp.where(m < tq, d - m, d + 2 * tq - m)
        b = (rel_bias[_t5_bucket(rel)].astype(F32) - far) * math.log2(math.e)
        tabs.append(jnp.where((rel >= 0)[:, None], b, NEG_BIG).T)
    return jnp.stack(tabs, axis=1)[:, :, None, :]


def _ffn_kernel(x_ref, ola_ref, od_ref, wout_ref, nw_ref, wg_ref, wu_ref, wd_ref, fnw_ref,
                out_ref, h_s, acc_s, *, final_norm):
    f = pl.program_id(1)

    @pl.when(f == 0)
    def _():
        x1 = (x_ref[...] + _dot(ola_ref[...], wout_ref[0:LA_V, :])
              + _dot(od_ref[...], wout_ref[LA_V:LA_V + DIFF_V, :]))
        acc_s[...] = x1
        h_s[...] = _rms(x1, nw_ref[...], NORM_EPS).astype(BF16)

    h = h_s[...]
    a = (_silu(_dot(h, wg_ref[...])) * _dot(h, wu_ref[...])).astype(BF16)
    acc_s[...] += _dot(a, wd_ref[...])

    @pl.when(f == pl.num_programs(1) - 1)
    def _():
        y = acc_s[...]
        if final_norm:
            y = _rms(y, fnw_ref[...], NORM_EPS)
        out_ref[...] = y


def _ffn(xf, o_la, o_d, wout, nw, wgu, wdn, fnw, tm, tf, final_norm):
    M, D = xf.shape
    F = wdn.shape[0]
    nf = F // tf
    kern = functools.partial(_ffn_kernel, final_norm=final_norm)
    return pl.pallas_call(
        kern,
        grid=(M // tm, nf),
        in_specs=[
            pl.BlockSpec((tm, D), lambda i, f: (i, 0)),
            pl.BlockSpec((tm, LA_V), lambda i, f: (i, 0)),
            pl.BlockSpec((tm, DIFF_V), lambda i, f: (i, 0)),
            pl.BlockSpec(wout.shape, lambda i, f: (0, 0)),
            pl.BlockSpec((1, D), lambda i, f: (0, 0)),
            pl.BlockSpec((D, tf), lambda i, f: (0, f)),
            pl.BlockSpec((D, tf), lambda i, f: (0, nf + f)),
            pl.BlockSpec((tf, D), lambda i, f: (f, 0)),
            pl.BlockSpec((1, D), lambda i, f: (0, 0)),
        ],
        out_specs=pl.BlockSpec((tm, D), lambda i, f: (i, 0)),
        out_shape=jax.ShapeDtypeStruct((M, D), F32),
        scratch_shapes=[pltpu.VMEM((tm, D), BF16), pltpu.VMEM((tm, D), F32)],
        compiler_params=pltpu.CompilerParams(
            dimension_semantics=("parallel", "arbitrary"), vmem_limit_bytes=VMEM_LIMIT),
        name="out_proj_ffn",
    )(xf, o_la, o_d, wout, nw.reshape(1, D), wgu, wgu, wdn, fnw.reshape(1, D))


def _pick(n, pref):
    return pref if n % pref == 0 else n


def kernel(x, attn_norm_w, w_in, conv_w, a_log, dt_bias, la_norm_w, lambda_q1, lambda_k1,
           lambda_q2, lambda_k2, diff_norm_w, rel_bias, w_out, ffn_norm_w, w_gate_up,
           w_down, final_norm_w):
    B, T, D = x.shape
    depth = w_in.shape[0]
    M = B * T
    d_ff = w_down.shape[1]
    tm_proj = _pick(M, 512)
    tm_ffn = _pick(M, 1024)
    tf = _pick(d_ff, 256)
    tb = _pick(T, 256)
    chunk = 64
    tq = _pick(T, 512)

    n_la = 2 * LA_QK + 2 * LA_V
    bias_tab = _bias_tables(rel_bias, tq)
    lane_pad = BA_PAD - 2 * LA_HEADS
    dscale = jnp.concatenate([jnp.full((DIFF_QK,), DIFF_DH ** -0.5 * math.log2(math.e), F32),
                              jnp.ones((DIFF_QK + DIFF_V,), F32)]).reshape(1, -1)

    xf = x.reshape(M, D)
    for l in range(depth):
        w = w_in[l]
        wla = jnp.concatenate(
            [w[:, :n_la + 2 * LA_HEADS], jnp.zeros((D, lane_pad), w.dtype)], axis=1).astype(BF16)
        wd = w[:, n_la + 2 * LA_HEADS:].astype(BF16)
        la, dd = _in_proj(xf, attn_norm_w[l], wla, wd, dscale, tm_proj)

        avec = jnp.concatenate([jnp.zeros((LA_HEADS,), F32), a_log[l].astype(F32),
                                jnp.zeros((lane_pad,), F32)]).reshape(1, BA_PAD)
        dtvec = jnp.concatenate([jnp.zeros((LA_HEADS,), F32), dt_bias[l].astype(F32),
                                 jnp.zeros((lane_pad,), F32)]).reshape(1, BA_PAD)
        o_la = _gdn(la, conv_w[l], avec, dtvec, la_norm_w[l], B, T, tb, chunk)

        lam_init = 0.8 - 0.6 * math.exp(-0.3 * l)
        lams = jnp.stack([lambda_q1[l], lambda_k1[l], lambda_q2[l], lambda_k2[l]]).astype(F32)
        o_d = _attn(dd, lams, bias_tab, diff_norm_w[l], B, T, tq, lam_init)

        xf = _ffn(xf, o_la, o_d, w_out[l].astype(BF16), ffn_norm_w[l],
                  w_gate_up[l].astype(BF16), w_down[l].astype(BF16), final_norm_w,
                  tm_ffn, tf, final_norm=(l == depth - 1))
    return xf.reshape(B, T, D)
```

```python
import functools
import math

import jax
import jax.numpy as jnp
from jax import lax
from jax.experimental import pallas as pl
from jax.experimental.pallas import tpu as pltpu

LA_HEADS = 4
LA_DK = 128
LA_DV = 128
LA_QK = LA_HEADS * LA_DK
LA_V = LA_HEADS * LA_DV
CONV_K = 4
DIFF_HEADS = 4
DIFF_DH = 64
DIFF_DV = 2 * DIFF_DH
DIFF_QK = DIFF_HEADS * 2 * DIFF_DH
DIFF_V = DIFF_HEADS * DIFF_DV
NUM_BUCKETS = 32
MAX_DISTANCE = 128
NORM_EPS = 1e-6
DIFF_NORM_EPS = 1e-5
L2_EPS = 1e-6

LANES = 128
SUBLANES = 8
BA_PAD = LANES
LA_WIDTH = 2 * LA_QK + 2 * LA_V + BA_PAD
NEG_BIG = -1e30
FAR_UNROLLS = (4, 2, 1)

VMEM_LIMIT = 52 * 1024 * 1024

F32 = jnp.float32
BF16 = jnp.bfloat16
HIGHEST = lax.Precision.HIGHEST


def _dot(a, b, precision=None):
    return jnp.dot(a, b, preferred_element_type=F32, precision=precision)


def _dot_nt(a, b, precision=None):
    return lax.dot_general(a, b, (((1,), (1,)), ((), ())),
                           preferred_element_type=F32, precision=precision)


def _dot_tn(a, b, precision=None):
    return lax.dot_general(a, b, (((0,), (0,)), ((), ())),
                           preferred_element_type=F32, precision=precision)


def _rms(x, w, eps):
    return x * lax.rsqrt(jnp.mean(x * x, axis=-1, keepdims=True) + eps) * w


def _silu(x):
    return x * jax.nn.sigmoid(x)


def _softplus(x):
    return jnp.maximum(x, 0.0) + jnp.log1p(jnp.exp(-jnp.abs(x)))


def _in_proj_kernel(x_ref, nw_ref, wla_ref, wd_ref, dscale_ref, la_ref, d_ref):
    h = _rms(x_ref[...], nw_ref[...], NORM_EPS).astype(BF16)
    la_ref[...] = _dot(h, wla_ref[...])
    d_ref[...] = (_dot(h, wd_ref[...]) * dscale_ref[...]).astype(BF16)


def _in_proj(xf, nw, wla, wd, dscale, tm):
    M, D = xf.shape
    return pl.pallas_call(
        _in_proj_kernel,
        grid=(M // tm,),
        in_specs=[
            pl.BlockSpec((tm, D), lambda i: (i, 0)),
            pl.BlockSpec((1, D), lambda i: (0, 0)),
            pl.BlockSpec(wla.shape, lambda i: (0, 0)),
            pl.BlockSpec(wd.shape, lambda i: (0, 0)),
            pl.BlockSpec(dscale.shape, lambda i: (0, 0)),
        ],
        out_specs=[
            pl.BlockSpec((tm, wla.shape[1]), lambda i: (i, 0)),
            pl.BlockSpec((tm, wd.shape[1]), lambda i: (i, 0)),
        ],
        out_shape=[
            jax.ShapeDtypeStruct((M, wla.shape[1]), F32),
            jax.ShapeDtypeStruct((M, wd.shape[1]), BF16),
        ],
        compiler_params=pltpu.CompilerParams(
            dimension_semantics=("parallel",), vmem_limit_bytes=VMEM_LIMIT),
        name="in_proj",
    )(xf, nw.reshape(1, D), wla, wd, dscale)


def _unit_lower_inverse_minus_eye(Ls, C):
    row = lax.broadcasted_iota(jnp.int32, (C, C), 0)
    col = lax.broadcasted_iota(jnp.int32, (C, C), 1)
    xs = None
    s = 1
    while s < C:
        same_pair = (row // (2 * s)) == (col // (2 * s))
        low_left = same_pair & ((row // s) % 2 == 1) & ((col // s) % 2 == 0)
        offs = [jnp.where(low_left, L, 0.0) for L in Ls]
        if xs is None:
            xs = [-m for m in offs]
        else:
            xb = [x.astype(BF16) for x in xs]
            ys = [m + _dot(m.astype(BF16), x) for m, x in zip(offs, xb)]
            xs = [x - y - _dot(x16, y.astype(BF16)) for x, x16, y in zip(xs, xb, ys)]
        s *= 2
    return xs


def _gdn_kernel(qkv_ref, z_ref, ba_ref, convw_ref, avec_ref, dtvec_ref, nw_ref,
                o_ref, xbuf, state, y_s, g_s, gt_s, beta_s, *, TB, C):
    t = pl.program_id(1)
    W = 2 * LA_QK + LA_V

    @pl.when(t == 0)
    def _():
        xbuf[0:SUBLANES, :] = jnp.zeros((SUBLANES, W), F32)
        state[...] = jnp.zeros(state.shape, F32)

    xbuf[SUBLANES:SUBLANES + TB, :] = qkv_ref[...]
    conv = convw_ref[CONV_K - 1:CONV_K, :] * xbuf[SUBLANES:SUBLANES + TB, :]
    for i in range(CONV_K - 1):
        off = SUBLANES - (CONV_K - 1) + i
        conv = conv + convw_ref[i:i + 1, :] * xbuf[off:off + TB, :]
    xbuf[0:SUBLANES, :] = xbuf[TB:TB + SUBLANES, :]
    y = _silu(conv)

    for h in range(LA_HEADS):
        qs = slice(h * LA_DK, (h + 1) * LA_DK)
        ks = slice(LA_QK + h * LA_DK, LA_QK + (h + 1) * LA_DK)
        q = y[:, qs]
        k = y[:, ks]
        y_s[:, qs] = q * lax.rsqrt(jnp.sum(q * q, -1, keepdims=True) + L2_EPS) * (LA_DK ** -0.5)
        y_s[:, ks] = k * lax.rsqrt(jnp.sum(k * k, -1, keepdims=True) + L2_EPS)
    y_s[:, 2 * LA_QK:] = y[:, 2 * LA_QK:]

    ba = ba_ref[...]
    beta_s[...] = jax.nn.sigmoid(ba)
    ba_lane = lax.broadcasted_iota(jnp.int32, (1, BA_PAD), 1)
    is_g = (ba_lane >= LA_HEADS) & (ba_lane < 2 * LA_HEADS)
    g_raw = -jnp.where(is_g, jnp.exp(avec_ref[...]), 0.0) * _softplus(ba + dtvec_ref[...])

    row = lax.broadcasted_iota(jnp.int32, (C, C), 0)
    col = lax.broadcasted_iota(jnp.int32, (C, C), 1)
    causal = row >= col
    strict = row > col
    tril = causal.astype(F32)
    nC = TB // C
    for c in range(nC):
        g_s[c * C:(c + 1) * C, :] = _dot(tril, g_raw[c * C:(c + 1) * C, :], HIGHEST)
    for j in range(TB // LANES):
        gt_s[:, j * LANES:(j + 1) * LANES] = g_s[j * LANES:(j + 1) * LANES, :].T

    probs = [(c, h) for c in range(nC) for h in range(LA_HEADS)]
    Ls, intras, rhss, qgs, kds, egls = [], [], [], [], [], []
    for c, h in probs:
        rs = slice(c * C, (c + 1) * C)
        qn = y_s[rs, h * LA_DK:(h + 1) * LA_DK]
        kn = y_s[rs, LA_QK + h * LA_DK:LA_QK + (h + 1) * LA_DK]
        v = y_s[rs, 2 * LA_QK + h * LA_DV:2 * LA_QK + (h + 1) * LA_DV]
        gb = jnp.broadcast_to(g_s[rs, LA_HEADS + h:LA_HEADS + h + 1], (C, LANES))
        beta = jnp.broadcast_to(beta_s[rs, h:h + 1], (C, LANES))
        g_row = gt_s[LA_HEADS + h:LA_HEADS + h + 1, rs]
        gdiff = gb[:, :C] - g_row
        decay = jnp.where(causal, jnp.exp(jnp.where(causal, gdiff, 0.0)), 0.0)
        kb = kn * beta
        kq = _dot_nt(jnp.concatenate([kb, qn], axis=0).astype(BF16), kn.astype(BF16))
        Ls.append(jnp.where(strict, kq[:C] * decay, 0.0))
        intras.append((kq[C:] * decay).astype(BF16))
        eg = jnp.exp(gb)
        rhss.append(jnp.concatenate([v * beta, kb * eg], axis=1))
        qgs.append((qn * eg).astype(BF16))
        g_last = gb[C - 1:C, :]
        kds.append((kn * jnp.exp(g_last - gb)).astype(BF16))
        egls.append(jnp.exp(g_last))

    xs = _unit_lower_inverse_minus_eye(Ls, C)
    sols = [r + _dot(x.astype(BF16), r.astype(BF16)) for x, r in zip(xs, rhss)]

    S = [state[h] for h in range(LA_HEADS)]
    for c in range(nC):
        ps = [c * LA_HEADS + h for h in range(LA_HEADS)]
        r2 = [_dot(jnp.concatenate([sols[p][:, LA_DV:].astype(BF16), qgs[p]], axis=0),
                   S[h].astype(BF16)) for h, p in enumerate(ps)]
        vn = [(sols[p][:, :LA_DV] - r2[h][:C]).astype(BF16) for h, p in enumerate(ps)]
        S = [S[h] * egls[p] + _dot_tn(kds[p], vn[h]) for h, p in enumerate(ps)]
        rs = slice(c * C, (c + 1) * C)
        for h, p in enumerate(ps):
            o = r2[h][C:] + _dot(intras[p], vn[h])
            zg = _silu(z_ref[rs, h * LA_DV:(h + 1) * LA_DV])
            o_ref[rs, h * LA_DV:(h + 1) * LA_DV] = (_rms(o, nw_ref[...], NORM_EPS) * zg).astype(BF16)
    for h in range(LA_HEADS):
        state[h] = S[h]


def _gdn(la, convw, avec, dtvec, nw, B, T, TB, C):
    M = B * T
    nT = T // TB
    W = 2 * LA_QK + LA_V
    kern = functools.partial(_gdn_kernel, TB=TB, C=C)
    return pl.pallas_call(
        kern,
        grid=(B, nT),
        in_specs=[
            pl.BlockSpec((TB, W), lambda b, t: (b * nT + t, 0)),
            pl.BlockSpec((TB, LA_V), lambda b, t: (b * nT + t, W // LA_V)),
            pl.BlockSpec((TB, BA_PAD), lambda b, t: (b * nT + t, (W + LA_V) // BA_PAD)),
            pl.BlockSpec((CONV_K, W), lambda b, t: (0, 0)),
            pl.BlockSpec((1, BA_PAD), lambda b, t: (0, 0)),
            pl.BlockSpec((1, BA_PAD), lambda b, t: (0, 0)),
            pl.BlockSpec((1, LA_DV), lambda b, t: (0, 0)),
        ],
        out_specs=pl.BlockSpec((TB, LA_V), lambda b, t: (b * nT + t, 0)),
        out_shape=jax.ShapeDtypeStruct((M, LA_V), BF16),
        scratch_shapes=[
            pltpu.VMEM((TB + SUBLANES, W), F32),
            pltpu.VMEM((LA_HEADS, LA_DK, LA_DV), F32),
            pltpu.VMEM((TB, W), F32),
            pltpu.VMEM((TB, BA_PAD), F32),
            pltpu.VMEM((LANES, TB), F32),
            pltpu.VMEM((TB, BA_PAD), F32),
        ],
        compiler_params=pltpu.CompilerParams(
            dimension_semantics=("parallel", "arbitrary"), vmem_limit_bytes=VMEM_LIMIT),
        name="gdn",
    )(la, la, la, convw, avec, dtvec, nw.reshape(1, LA_DV))


def _transpose_blocks(x):
    n = x.shape[0] // LANES
    return jnp.concatenate([x[j * LANES:(j + 1) * LANES, :].T for j in range(n)], axis=1)


def _attn_kernel(lam_ref, tab_ref, q_ref, k_ref, v_ref, nw_ref, o_ref, bias_s, vt_s, *,
                 tq, lam_init):
    i = pl.program_id(2)
    n_blocks = vt_s.shape[0]

    @pl.when(i == 0)
    def _():
        for e in range(2):
            x = jnp.broadcast_to(tab_ref[0, e], (tq, 2 * tq))
            bias_s[e] = pltpu.roll(x, 0, 1, stride=1, stride_axis=0)[:, :tq]
        for j in range(n_blocks):
            vt_s[j] = _transpose_blocks(v_ref[0, j * tq:(j + 1) * tq, :].astype(F32)).astype(BF16)

    qt = _transpose_blocks(q_ref[0].astype(F32))
    sub = lax.broadcasted_iota(jnp.int32, qt.shape, 0)
    q2t = jnp.concatenate([jnp.where(sub < DIFF_DH, qt, 0.0),
                           jnp.where(sub >= DIFF_DH, qt, 0.0)], axis=1).astype(BF16)

    def scores(j):
        kj = k_ref[0, pl.ds(pl.multiple_of(j * tq, tq), tq), :]
        return _dot(kj, q2t)

    def biased(s, bias):
        return jnp.concatenate([s[:, :tq] + bias, s[:, tq:] + bias], axis=1)

    def update(carry, ss, js):
        m, l, acc = carry
        m_new = m
        for s in ss:
            m_new = jnp.maximum(m_new, jnp.max(s, axis=0, keepdims=True))
        alpha = jnp.exp2(m - m_new)
        l = alpha * l
        acc = alpha * acc
        for s, j in zip(ss, js):
            p = jnp.exp2(s - m_new)
            l = l + jnp.sum(p, axis=0, keepdims=True)
            acc = acc + _dot(vt_s[j], p.astype(BF16))
        return m_new, l, acc

    carry = (jnp.full((1, 2 * tq), NEG_BIG, F32), jnp.zeros((1, 2 * tq), F32),
             jnp.zeros((DIFF_DV, 2 * tq), F32))
    n_far = jnp.maximum(i - 1, 0)

    done = 0
    for width in FAR_UNROLLS:
        def body(g, c, width=width, done=done):
            js = [done + width * g + u for u in range(width)]
            ss = [scores(j) for j in js]
            for s, j in zip(ss, js):
                c = update(c, [s], [j])
            return c
        n_groups = (n_far - done) // width
        carry = lax.fori_loop(0, n_groups, body, carry)
        done = done + n_groups * width
    first = jnp.where(i == 0, bias_s[0], bias_s[1])
    second = jnp.where(i == 0, NEG_BIG, bias_s[0])
    s_first = biased(scores(n_far), first)
    s_second = biased(scores(n_far + 1), second)
    carry = update(carry, [s_first], [n_far])
    m, l, acc = update(carry, [s_second], [n_far + 1])

    lam = (jnp.exp(jnp.sum(lam_ref[0:1, :] * lam_ref[1:2, :], axis=-1, keepdims=True))
           - jnp.exp(jnp.sum(lam_ref[2:3, :] * lam_ref[3:4, :], axis=-1, keepdims=True)) + lam_init)
    ot = acc[:, :tq] / l[:, :tq] - lam * (acc[:, tq:] / l[:, tq:])
    o = jnp.concatenate([ot[:, j * LANES:(j + 1) * LANES].T for j in range(tq // LANES)], axis=0)
    o_ref[0] = (_rms(o, nw_ref[...], DIFF_NORM_EPS) * (1.0 - lam_init)).astype(BF16)


def _attn(dd, lams, bias_tab, nw, B, T, tq, lam_init):
    assert T >= 2 * tq and T % tq == 0
    nq = T // tq
    dd3 = dd.reshape(B, T, dd.shape[1])
    kern = functools.partial(_attn_kernel, tq=tq, lam_init=lam_init)
    out = pl.pallas_call(
        kern,
        grid=(B, DIFF_HEADS, nq),
        in_specs=[
            pl.BlockSpec((4, DIFF_DH), lambda b, h, i: (0, 0)),
            pl.BlockSpec((1, 2, 1, 2 * tq), lambda b, h, i: (h, 0, 0, 0)),
            pl.BlockSpec((1, tq, LANES), lambda b, h, i: (b, i, h)),
            pl.BlockSpec((1, T, LANES), lambda b, h, i: (b, 0, DIFF_HEADS + h)),
            pl.BlockSpec((1, T, LANES), lambda b, h, i: (b, 0, 2 * DIFF_HEADS + h)),
            pl.BlockSpec((1, DIFF_DV), lambda b, h, i: (0, 0)),
        ],
        out_specs=pl.BlockSpec((1, tq, DIFF_DV), lambda b, h, i: (b, i, h)),
        out_shape=jax.ShapeDtypeStruct((B, T, DIFF_V), BF16),
        scratch_shapes=[pltpu.VMEM((2, tq, tq), F32),
                        pltpu.VMEM((nq, DIFF_DV, tq), BF16)],
        compiler_params=pltpu.CompilerParams(
            dimension_semantics=("parallel", "parallel", "arbitrary"),
            vmem_limit_bytes=VMEM_LIMIT),
        name="diff_attn",
    )(lams, bias_tab, dd3, dd3, dd3, nw.reshape(1, DIFF_DV))
    return out.reshape(B * T, DIFF_V)


def _t5_bucket(rel):
    n = jnp.maximum(rel, 0)
    max_exact = NUM_BUCKETS // 2
    nf = jnp.maximum(n, 1).astype(F32)
    large = max_exact + (jnp.log(nf / max_exact) / math.log(MAX_DISTANCE / max_exact)
                         * (NUM_BUCKETS - max_exact)).astype(jnp.int32)
    large = jnp.minimum(large, NUM_BUCKETS - 1)
    return jnp.where(n < max_exact, n, large)


def _bias_tables(rel_bias, tq):
    assert tq + 1 >= MAX_DISTANCE
    m = jnp.arange(2 * tq)
    far = rel_bias[NUM_BUCKETS - 1].astype(F32)
    tabs = []
    for d in (0, tq):
        rel = jnp.where(m < tq, d + m, d + m - 2 * tq)
        b = (rel_bias[_t5_bucket(rel)].astype(F32) - far) * math.log2(math.e)
        tabs.append(jnp.where((rel >= 0)[:, None], b, NEG_BIG).T)
    return jnp.stack(tabs, axis=1)[:, :, None, :]


def _ffn_kernel(x_ref, ola_ref, od_ref, wout_ref, nw_ref, wg_ref, wu_ref, wd_ref, fnw_ref,
                out_ref, h_s, acc_s, *, final_norm):
    f = pl.program_id(1)

    @pl.when(f == 0)
    def _():
        x1 = (x_ref[...] + _dot(ola_ref[...], wout_ref[0:LA_V, :])
              + _dot(od_ref[...], wout_ref[LA_V:LA_V + DIFF_V, :]))
        acc_s[...] = x1
        h_s[...] = _rms(x1, nw_ref[...], NORM_EPS).astype(BF16)

    h = h_s[...]
    a = (_silu(_dot(h, wg_ref[...])) * _dot(h, wu_ref[...])).astype(BF16)
    acc_s[...] += _dot(a, wd_ref[...])

    @pl.when(f == pl.num_programs(1) - 1)
    def _():
        y = acc_s[...]
        if final_norm:
            y = _rms(y, fnw_ref[...], NORM_EPS)
        out_ref[...] = y


def _ffn(xf, o_la, o_d, wout, nw, wgu, wdn, fnw, tm, tf, final_norm):
    M, D = xf.shape
    F = wdn.shape[0]
    nf = F // tf
    kern = functools.partial(_ffn_kernel, final_norm=final_norm)
    return pl.pallas_call(
        kern,
        grid=(M // tm, nf),
        in_specs=[
            pl.BlockSpec((tm, D), lambda i, f: (i, 0)),
            pl.BlockSpec((tm, LA_V), lambda i, f: (i, 0)),
            pl.BlockSpec((tm, DIFF_V), lambda i, f: (i, 0)),
            pl.BlockSpec(wout.shape, lambda i, f: (0, 0)),
            pl.BlockSpec((1, D), lambda i, f: (0, 0)),
            pl.BlockSpec((D, tf), lambda i, f: (0, f)),
            pl.BlockSpec((D, tf), lambda i, f: (0, nf + f)),
            pl.BlockSpec((tf, D), lambda i, f: (f, 0)),
            pl.BlockSpec((1, D), lambda i, f: (0, 0)),
        ],
        out_specs=pl.BlockSpec((tm, D), lambda i, f: (i, 0)),
        out_shape=jax.ShapeDtypeStruct((M, D), F32),
        scratch_shapes=[pltpu.VMEM((tm, D), BF16), pltpu.VMEM((tm, D), F32)],
        compiler_params=pltpu.CompilerParams(
            dimension_semantics=("parallel", "arbitrary"), vmem_limit_bytes=VMEM_LIMIT),
        name="out_proj_ffn",
    )(xf, o_la, o_d, wout, nw.reshape(1, D), wgu, wgu, wdn, fnw.reshape(1, D))


def _pick(n, pref):
    return pref if n % pref == 0 else n


def kernel(x, attn_norm_w, w_in, conv_w, a_log, dt_bias, la_norm_w, lambda_q1, lambda_k1,
           lambda_q2, lambda_k2, diff_norm_w, rel_bias, w_out, ffn_norm_w, w_gate_up,
           w_down, final_norm_w):
    B, T, D = x.shape
    depth = w_in.shape[0]
    M = B * T
    d_ff = w_down.shape[1]
    tm_proj = _pick(M, 512)
    tm_ffn = _pick(M, 1024)
    tf = _pick(d_ff, 256)
    tb = _pick(T, 256)
    chunk = 64
    tq = _pick(T, 512)

    n_la = 2 * LA_QK + 2 * LA_V
    bias_tab = _bias_tables(rel_bias, tq)
    lane_pad = BA_PAD - 2 * LA_HEADS
    dscale = jnp.concatenate([jnp.full((DIFF_QK,), DIFF_DH ** -0.5 * math.log2(math.e), F32),
                              jnp.ones((DIFF_QK + DIFF_V,), F32)]).reshape(1, -1)

    xf = x.reshape(M, D)
    for l in range(depth):
        w = w_in[l]
        wla = jnp.concatenate(
            [w[:, :n_la + 2 * LA_HEADS], jnp.zeros((D, lane_pad), w.dtype)], axis=1).astype(BF16)
        wd = w[:, n_la + 2 * LA_HEADS:].astype(BF16)
        la, dd = _in_proj(xf, attn_norm_w[l], wla, wd, dscale, tm_proj)

        avec = jnp.concatenate([jnp.zeros((LA_HEADS,), F32), a_log[l].astype(F32),
                                jnp.zeros((lane_pad,), F32)]).reshape(1, BA_PAD)
        dtvec = jnp.concatenate([jnp.zeros((LA_HEADS,), F32), dt_bias[l].astype(F32),
                                 jnp.zeros((lane_pad,), F32)]).reshape(1, BA_PAD)
        o_la = _gdn(la, conv_w[l], avec, dtvec, la_norm_w[l], B, T, tb, chunk)

        lam_init = 0.8 - 0.6 * math.exp(-0.3 * l)
        lams = jnp.stack([lambda_q1[l], lambda_k1[l], lambda_q2[l], lambda_k2[l]]).astype(F32)
        o_d = _attn(dd, lams, bias_tab, diff_norm_w[l], B, T, tq, lam_init)

        xf = _ffn(xf, o_la, o_d, w_out[l].astype(BF16), ffn_norm_w[l],
                  w_gate_up[l].astype(BF16), w_down[l].astype(BF16), final_norm_w,
                  tm_ffn, tf, final_norm=(l == depth - 1))
    return xf.reshape(B, T, D)
```

```python
import functools
import math

import jax
import jax.numpy as jnp
from jax import lax
from jax.experimental import pallas as pl
from jax.experimental.pallas import tpu as pltpu

LA_HEADS = 4
LA_DK = 128
LA_DV = 128
LA_QK = LA_HEADS * LA_DK
LA_V = LA_HEADS * LA_DV
CONV_K = 4
DIFF_HEADS = 4
DIFF_DH = 64
DIFF_DV = 2 * DIFF_DH
DIFF_QK = DIFF_HEADS * 2 * DIFF_DH
DIFF_V = DIFF_HEADS * DIFF_DV
NUM_BUCKETS = 32
MAX_DISTANCE = 128
NORM_EPS = 1e-6
DIFF_NORM_EPS = 1e-5
L2_EPS = 1e-6

LANES = 128
SUBLANES = 8
BA_PAD = LANES
LA_WIDTH = 2 * LA_QK + 2 * LA_V + BA_PAD
NEG_BIG = -1e30
FAR_UNROLLS = (4, 2, 1)
MAX_LEAD = 40.0

VMEM_LIMIT = 52 * 1024 * 1024

F32 = jnp.float32
BF16 = jnp.bfloat16
HIGHEST = lax.Precision.HIGHEST


def _dot(a, b, precision=None):
    return jnp.dot(a, b, preferred_element_type=F32, precision=precision)


def _dot_nt(a, b, precision=None):
    return lax.dot_general(a, b, (((1,), (1,)), ((), ())),
                           preferred_element_type=F32, precision=precision)


def _dot_tn(a, b, precision=None):
    return lax.dot_general(a, b, (((0,), (0,)), ((), ())),
                           preferred_element_type=F32, precision=precision)


def _rms(x, w, eps):
    return x * lax.rsqrt(jnp.mean(x * x, axis=-1, keepdims=True) + eps) * w


def _silu(x):
    return x * jax.nn.sigmoid(x)


def _softplus(x):
    return jnp.maximum(x, 0.0) + jnp.log1p(jnp.exp(-jnp.abs(x)))


def _in_proj_kernel(x_ref, nw_ref, wla_ref, wd_ref, dscale_ref, la_ref, d_ref):
    h = _rms(x_ref[...], nw_ref[...], NORM_EPS).astype(BF16)
    la_ref[...] = _dot(h, wla_ref[...])
    d_ref[...] = (_dot(h, wd_ref[...]) * dscale_ref[...]).astype(BF16)


def _in_proj(xf, nw, wla, wd, dscale, tm):
    M, D = xf.shape
    return pl.pallas_call(
        _in_proj_kernel,
        grid=(M // tm,),
        in_specs=[
            pl.BlockSpec((tm, D), lambda i: (i, 0)),
            pl.BlockSpec((1, D), lambda i: (0, 0)),
            pl.BlockSpec(wla.shape, lambda i: (0, 0)),
            pl.BlockSpec(wd.shape, lambda i: (0, 0)),
            pl.BlockSpec(dscale.shape, lambda i: (0, 0)),
        ],
        out_specs=[
            pl.BlockSpec((tm, wla.shape[1]), lambda i: (i, 0)),
            pl.BlockSpec((tm, wd.shape[1]), lambda i: (i, 0)),
        ],
        out_shape=[
            jax.ShapeDtypeStruct((M, wla.shape[1]), F32),
            jax.ShapeDtypeStruct((M, wd.shape[1]), BF16),
        ],
        compiler_params=pltpu.CompilerParams(
            dimension_semantics=("parallel",), vmem_limit_bytes=VMEM_LIMIT),
        name="in_proj",
    )(xf, nw.reshape(1, D), wla, wd, dscale)


def _unit_lower_inverse_minus_eye(Ls, C):
    row = lax.broadcasted_iota(jnp.int32, (C, C), 0)
    col = lax.broadcasted_iota(jnp.int32, (C, C), 1)
    xs = None
    s = 1
    while s < C:
        same_pair = (row // (2 * s)) == (col // (2 * s))
        low_left = same_pair & ((row // s) % 2 == 1) & ((col // s) % 2 == 0)
        offs = [jnp.where(low_left, L, 0.0) for L in Ls]
        if xs is None:
            xs = [-m for m in offs]
        else:
            xb = [x.astype(BF16) for x in xs]
            ys = [m + _dot(m.astype(BF16), x) for m, x in zip(offs, xb)]
            xs = [x - y - _dot(x16, y.astype(BF16)) for x, x16, y in zip(xs, xb, ys)]
        s *= 2
    return xs


def _gdn_kernel(qkv_ref, z_ref, ba_ref, convw_ref, avec_ref, dtvec_ref, nw_ref,
                o_ref, xbuf, state, y_s, g_s, gt_s, beta_s, *, TB, C):
    t = pl.program_id(1)
    W = 2 * LA_QK + LA_V

    @pl.when(t == 0)
    def _():
        xbuf[0:SUBLANES, :] = jnp.zeros((SUBLANES, W), F32)
        state[...] = jnp.zeros(state.shape, F32)

    xbuf[SUBLANES:SUBLANES + TB, :] = qkv_ref[...]
    conv = convw_ref[CONV_K - 1:CONV_K, :] * xbuf[SUBLANES:SUBLANES + TB, :]
    for i in range(CONV_K - 1):
        off = SUBLANES - (CONV_K - 1) + i
        conv = conv + convw_ref[i:i + 1, :] * xbuf[off:off + TB, :]
    xbuf[0:SUBLANES, :] = xbuf[TB:TB + SUBLANES, :]
    y = _silu(conv)

    for h in range(LA_HEADS):
        qs = slice(h * LA_DK, (h + 1) * LA_DK)
        ks = slice(LA_QK + h * LA_DK, LA_QK + (h + 1) * LA_DK)
        q = y[:, qs]
        k = y[:, ks]
        y_s[:, qs] = q * lax.rsqrt(jnp.sum(q * q, -1, keepdims=True) + L2_EPS) * (LA_DK ** -0.5)
        y_s[:, ks] = k * lax.rsqrt(jnp.sum(k * k, -1, keepdims=True) + L2_EPS)
    y_s[:, 2 * LA_QK:] = y[:, 2 * LA_QK:]

    ba = ba_ref[...]
    beta_s[...] = jax.nn.sigmoid(ba)
    ba_lane = lax.broadcasted_iota(jnp.int32, (1, BA_PAD), 1)
    is_g = (ba_lane >= LA_HEADS) & (ba_lane < 2 * LA_HEADS)
    g_raw = -jnp.where(is_g, jnp.exp(avec_ref[...]), 0.0) * _softplus(ba + dtvec_ref[...])

    row = lax.broadcasted_iota(jnp.int32, (C, C), 0)
    col = lax.broadcasted_iota(jnp.int32, (C, C), 1)
    causal = row >= col
    strict = row > col
    tril = causal.astype(F32)
    nC = TB // C
    for c in range(nC):
        g_s[c * C:(c + 1) * C, :] = _dot(tril, g_raw[c * C:(c + 1) * C, :], HIGHEST)
    for j in range(TB // LANES):
        gt_s[:, j * LANES:(j + 1) * LANES] = g_s[j * LANES:(j + 1) * LANES, :].T

    probs = [(c, h) for c in range(nC) for h in range(LA_HEADS)]
    Ls, intras, rhss, qgs, kds, egls = [], [], [], [], [], []
    for c, h in probs:
        rs = slice(c * C, (c + 1) * C)
        qn = y_s[rs, h * LA_DK:(h + 1) * LA_DK]
        kn = y_s[rs, LA_QK + h * LA_DK:LA_QK + (h + 1) * LA_DK]
        v = y_s[rs, 2 * LA_QK + h * LA_DV:2 * LA_QK + (h + 1) * LA_DV]
        gb = jnp.broadcast_to(g_s[rs, LA_HEADS + h:LA_HEADS + h + 1], (C, LANES))
        beta = jnp.broadcast_to(beta_s[rs, h:h + 1], (C, LANES))
        g_row = gt_s[LA_HEADS + h:LA_HEADS + h + 1, rs]
        gdiff = gb[:, :C] - g_row
        decay = jnp.where(causal, jnp.exp(jnp.where(causal, gdiff, 0.0)), 0.0)
        kb = kn * beta
        kq = _dot_nt(jnp.concatenate([kb, qn], axis=0).astype(BF16), kn.astype(BF16))
        Ls.append(jnp.where(strict, kq[:C] * decay, 0.0))
        intras.append((kq[C:] * decay).astype(BF16))
        eg = jnp.exp(gb)
        rhss.append(jnp.concatenate([v * beta, kb * eg], axis=1))
        qgs.append((qn * eg).astype(BF16))
        g_last = gb[C - 1:C, :]
        kds.append((kn * jnp.exp(g_last - gb)).astype(BF16))
        egls.append(jnp.exp(g_last))

    xs = _unit_lower_inverse_minus_eye(Ls, C)
    sols = [r + _dot(x.astype(BF16), r.astype(BF16)) for x, r in zip(xs, rhss)]

    S = [state[h] for h in range(LA_HEADS)]
    for c in range(nC):
        ps = [c * LA_HEADS + h for h in range(LA_HEADS)]
        r2 = [_dot(jnp.concatenate([sols[p][:, LA_DV:].astype(BF16), qgs[p]], axis=0),
                   S[h].astype(BF16)) for h, p in enumerate(ps)]
        vn = [(sols[p][:, :LA_DV] - r2[h][:C]).astype(BF16) for h, p in enumerate(ps)]
        S = [S[h] * egls[p] + _dot_tn(kds[p], vn[h]) for h, p in enumerate(ps)]
        rs = slice(c * C, (c + 1) * C)
        for h, p in enumerate(ps):
            o = r2[h][C:] + _dot(intras[p], vn[h])
            zg = _silu(z_ref[rs, h * LA_DV:(h + 1) * LA_DV])
            o_ref[rs, h * LA_DV:(h + 1) * LA_DV] = (_rms(o, nw_ref[...], NORM_EPS) * zg).astype(BF16)
    for h in range(LA_HEADS):
        state[h] = S[h]


def _gdn(la, convw, avec, dtvec, nw, B, T, TB, C):
    M = B * T
    nT = T // TB
    W = 2 * LA_QK + LA_V
    kern = functools.partial(_gdn_kernel, TB=TB, C=C)
    return pl.pallas_call(
        kern,
        grid=(B, nT),
        in_specs=[
            pl.BlockSpec((TB, W), lambda b, t: (b * nT + t, 0)),
            pl.BlockSpec((TB, LA_V), lambda b, t: (b * nT + t, W // LA_V)),
            pl.BlockSpec((TB, BA_PAD), lambda b, t: (b * nT + t, (W + LA_V) // BA_PAD)),
            pl.BlockSpec((CONV_K, W), lambda b, t: (0, 0)),
            pl.BlockSpec((1, BA_PAD), lambda b, t: (0, 0)),
            pl.BlockSpec((1, BA_PAD), lambda b, t: (0, 0)),
            pl.BlockSpec((1, LA_DV), lambda b, t: (0, 0)),
        ],
        out_specs=pl.BlockSpec((TB, LA_V), lambda b, t: (b * nT + t, 0)),
        out_shape=jax.ShapeDtypeStruct((M, LA_V), BF16),
        scratch_shapes=[
            pltpu.VMEM((TB + SUBLANES, W), F32),
            pltpu.VMEM((LA_HEADS, LA_DK, LA_DV), F32),
            pltpu.VMEM((TB, W), F32),
            pltpu.VMEM((TB, BA_PAD), F32),
            pltpu.VMEM((LANES, TB), F32),
            pltpu.VMEM((TB, BA_PAD), F32),
        ],
        compiler_params=pltpu.CompilerParams(
            dimension_semantics=("parallel", "arbitrary"), vmem_limit_bytes=VMEM_LIMIT),
        name="gdn",
    )(la, la, la, convw, avec, dtvec, nw.reshape(1, LA_DV))


def _transpose_blocks(x):
    n = x.shape[0] // LANES
    return jnp.concatenate([x[j * LANES:(j + 1) * LANES, :].T for j in range(n)], axis=1)


def _attn_kernel(lam_ref, tab_ref, q_ref, k_ref, v_ref, nw_ref, o_ref, bias_s, vt_s, *,
                 tq, lam_init):
    i = pl.program_id(2)
    n_blocks = vt_s.shape[0]

    @pl.when(i == 0)
    def _():
        for e in range(2):
            x = jnp.broadcast_to(tab_ref[0, e], (tq, 2 * tq))
            bias_s[e] = pltpu.roll(x, 0, 1, stride=1, stride_axis=0)[:, :tq]
        for j in range(n_blocks):
            vt_s[j] = _transpose_blocks(v_ref[0, j * tq:(j + 1) * tq, :].astype(F32)).astype(BF16)

    qt = _transpose_blocks(q_ref[0].astype(F32))
    sub = lax.broadcasted_iota(jnp.int32, qt.shape, 0)
    q2t = jnp.concatenate([jnp.where(sub < DIFF_DH, qt, 0.0),
                           jnp.where(sub >= DIFF_DH, qt, 0.0)], axis=1).astype(BF16)

    def scores(j):
        kj = k_ref[0, pl.ds(pl.multiple_of(j * tq, tq), tq), :]
        return _dot(kj, q2t)

    def biased(s, bias):
        return jnp.concatenate([s[:, :tq] + bias, s[:, tq:] + bias], axis=1)

    def update(carry, ss, js):
        m, l, acc = carry
        m_new = m
        for s in ss:
            m_new = jnp.maximum(m_new, jnp.max(s, axis=0, keepdims=True))
        alpha = jnp.exp2(m - m_new)
        l = alpha * l
        acc = alpha * acc
        for s, j in zip(ss, js):
            p = jnp.exp2(s - m_new)
            l = l + jnp.sum(p, axis=0, keepdims=True)
            acc = acc + _dot(vt_s[j], p.astype(BF16))
        return m_new, l, acc

    def group_update(carry, blocks):
        m, l, acc = carry
        m_new = m
        for j, thunk, off in blocks:
            s = thunk()
            ref = m if off is None else m + off
            p = jnp.exp2(s - ref)
            l = l + jnp.sum(p, axis=0, keepdims=True)
            acc = acc + _dot(vt_s[j], p.astype(BF16))
            smax = jnp.max(s, axis=0, keepdims=True)
            m_new = jnp.maximum(m_new, smax if off is None else smax - off)
        alpha = jnp.exp2(m - m_new)
        fast = (m_new, alpha * l, alpha * acc)

        def redo():
            c = carry
            for j, thunk, off in blocks:
                s = thunk()
                c = update(c, [s if off is None else s - off], [j])
            return c

        return lax.cond(jnp.max(m_new - m) <= MAX_LEAD, lambda: fast, redo)

    j_near = jnp.maximum(i - 1, 0)
    edge = LANES
    assert edge >= MAX_DISTANCE and tq >= 2 * edge

    def near_scores():
        s = scores(j_near)
        corner = bias_s[1, tq - edge:tq, 0:edge]
        bot = s[tq - edge:]
        bot = jnp.concatenate([bot[:, :edge] + corner, bot[:, edge:tq],
                               bot[:, tq:tq + edge] + corner, bot[:, tq + edge:]], axis=1)
        return jnp.concatenate([s[:tq - edge], bot], axis=0)

    s_diag = biased(scores(i), bias_s[0])
    m0 = jnp.max(s_diag[:DIFF_DH], axis=0, keepdims=True)
    carry = (m0, jnp.zeros((1, 2 * tq), F32), jnp.zeros((DIFF_DV, 2 * tq), F32))
    no_near = jnp.where(i == 0, -NEG_BIG, 0.0)
    carry = group_update(carry, [(i, lambda: s_diag, None), (j_near, near_scores, no_near)])
    n_far = jnp.maximum(i - 1, 0)
    done = 0
    for width in FAR_UNROLLS:
        def body(g, c, width=width, done=done):
            js = [done + width * g + u for u in range(width)]
            return group_update(c, [(j, functools.partial(scores, j), None) for j in js])
        n_groups = (n_far - done) // width
        carry = lax.fori_loop(0, n_groups, body, carry)
        done = done + n_groups * width
    m, l, acc = carry

    lam = (jnp.exp(jnp.sum(lam_ref[0:1, :] * lam_ref[1:2, :], axis=-1, keepdims=True))
           - jnp.exp(jnp.sum(lam_ref[2:3, :] * lam_ref[3:4, :], axis=-1, keepdims=True)) + lam_init)
    ot = acc[:, :tq] / l[:, :tq] - lam * (acc[:, tq:] / l[:, tq:])
    o = jnp.concatenate([ot[:, j * LANES:(j + 1) * LANES].T for j in range(tq // LANES)], axis=0)
    o_ref[0] = (_rms(o, nw_ref[...], DIFF_NORM_EPS) * (1.0 - lam_init)).astype(BF16)


def _attn(dd, lams, bias_tab, nw, B, T, tq, lam_init):
    assert T % tq == 0
    nq = T // tq
    dd3 = dd.reshape(B, T, dd.shape[1])
    kern = functools.partial(_attn_kernel, tq=tq, lam_init=lam_init)
    out = pl.pallas_call(
        kern,
        grid=(B, DIFF_HEADS, nq),
        in_specs=[
            pl.BlockSpec((4, DIFF_DH), lambda b, h, i: (0, 0)),
            pl.BlockSpec((1, 2, 1, 2 * tq), lambda b, h, i: (h, 0, 0, 0)),
            pl.BlockSpec((1, tq, LANES), lambda b, h, i: (b, i, h)),
            pl.BlockSpec((1, T, LANES), lambda b, h, i: (b, 0, DIFF_HEADS + h)),
            pl.BlockSpec((1, T, LANES), lambda b, h, i: (b, 0, 2 * DIFF_HEADS + h)),
            pl.BlockSpec((1, DIFF_DV), lambda b, h, i: (0, 0)),
        ],
        out_specs=pl.BlockSpec((1, tq, DIFF_DV), lambda b, h, i: (b, i, h)),
        out_shape=jax.ShapeDtypeStruct((B, T, DIFF_V), BF16),
        scratch_shapes=[pltpu.VMEM((2, tq, tq), F32),
                        pltpu.VMEM((nq, DIFF_DV, tq), BF16)],
        compiler_params=pltpu.CompilerParams(
            dimension_semantics=("parallel", "parallel", "arbitrary"),
            vmem_limit_bytes=VMEM_LIMIT),
        name="diff_attn",
    )(lams, bias_tab, dd3, dd3, dd3, nw.reshape(1, DIFF_DV))
    return out.reshape(B * T, DIFF_V)


def _t5_bucket(rel):
    n = jnp.maximum(rel, 0)
    max_exact = NUM_BUCKETS // 2
    nf = jnp.maximum(n, 1).astype(F32)
    large = max_exact + (jnp.log(nf / max_exact) / math.log(MAX_DISTANCE / max_exact)
                         * (NUM_BUCKETS - max_exact)).astype(jnp.int32)
    large = jnp.minimum(large, NUM_BUCKETS - 1)
    return jnp.where(n < max_exact, n, large)


def _bias_tables(rel_bias, tq):
    assert tq + 1 >= MAX_DISTANCE
    m = jnp.arange(2 * tq)
    far = rel_bias[NUM_BUCKETS - 1].astype(F32)
    tabs = []
    for d in (0, tq):
        rel = jnp.where(m < tq, d + m, d + m - 2 * tq)
        b = (rel_bias[_t5_bucket(rel)].astype(F32) - far) * math.log2(math.e)
        tabs.append(jnp.where((rel >= 0)[:, None], b, NEG_BIG).T)
    return jnp.stack(tabs, axis=1)[:, :, None, :]


def _ffn_kernel(x_ref, ola_ref, od_ref, wout_ref, nw_ref, wg_ref, wu_ref, wd_ref, fnw_ref,
                out_ref, h_s, acc_s, *, final_norm):
    f = pl.program_id(1)

    @pl.when(f == 0)
    def _():
        x1 = (x_ref[...] + _dot(ola_ref[...], wout_ref[0:LA_V, :])
              + _dot(od_ref[...], wout_ref[LA_V:LA_V + DIFF_V, :]))
        acc_s[...] = x1
        h_s[...] = _rms(x1, nw_ref[...], NORM_EPS).astype(BF16)

    h = h_s[...]
    a = (_silu(_dot(h, wg_ref[...])) * _dot(h, wu_ref[...])).astype(BF16)
    acc_s[...] += _dot(a, wd_ref[...])

    @pl.when(f == pl.num_programs(1) - 1)
    def _():
        y = acc_s[...]
        if final_norm:
            y = _rms(y, fnw_ref[...], NORM_EPS)
        out_ref[...] = y


def _ffn(xf, o_la, o_d, wout, nw, wgu, wdn, fnw, tm, tf, final_norm):
    M, D = xf.shape
    F = wdn.shape[0]
    nf = F // tf
    kern = functools.partial(_ffn_kernel, final_norm=final_norm)
    return pl.pallas_call(
        kern,
        grid=(M // tm, nf),
        in_specs=[
            pl.BlockSpec((tm, D), lambda i, f: (i, 0)),
            pl.BlockSpec((tm, LA_V), lambda i, f: (i, 0)),
            pl.BlockSpec((tm, DIFF_V), lambda i, f: (i, 0)),
            pl.BlockSpec(wout.shape, lambda i, f: (0, 0)),
            pl.BlockSpec((1, D), lambda i, f: (0, 0)),
            pl.BlockSpec((D, tf), lambda i, f: (0, f)),
            pl.BlockSpec((D, tf), lambda i, f: (0, nf + f)),
            pl.BlockSpec((tf, D), lambda i, f: (f, 0)),
            pl.BlockSpec((1, D), lambda i, f: (0, 0)),
        ],
        out_specs=pl.BlockSpec((tm, D), lambda i, f: (i, 0)),
        out_shape=jax.ShapeDtypeStruct((M, D), F32),
        scratch_shapes=[pltpu.VMEM((tm, D), BF16), pltpu.VMEM((tm, D), F32)],
        compiler_params=pltpu.CompilerParams(
            dimension_semantics=("parallel", "arbitrary"), vmem_limit_bytes=VMEM_LIMIT),
        name="out_proj_ffn",
    )(xf, o_la, o_d, wout, nw.reshape(1, D), wgu, wgu, wdn, fnw.reshape(1, D))


def _pick(n, pref):
    return pref if n % pref == 0 else n


def kernel(x, attn_norm_w, w_in, conv_w, a_log, dt_bias, la_norm_w, lambda_q1, lambda_k1,
           lambda_q2, lambda_k2, diff_norm_w, rel_bias, w_out, ffn_norm_w, w_gate_up,
           w_down, final_norm_w):
    B, T, D = x.shape
    depth = w_in.shape[0]
    M = B * T
    d_ff = w_down.shape[1]
    tm_proj = _pick(M, 512)
    tm_ffn = _pick(M, 1024)
    tf = _pick(d_ff, 256)
    tb = _pick(T, 256)
    chunk = 64
    tq = _pick(T, 512)

    n_la = 2 * LA_QK + 2 * LA_V
    bias_tab = _bias_tables(rel_bias, tq)
    lane_pad = BA_PAD - 2 * LA_HEADS
    dscale = jnp.concatenate([jnp.full((DIFF_QK,), DIFF_DH ** -0.5 * math.log2(math.e), F32),
                              jnp.ones((DIFF_QK + DIFF_V,), F32)]).reshape(1, -1)

    xf = x.reshape(M, D)
    for l in range(depth):
        w = w_in[l]
        wla = jnp.concatenate(
            [w[:, :n_la + 2 * LA_HEADS], jnp.zeros((D, lane_pad), w.dtype)], axis=1).astype(BF16)
        wd = w[:, n_la + 2 * LA_HEADS:].astype(BF16)
        la, dd = _in_proj(xf, attn_norm_w[l], wla, wd, dscale, tm_proj)

        avec = jnp.concatenate([jnp.zeros((LA_HEADS,), F32), a_log[l].astype(F32),
                                jnp.zeros((lane_pad,), F32)]).reshape(1, BA_PAD)
        dtvec = jnp.concatenate([jnp.zeros((LA_HEADS,), F32), dt_bias[l].astype(F32),
                                 jnp.zeros((lane_pad,), F32)]).reshape(1, BA_PAD)
        o_la = _gdn(la, conv_w[l], avec, dtvec, la_norm_w[l], B, T, tb, chunk)

        lam_init = 0.8 - 0.6 * math.exp(-0.3 * l)
        lams = jnp.stack([lambda_q1[l], lambda_k1[l], lambda_q2[l], lambda_k2[l]]).astype(F32)
        o_d = _attn(dd, lams, bias_tab, diff_norm_w[l], B, T, tq, lam_init)

        xf = _ffn(xf, o_la, o_d, w_out[l].astype(BF16), ffn_norm_w[l],
                  w_gate_up[l].astype(BF16), w_down[l].astype(BF16), final_norm_w,
                  tm_ffn, tf, final_norm=(l == depth - 1))
    return xf.reshape(B, T, D)
```

```python
import functools
import math

import jax
import jax.numpy as jnp
from jax import lax
from jax.experimental import pallas as pl
from jax.experimental.pallas import tpu as pltpu

LA_HEADS = 4
LA_DK = 128
LA_DV = 128
LA_QK = LA_HEADS * LA_DK
LA_V = LA_HEADS * LA_DV
CONV_K = 4
DIFF_HEADS = 4
DIFF_DH = 64
DIFF_DV = 2 * DIFF_DH
DIFF_QK = DIFF_HEADS * 2 * DIFF_DH
DIFF_V = DIFF_HEADS * DIFF_DV
NUM_BUCKETS = 32
MAX_DISTANCE = 128
NORM_EPS = 1e-6
DIFF_NORM_EPS = 1e-5
L2_EPS = 1e-6

LANES = 128
SUBLANES = 8
BA_PAD = LANES
LA_WIDTH = 2 * LA_QK + 2 * LA_V + BA_PAD
NEG_BIG = -1e30
FAR_UNROLLS = (4, 2, 1)
MAX_LEAD = 40.0

VMEM_LIMIT = 52 * 1024 * 1024

F32 = jnp.float32
BF16 = jnp.bfloat16
HIGHEST = lax.Precision.HIGHEST


def _dot(a, b, precision=None):
    return jnp.dot(a, b, preferred_element_type=F32, precision=precision)


def _dot_nt(a, b, precision=None):
    return lax.dot_general(a, b, (((1,), (1,)), ((), ())),
                           preferred_element_type=F32, precision=precision)


def _dot_tn(a, b, precision=None):
    return lax.dot_general(a, b, (((0,), (0,)), ((), ())),
                           preferred_element_type=F32, precision=precision)


def _rms(x, w, eps):
    return x * lax.rsqrt(jnp.mean(x * x, axis=-1, keepdims=True) + eps) * w


def _silu(x):
    h = 0.5 * x
    return h + h * jnp.tanh(h)


def _softplus(x):
    return jnp.maximum(x, 0.0) + jnp.log1p(jnp.exp(-jnp.abs(x)))


def _in_proj_kernel(x_ref, nw_ref, wla_ref, wd_ref, dscale_ref, la_ref, d_ref):
    h = _rms(x_ref[...], nw_ref[...], NORM_EPS).astype(BF16)
    la_ref[...] = _dot(h, wla_ref[...])
    d_ref[...] = (_dot(h, wd_ref[...]) * dscale_ref[...]).astype(BF16)


def _in_proj(xf, nw, wla, wd, dscale, tm):
    M, D = xf.shape
    return pl.pallas_call(
        _in_proj_kernel,
        grid=(M // tm,),
        in_specs=[
            pl.BlockSpec((tm, D), lambda i: (i, 0)),
            pl.BlockSpec((1, D), lambda i: (0, 0)),
            pl.BlockSpec(wla.shape, lambda i: (0, 0)),
            pl.BlockSpec(wd.shape, lambda i: (0, 0)),
            pl.BlockSpec(dscale.shape, lambda i: (0, 0)),
        ],
        out_specs=[
            pl.BlockSpec((tm, wla.shape[1]), lambda i: (i, 0)),
            pl.BlockSpec((tm, wd.shape[1]), lambda i: (i, 0)),
        ],
        out_shape=[
            jax.ShapeDtypeStruct((M, wla.shape[1]), F32),
            jax.ShapeDtypeStruct((M, wd.shape[1]), BF16),
        ],
        compiler_params=pltpu.CompilerParams(
            dimension_semantics=("parallel",), vmem_limit_bytes=VMEM_LIMIT),
        name="in_proj",
    )(xf, nw.reshape(1, D), wla, wd, dscale)


def _unit_lower_inverse_minus_eye(Ls, C):
    row = lax.broadcasted_iota(jnp.int32, (C, C), 0)
    col = lax.broadcasted_iota(jnp.int32, (C, C), 1)
    xs = None
    s = 1
    while s < C:
        same_pair = (row // (2 * s)) == (col // (2 * s))
        low_left = same_pair & ((row // s) % 2 == 1) & ((col // s) % 2 == 0)
        offs = [jnp.where(low_left, L, 0.0) for L in Ls]
        if xs is None:
            xs = [-m for m in offs]
        else:
            xb = [x.astype(BF16) for x in xs]
            ys = [m + _dot(m.astype(BF16), x) for m, x in zip(offs, xb)]
            xs = [x - y - _dot(x16, y.astype(BF16)) for x, x16, y in zip(xs, xb, ys)]
        s *= 2
    return xs


def _gdn_kernel(qkv_ref, z_ref, ba_ref, convw_ref, avec_ref, dtvec_ref, nw_ref,
                o_ref, xbuf, state, y_s, g_s, gt_s, beta_s, *, TB, C):
    t = pl.program_id(1)
    W = 2 * LA_QK + LA_V

    @pl.when(t == 0)
    def _():
        xbuf[0:SUBLANES, :] = jnp.zeros((SUBLANES, W), F32)
        state[...] = jnp.zeros(state.shape, F32)

    xbuf[SUBLANES:SUBLANES + TB, :] = qkv_ref[...]
    conv = convw_ref[CONV_K - 1:CONV_K, :] * xbuf[SUBLANES:SUBLANES + TB, :]
    for i in range(CONV_K - 1):
        off = SUBLANES - (CONV_K - 1) + i
        conv = conv + convw_ref[i:i + 1, :] * xbuf[off:off + TB, :]
    xbuf[0:SUBLANES, :] = xbuf[TB:TB + SUBLANES, :]
    y = _silu(conv)

    for h in range(LA_HEADS):
        qs = slice(h * LA_DK, (h + 1) * LA_DK)
        ks = slice(LA_QK + h * LA_DK, LA_QK + (h + 1) * LA_DK)
        q = y[:, qs]
        k = y[:, ks]
        y_s[:, qs] = q * lax.rsqrt(jnp.sum(q * q, -1, keepdims=True) + L2_EPS) * (LA_DK ** -0.5)
        y_s[:, ks] = k * lax.rsqrt(jnp.sum(k * k, -1, keepdims=True) + L2_EPS)
    y_s[:, 2 * LA_QK:] = y[:, 2 * LA_QK:]

    ba = ba_ref[...]
    beta_s[...] = jax.nn.sigmoid(ba)
    ba_lane = lax.broadcasted_iota(jnp.int32, (1, BA_PAD), 1)
    is_g = (ba_lane >= LA_HEADS) & (ba_lane < 2 * LA_HEADS)
    g_raw = -jnp.where(is_g, jnp.exp(avec_ref[...]), 0.0) * _softplus(ba + dtvec_ref[...])

    row = lax.broadcasted_iota(jnp.int32, (C, C), 0)
    col = lax.broadcasted_iota(jnp.int32, (C, C), 1)
    causal = row >= col
    strict = row > col
    tril = causal.astype(F32)
    nC = TB // C
    for c in range(nC):
        g_s[c * C:(c + 1) * C, :] = _dot(tril, g_raw[c * C:(c + 1) * C, :], HIGHEST)
    for j in range(TB // LANES):
        gt_s[:, j * LANES:(j + 1) * LANES] = g_s[j * LANES:(j + 1) * LANES, :].T

    probs = [(c, h) for c in range(nC) for h in range(LA_HEADS)]
    Ls, intras, rhss, qgs, kds, egls = [], [], [], [], [], []
    for c, h in probs:
        rs = slice(c * C, (c + 1) * C)
        qn = y_s[rs, h * LA_DK:(h + 1) * LA_DK]
        kn = y_s[rs, LA_QK + h * LA_DK:LA_QK + (h + 1) * LA_DK]
        v = y_s[rs, 2 * LA_QK + h * LA_DV:2 * LA_QK + (h + 1) * LA_DV]
        gb = jnp.broadcast_to(g_s[rs, LA_HEADS + h:LA_HEADS + h + 1], (C, LANES))
        beta = jnp.broadcast_to(beta_s[rs, h:h + 1], (C, LANES))
        g_row = gt_s[LA_HEADS + h:LA_HEADS + h + 1, rs]
        gdiff = gb[:, :C] - g_row
        decay = jnp.where(causal, jnp.exp(jnp.where(causal, gdiff, 0.0)), 0.0)
        kb = kn * beta
        kq = _dot_nt(jnp.concatenate([kb, qn], axis=0).astype(BF16), kn.astype(BF16))
        Ls.append(jnp.where(strict, kq[:C] * decay, 0.0))
        intras.append((kq[C:] * decay).astype(BF16))
        eg = jnp.exp(gb)
        rhss.append(jnp.concatenate([v * beta, kb * eg], axis=1))
        qgs.append((qn * eg).astype(BF16))
        g_last = gb[C - 1:C, :]
        kds.append((kn * jnp.exp(g_last - gb)).astype(BF16))
        egls.append(jnp.exp(g_last))

    xs = _unit_lower_inverse_minus_eye(Ls, C)
    sols = [r + _dot(x.astype(BF16), r.astype(BF16)) for x, r in zip(xs, rhss)]

    S = [state[h] for h in range(LA_HEADS)]
    for c in range(nC):
        ps = [c * LA_HEADS + h for h in range(LA_HEADS)]
        r2 = [_dot(jnp.concatenate([sols[p][:, LA_DV:].astype(BF16), qgs[p]], axis=0),
                   S[h].astype(BF16)) for h, p in enumerate(ps)]
        vn = [(sols[p][:, :LA_DV] - r2[h][:C]).astype(BF16) for h, p in enumerate(ps)]
        S = [S[h] * egls[p] + _dot_tn(kds[p], vn[h]) for h, p in enumerate(ps)]
        rs = slice(c * C, (c + 1) * C)
        for h, p in enumerate(ps):
            o = r2[h][C:] + _dot(intras[p], vn[h])
            zg = _silu(z_ref[rs, h * LA_DV:(h + 1) * LA_DV])
            o_ref[rs, h * LA_DV:(h + 1) * LA_DV] = (_rms(o, nw_ref[...], NORM_EPS) * zg).astype(BF16)
    for h in range(LA_HEADS):
        state[h] = S[h]


def _gdn(la, convw, avec, dtvec, nw, B, T, TB, C):
    M = B * T
    nT = T // TB
    W = 2 * LA_QK + LA_V
    kern = functools.partial(_gdn_kernel, TB=TB, C=C)
    return pl.pallas_call(
        kern,
        grid=(B, nT),
        in_specs=[
            pl.BlockSpec((TB, W), lambda b, t: (b * nT + t, 0)),
            pl.BlockSpec((TB, LA_V), lambda b, t: (b * nT + t, W // LA_V)),
            pl.BlockSpec((TB, BA_PAD), lambda b, t: (b * nT + t, (W + LA_V) // BA_PAD)),
            pl.BlockSpec((CONV_K, W), lambda b, t: (0, 0)),
            pl.BlockSpec((1, BA_PAD), lambda b, t: (0, 0)),
            pl.BlockSpec((1, BA_PAD), lambda b, t: (0, 0)),
            pl.BlockSpec((1, LA_DV), lambda b, t: (0, 0)),
        ],
        out_specs=pl.BlockSpec((TB, LA_V), lambda b, t: (b * nT + t, 0)),
        out_shape=jax.ShapeDtypeStruct((M, LA_V), BF16),
        scratch_shapes=[
            pltpu.VMEM((TB + SUBLANES, W), F32),
            pltpu.VMEM((LA_HEADS, LA_DK, LA_DV), F32),
            pltpu.VMEM((TB, W), F32),
            pltpu.VMEM((TB, BA_PAD), F32),
            pltpu.VMEM((LANES, TB), F32),
            pltpu.VMEM((TB, BA_PAD), F32),
        ],
        compiler_params=pltpu.CompilerParams(
            dimension_semantics=("parallel", "arbitrary"), vmem_limit_bytes=VMEM_LIMIT),
        name="gdn",
    )(la, la, la, convw, avec, dtvec, nw.reshape(1, LA_DV))


def _transpose_blocks(x):
    n = x.shape[0] // LANES
    return jnp.concatenate([x[j * LANES:(j + 1) * LANES, :].T for j in range(n)], axis=1)


def _attn_kernel(lam_ref, tab_ref, q_ref, k_ref, v_ref, nw_ref, o_ref, bias_s, vt_s, *,
                 tq, lam_init):
    i = pl.program_id(2)
    n_blocks = vt_s.shape[0]

    @pl.when(i == 0)
    def _():
        for e in range(2):
            x = jnp.broadcast_to(tab_ref[0, e], (tq, 2 * tq))
            bias_s[e] = pltpu.roll(x, 0, 1, stride=1, stride_axis=0)[:, :tq]
        for j in range(n_blocks):
            vt_s[j] = _transpose_blocks(v_ref[0, j * tq:(j + 1) * tq, :].astype(F32)).astype(BF16)

    qt = _transpose_blocks(q_ref[0].astype(F32))
    sub = lax.broadcasted_iota(jnp.int32, qt.shape, 0)
    q2t = jnp.concatenate([jnp.where(sub < DIFF_DH, qt, 0.0),
                           jnp.where(sub >= DIFF_DH, qt, 0.0)], axis=1).astype(BF16)

    def scores(j):
        kj = k_ref[0, pl.ds(pl.multiple_of(j * tq, tq), tq), :]
        return _dot(kj, q2t)

    def biased(s, bias):
        return jnp.concatenate([s[:, :tq] + bias, s[:, tq:] + bias], axis=1)

    def update(carry, ss, js):
        m, l, acc = carry
        m_new = m
        for s in ss:
            m_new = jnp.maximum(m_new, jnp.max(s, axis=0, keepdims=True))
        alpha = jnp.exp2(m - m_new)
        l = alpha * l
        acc = alpha * acc
        for s, j in zip(ss, js):
            p = jnp.exp2(s - m_new)
            l = l + jnp.sum(p, axis=0, keepdims=True)
            acc = acc + _dot(vt_s[j], p.astype(BF16))
        return m_new, l, acc

    def group_update(carry, blocks):
        m, l, acc = carry
        m_new = m
        for j, thunk, off in blocks:
            s = thunk()
            ref = m if off is None else m + off
            p = jnp.exp2(s - ref)
            l = l + jnp.sum(p, axis=0, keepdims=True)
            acc = acc + _dot(vt_s[j], p.astype(BF16))
            smax = jnp.max(s, axis=0, keepdims=True)
            m_new = jnp.maximum(m_new, smax if off is None else smax - off)
        alpha = jnp.exp2(m - m_new)
        fast = (m_new, alpha * l, alpha * acc)

        def redo():
            c = carry
            for j, thunk, off in blocks:
                s = thunk()
                c = update(c, [s if off is None else s - off], [j])
            return c

        return lax.cond(jnp.max(m_new - m) <= MAX_LEAD, lambda: fast, redo)

    j_near = jnp.maximum(i - 1, 0)
    edge = LANES
    assert edge >= MAX_DISTANCE and tq >= 2 * edge

    def near_scores():
        s = scores(j_near)
        corner = bias_s[1, tq - edge:tq, 0:edge]
        bot = s[tq - edge:]
        bot = jnp.concatenate([bot[:, :edge] + corner, bot[:, edge:tq],
                               bot[:, tq:tq + edge] + corner, bot[:, tq + edge:]], axis=1)
        return jnp.concatenate([s[:tq - edge], bot], axis=0)

    s_diag = biased(scores(i), bias_s[0])
    m0 = jnp.max(s_diag[:DIFF_DH], axis=0, keepdims=True)
    carry = (m0, jnp.zeros((1, 2 * tq), F32), jnp.zeros((DIFF_DV, 2 * tq), F32))
    no_near = jnp.where(i == 0, -NEG_BIG, 0.0)
    carry = group_update(carry, [(i, lambda: s_diag, None), (j_near, near_scores, no_near)])
    n_far = jnp.maximum(i - 1, 0)
    done = 0
    for width in FAR_UNROLLS:
        def body(g, c, width=width, done=done):
            js = [done + width * g + u for u in range(width)]
            return group_update(c, [(j, functools.partial(scores, j), None) for j in js])
        n_groups = (n_far - done) // width
        carry = lax.fori_loop(0, n_groups, body, carry)
        done = done + n_groups * width
    m, l, acc = carry

    lam = (jnp.exp(jnp.sum(lam_ref[0:1, :] * lam_ref[1:2, :], axis=-1, keepdims=True))
           - jnp.exp(jnp.sum(lam_ref[2:3, :] * lam_ref[3:4, :], axis=-1, keepdims=True)) + lam_init)
    ot = acc[:, :tq] / l[:, :tq] - lam * (acc[:, tq:] / l[:, tq:])
    o = jnp.concatenate([ot[:, j * LANES:(j + 1) * LANES].T for j in range(tq // LANES)], axis=0)
    o_ref[0] = (_rms(o, nw_ref[...], DIFF_NORM_EPS) * (1.0 - lam_init)).astype(BF16)


def _attn(dd, lams, bias_tab, nw, B, T, tq, lam_init):
    assert T % tq == 0
    nq = T // tq
    dd3 = dd.reshape(B, T, dd.shape[1])
    kern = functools.partial(_attn_kernel, tq=tq, lam_init=lam_init)
    out = pl.pallas_call(
        kern,
        grid=(B, DIFF_HEADS, nq),
        in_specs=[
            pl.BlockSpec((4, DIFF_DH), lambda b, h, i: (0, 0)),
            pl.BlockSpec((1, 2, 1, 2 * tq), lambda b, h, i: (h, 0, 0, 0)),
            pl.BlockSpec((1, tq, LANES), lambda b, h, i: (b, i, h)),
            pl.BlockSpec((1, T, LANES), lambda b, h, i: (b, 0, DIFF_HEADS + h)),
            pl.BlockSpec((1, T, LANES), lambda b, h, i: (b, 0, 2 * DIFF_HEADS + h)),
            pl.BlockSpec((1, DIFF_DV), lambda b, h, i: (0, 0)),
        ],
        out_specs=pl.BlockSpec((1, tq, DIFF_DV), lambda b, h, i: (b, i, h)),
        out_shape=jax.ShapeDtypeStruct((B, T, DIFF_V), BF16),
        scratch_shapes=[pltpu.VMEM((2, tq, tq), F32),
                        pltpu.VMEM((nq, DIFF_DV, tq), BF16)],
        compiler_params=pltpu.CompilerParams(
            dimension_semantics=("parallel", "parallel", "arbitrary"),
            vmem_limit_bytes=VMEM_LIMIT),
        name="diff_attn",
    )(lams, bias_tab, dd3, dd3, dd3, nw.reshape(1, DIFF_DV))
    return out.reshape(B * T, DIFF_V)


def _t5_bucket(rel):
    n = jnp.maximum(rel, 0)
    max_exact = NUM_BUCKETS // 2
    nf = jnp.maximum(n, 1).astype(F32)
    large = max_exact + (jnp.log(nf / max_exact) / math.log(MAX_DISTANCE / max_exact)
                         * (NUM_BUCKETS - max_exact)).astype(jnp.int32)
    large = jnp.minimum(large, NUM_BUCKETS - 1)
    return jnp.where(n < max_exact, n, large)


def _bias_tables(rel_bias, tq):
    assert tq + 1 >= MAX_DISTANCE
    m = jnp.arange(2 * tq)
    far = rel_bias[NUM_BUCKETS - 1].astype(F32)
    tabs = []
    for d in (0, tq):
        rel = jnp.where(m < tq, d + m, d + m - 2 * tq)
        b = (rel_bias[_t5_bucket(rel)].astype(F32) - far) * math.log2(math.e)
        tabs.append(jnp.where((rel >= 0)[:, None], b, NEG_BIG).T)
    return jnp.stack(tabs, axis=1)[:, :, None, :]


def _ffn_kernel(x_ref, ola_ref, od_ref, wout_ref, nw_ref, wgu_ref, wd_ref, fnw_ref,
                out_ref, *, final_norm):
    d_ff = wd_ref.shape[0]
    y = (x_ref[...] + _dot(ola_ref[...], wout_ref[0:LA_V, :])
         + _dot(od_ref[...], wout_ref[LA_V:LA_V + DIFF_V, :]))
    h = _rms(y, nw_ref[...], NORM_EPS).astype(BF16)
    gate = _dot(h, wgu_ref[:, 0:d_ff])
    up = _dot(h, wgu_ref[:, d_ff:2 * d_ff])
    y = y + _dot((_silu(gate) * up).astype(BF16), wd_ref[...])
    if final_norm:
        y = _rms(y, fnw_ref[...], NORM_EPS)
    out_ref[...] = y


def _ffn(xf, o_la, o_d, wout, nw, wgu, wdn, fnw, tm, final_norm):
    M, D = xf.shape
    kern = functools.partial(_ffn_kernel, final_norm=final_norm)
    resident = dict(pipeline_mode=pl.Buffered(1))
    return pl.pallas_call(
        kern,
        grid=(M // tm,),
        in_specs=[
            pl.BlockSpec((tm, D), lambda i: (i, 0)),
            pl.BlockSpec((tm, LA_V), lambda i: (i, 0)),
            pl.BlockSpec((tm, DIFF_V), lambda i: (i, 0)),
            pl.BlockSpec(wout.shape, lambda i: (0, 0), **resident),
            pl.BlockSpec((1, D), lambda i: (0, 0)),
            pl.BlockSpec(wgu.shape, lambda i: (0, 0), **resident),
            pl.BlockSpec(wdn.shape, lambda i: (0, 0), **resident),
            pl.BlockSpec((1, D), lambda i: (0, 0)),
        ],
        out_specs=pl.BlockSpec((tm, D), lambda i: (i, 0)),
        out_shape=jax.ShapeDtypeStruct((M, D), F32),
        compiler_params=pltpu.CompilerParams(
            dimension_semantics=("parallel",), vmem_limit_bytes=VMEM_LIMIT),
        name="out_proj_ffn",
    )(xf, o_la, o_d, wout, nw.reshape(1, D), wgu, wdn, fnw.reshape(1, D))


def _pick(n, pref):
    return pref if n % pref == 0 else n


def kernel(x, attn_norm_w, w_in, conv_w, a_log, dt_bias, la_norm_w, lambda_q1, lambda_k1,
           lambda_q2, lambda_k2, diff_norm_w, rel_bias, w_out, ffn_norm_w, w_gate_up,
           w_down, final_norm_w):
    B, T, D = x.shape
    depth = w_in.shape[0]
    M = B * T
    d_ff = w_down.shape[1]
    tm_proj = _pick(M, 512)
    tm_ffn = _pick(M, 512)
    tb = _pick(T, 256)
    chunk = 64
    tq = _pick(T, 512)

    n_la = 2 * LA_QK + 2 * LA_V
    bias_tab = _bias_tables(rel_bias, tq)
    lane_pad = BA_PAD - 2 * LA_HEADS
    dscale = jnp.concatenate([jnp.full((DIFF_QK,), DIFF_DH ** -0.5 * math.log2(math.e), F32),
                              jnp.ones((DIFF_QK + DIFF_V,), F32)]).reshape(1, -1)

    xf = x.reshape(M, D)
    for l in range(depth):
        w = w_in[l]
        wla = jnp.concatenate(
            [w[:, :n_la + 2 * LA_HEADS], jnp.zeros((D, lane_pad), w.dtype)], axis=1).astype(BF16)
        wd = w[:, n_la + 2 * LA_HEADS:].astype(BF16)
        la, dd = _in_proj(xf, attn_norm_w[l], wla, wd, dscale, tm_proj)

        avec = jnp.concatenate([jnp.zeros((LA_HEADS,), F32), a_log[l].astype(F32),
                                jnp.zeros((lane_pad,), F32)]).reshape(1, BA_PAD)
        dtvec = jnp.concatenate([jnp.zeros((LA_HEADS,), F32), dt_bias[l].astype(F32),
                                 jnp.zeros((lane_pad,), F32)]).reshape(1, BA_PAD)
        o_la = _gdn(la, conv_w[l], avec, dtvec, la_norm_w[l], B, T, tb, chunk)

        lam_init = 0.8 - 0.6 * math.exp(-0.3 * l)
        lams = jnp.stack([lambda_q1[l], lambda_k1[l], lambda_q2[l], lambda_k2[l]]).astype(F32)
        o_d = _attn(dd, lams, bias_tab, diff_norm_w[l], B, T, tq, lam_init)

        xf = _ffn(xf, o_la, o_d, w_out[l].astype(BF16), ffn_norm_w[l],
                  w_gate_up[l].astype(BF16), w_down[l].astype(BF16), final_norm_w,
                  tm_ffn, final_norm=(l == depth - 1))
    return xf.reshape(B, T, D)
```

```python
import functools
import math

import jax
import jax.numpy as jnp
from jax import lax
from jax.experimental import pallas as pl
from jax.experimental.pallas import tpu as pltpu

LA_HEADS = 4
LA_DK = 128
LA_DV = 128
LA_QK = LA_HEADS * LA_DK
LA_V = LA_HEADS * LA_DV
CONV_K = 4
DIFF_HEADS = 4
DIFF_DH = 64
DIFF_DV = 2 * DIFF_DH
DIFF_QK = DIFF_HEADS * 2 * DIFF_DH
DIFF_V = DIFF_HEADS * DIFF_DV
NUM_BUCKETS = 32
MAX_DISTANCE = 128
NORM_EPS = 1e-6
DIFF_NORM_EPS = 1e-5
L2_EPS = 1e-6

LANES = 128
SUBLANES = 8
BA_PAD = LANES
LA_WIDTH = 2 * LA_QK + 2 * LA_V + BA_PAD
NEG_BIG = -1e30
FAR_UNROLLS = (4, 2, 1)
MAX_LEAD = 40.0

VMEM_LIMIT = 52 * 1024 * 1024

F32 = jnp.float32
BF16 = jnp.bfloat16
HIGHEST = lax.Precision.HIGHEST


def _dot(a, b, precision=None):
    return jnp.dot(a, b, preferred_element_type=F32, precision=precision)


def _dot_nt(a, b, precision=None):
    return lax.dot_general(a, b, (((1,), (1,)), ((), ())),
                           preferred_element_type=F32, precision=precision)


def _dot_tn(a, b, precision=None):
    return lax.dot_general(a, b, (((0,), (0,)), ((), ())),
                           preferred_element_type=F32, precision=precision)


def _rms(x, w, eps):
    return x * lax.rsqrt(jnp.mean(x * x, axis=-1, keepdims=True) + eps) * w


def _silu(x):
    h = 0.5 * x
    return h + h * jnp.tanh(h)


def _softplus(x):
    return jnp.maximum(x, 0.0) + jnp.log1p(jnp.exp(-jnp.abs(x)))


def _in_proj_kernel(x_ref, nw_ref, wla_ref, wd_ref, dscale_ref, la_ref, d_ref):
    h = _rms(x_ref[...], nw_ref[...], NORM_EPS).astype(BF16)
    la_ref[...] = _dot(h, wla_ref[...])
    d_ref[...] = (_dot(h, wd_ref[...]) * dscale_ref[...]).astype(BF16)


def _in_proj(xf, nw, wla, wd, dscale, tm):
    M, D = xf.shape
    return pl.pallas_call(
        _in_proj_kernel,
        grid=(M // tm,),
        in_specs=[
            pl.BlockSpec((tm, D), lambda i: (i, 0)),
            pl.BlockSpec((1, D), lambda i: (0, 0)),
            pl.BlockSpec(wla.shape, lambda i: (0, 0)),
            pl.BlockSpec(wd.shape, lambda i: (0, 0)),
            pl.BlockSpec(dscale.shape, lambda i: (0, 0)),
        ],
        out_specs=[
            pl.BlockSpec((tm, wla.shape[1]), lambda i: (i, 0)),
            pl.BlockSpec((tm, wd.shape[1]), lambda i: (i, 0)),
        ],
        out_shape=[
            jax.ShapeDtypeStruct((M, wla.shape[1]), F32),
            jax.ShapeDtypeStruct((M, wd.shape[1]), BF16),
        ],
        compiler_params=pltpu.CompilerParams(
            dimension_semantics=("parallel",), vmem_limit_bytes=VMEM_LIMIT),
        name="in_proj",
    )(xf, nw.reshape(1, D), wla, wd, dscale)


def _unit_lower_inverse_minus_eye(Ls, C):
    row = lax.broadcasted_iota(jnp.int32, (C, C), 0)
    col = lax.broadcasted_iota(jnp.int32, (C, C), 1)
    xs = None
    s = 1
    while s < C:
        same_pair = (row // (2 * s)) == (col // (2 * s))
        low_left = same_pair & ((row // s) % 2 == 1) & ((col // s) % 2 == 0)
        offs = [jnp.where(low_left, L, 0.0) for L in Ls]
        if xs is None:
            xs = [-m for m in offs]
        else:
            xb = [x.astype(BF16) for x in xs]
            ys = [m + _dot(m.astype(BF16), x) for m, x in zip(offs, xb)]
            xs = [x - y - _dot(x16, y.astype(BF16)) for x, x16, y in zip(xs, xb, ys)]
        s *= 2
    return xs


def _gdn_kernel(qkv_ref, z_ref, ba_ref, convw_ref, avec_ref, dtvec_ref, nw_ref,
                o_ref, xbuf, state, y_s, g_s, gt_s, beta_s, *, TB, C):
    t = pl.program_id(0)
    W = 2 * LA_QK + LA_V
    B = qkv_ref.shape[0]
    nC = TB // C

    @pl.when(t == 0)
    def _():
        xbuf[:, 0:SUBLANES, :] = jnp.zeros((B, SUBLANES, W), F32)
        state[...] = jnp.zeros(state.shape, F32)

    row = lax.broadcasted_iota(jnp.int32, (C, C), 0)
    col = lax.broadcasted_iota(jnp.int32, (C, C), 1)
    causal = row >= col
    strict = row > col
    tril = causal.astype(F32)
    ba_lane = lax.broadcasted_iota(jnp.int32, (1, BA_PAD), 1)
    is_g = (ba_lane >= LA_HEADS) & (ba_lane < 2 * LA_HEADS)

    for b in range(B):
        xbuf[b, SUBLANES:SUBLANES + TB, :] = qkv_ref[b]
        conv = convw_ref[CONV_K - 1:CONV_K, :] * xbuf[b, SUBLANES:SUBLANES + TB, :]
        for i in range(CONV_K - 1):
            off = SUBLANES - (CONV_K - 1) + i
            conv = conv + convw_ref[i:i + 1, :] * xbuf[b, off:off + TB, :]
        xbuf[b, 0:SUBLANES, :] = xbuf[b, TB:TB + SUBLANES, :]
        y = _silu(conv)

        for h in range(LA_HEADS):
            qs = slice(h * LA_DK, (h + 1) * LA_DK)
            ks = slice(LA_QK + h * LA_DK, LA_QK + (h + 1) * LA_DK)
            q = y[:, qs]
            k = y[:, ks]
            y_s[b, :, qs] = (q * lax.rsqrt(jnp.sum(q * q, -1, keepdims=True) + L2_EPS)
                             * (LA_DK ** -0.5))
            y_s[b, :, ks] = k * lax.rsqrt(jnp.sum(k * k, -1, keepdims=True) + L2_EPS)
        y_s[b, :, 2 * LA_QK:] = y[:, 2 * LA_QK:]

        ba = ba_ref[b]
        beta_s[b] = jax.nn.sigmoid(ba)
        g_raw = -jnp.where(is_g, jnp.exp(avec_ref[...]), 0.0) * _softplus(ba + dtvec_ref[...])
        for c in range(nC):
            g_s[b, c * C:(c + 1) * C, :] = _dot(tril, g_raw[c * C:(c + 1) * C, :], HIGHEST)
        for j in range(TB // LANES):
            gt_s[b, :, j * LANES:(j + 1) * LANES] = g_s[b, j * LANES:(j + 1) * LANES, :].T

    probs = [(b, c, h) for b in range(B) for c in range(nC) for h in range(LA_HEADS)]
    index = {p: n for n, p in enumerate(probs)}
    Ls, intras, rhss, qgs, kds, egls = [], [], [], [], [], []
    for b, c, h in probs:
        rs = slice(c * C, (c + 1) * C)
        qn = y_s[b, rs, h * LA_DK:(h + 1) * LA_DK]
        kn = y_s[b, rs, LA_QK + h * LA_DK:LA_QK + (h + 1) * LA_DK]
        v = y_s[b, rs, 2 * LA_QK + h * LA_DV:2 * LA_QK + (h + 1) * LA_DV]
        gb = jnp.broadcast_to(g_s[b, rs, LA_HEADS + h:LA_HEADS + h + 1], (C, LANES))
        beta = jnp.broadcast_to(beta_s[b, rs, h:h + 1], (C, LANES))
        g_row = gt_s[b, LA_HEADS + h:LA_HEADS + h + 1, rs]
        gdiff = gb[:, :C] - g_row
        decay = jnp.where(causal, jnp.exp(jnp.where(causal, gdiff, 0.0)), 0.0)
        kb = kn * beta
        kq = _dot_nt(jnp.concatenate([kb, qn], axis=0).astype(BF16), kn.astype(BF16))
        Ls.append(jnp.where(strict, kq[:C] * decay, 0.0))
        intras.append((kq[C:] * decay).astype(BF16))
        eg = jnp.exp(gb)
        rhss.append(jnp.concatenate([v * beta, kb * eg], axis=1))
        qgs.append((qn * eg).astype(BF16))
        g_last = gb[C - 1:C, :]
        kds.append((kn * jnp.exp(g_last - gb)).astype(BF16))
        egls.append(jnp.exp(g_last))

    xs = _unit_lower_inverse_minus_eye(Ls, C)
    sols = [r + _dot(x.astype(BF16), r.astype(BF16)) for x, r in zip(xs, rhss)]

    streams = [(b, h) for b in range(B) for h in range(LA_HEADS)]
    S = [state[b * LA_HEADS + h] for b, h in streams]
    for c in range(nC):
        ps = [index[(b, c, h)] for b, h in streams]
        r2 = [_dot(jnp.concatenate([sols[p][:, LA_DV:].astype(BF16), qgs[p]], axis=0),
                   S[n].astype(BF16)) for n, p in enumerate(ps)]
        vn = [(sols[p][:, :LA_DV] - r2[n][:C]).astype(BF16) for n, p in enumerate(ps)]
        S = [S[n] * egls[p] + _dot_tn(kds[p], vn[n]) for n, p in enumerate(ps)]
        rs = slice(c * C, (c + 1) * C)
        for n, ((b, h), p) in enumerate(zip(streams, ps)):
            o = r2[n][C:] + _dot(intras[p], vn[n])
            zg = _silu(z_ref[b, rs, h * LA_DV:(h + 1) * LA_DV])
            o_ref[b, rs, h * LA_DV:(h + 1) * LA_DV] = (
                _rms(o, nw_ref[...], NORM_EPS) * zg).astype(BF16)
    for n in range(len(streams)):
        state[n] = S[n]


def _gdn(la, convw, avec, dtvec, nw, B, T, TB, C):
    nT = T // TB
    W = 2 * LA_QK + LA_V
    la3 = la.reshape(B, T, la.shape[1])
    kern = functools.partial(_gdn_kernel, TB=TB, C=C)
    out = pl.pallas_call(
        kern,
        grid=(nT,),
        in_specs=[
            pl.BlockSpec((B, TB, W), lambda t: (0, t, 0)),
            pl.BlockSpec((B, TB, LA_V), lambda t: (0, t, W // LA_V)),
            pl.BlockSpec((B, TB, BA_PAD), lambda t: (0, t, (W + LA_V) // BA_PAD)),
            pl.BlockSpec((CONV_K, W), lambda t: (0, 0)),
            pl.BlockSpec((1, BA_PAD), lambda t: (0, 0)),
            pl.BlockSpec((1, BA_PAD), lambda t: (0, 0)),
            pl.BlockSpec((1, LA_DV), lambda t: (0, 0)),
        ],
        out_specs=pl.BlockSpec((B, TB, LA_V), lambda t: (0, t, 0)),
        out_shape=jax.ShapeDtypeStruct((B, T, LA_V), BF16),
        scratch_shapes=[
            pltpu.VMEM((B, TB + SUBLANES, W), F32),
            pltpu.VMEM((B * LA_HEADS, LA_DK, LA_DV), F32),
            pltpu.VMEM((B, TB, W), F32),
            pltpu.VMEM((B, TB, BA_PAD), F32),
            pltpu.VMEM((B, LANES, TB), F32),
            pltpu.VMEM((B, TB, BA_PAD), F32),
        ],
        compiler_params=pltpu.CompilerParams(
            dimension_semantics=("arbitrary",), vmem_limit_bytes=VMEM_LIMIT),
        name="gdn",
    )(la3, la3, la3, convw, avec, dtvec, nw.reshape(1, LA_DV))
    return out.reshape(B * T, LA_V)


def _transpose_blocks(x):
    n = x.shape[0] // LANES
    return jnp.concatenate([x[j * LANES:(j + 1) * LANES, :].T for j in range(n)], axis=1)


def _attn_kernel(lam_ref, tab_ref, q_ref, k_ref, v_ref, nw_ref, o_ref, bias_s, vt_s, *,
                 tq, lam_init):
    i = pl.program_id(2)
    n_blocks = vt_s.shape[0]

    @pl.when(i == 0)
    def _():
        for e in range(2):
            x = jnp.broadcast_to(tab_ref[0, e], (tq, 2 * tq))
            bias_s[e] = pltpu.roll(x, 0, 1, stride=1, stride_axis=0)[:, :tq]
        for j in range(n_blocks):
            vt_s[j] = _transpose_blocks(v_ref[0, j * tq:(j + 1) * tq, :].astype(F32)).astype(BF16)

    qt = _transpose_blocks(q_ref[0].astype(F32))
    sub = lax.broadcasted_iota(jnp.int32, qt.shape, 0)
    q2t = jnp.concatenate([jnp.where(sub < DIFF_DH, qt, 0.0),
                           jnp.where(sub >= DIFF_DH, qt, 0.0)], axis=1).astype(BF16)

    def scores(j):
        kj = k_ref[0, pl.ds(pl.multiple_of(j * tq, tq), tq), :]
        return _dot(kj, q2t)

    def biased(s, bias):
        return jnp.concatenate([s[:, :tq] + bias, s[:, tq:] + bias], axis=1)

    def update(carry, ss, js):
        m, l, acc = carry
        m_new = m
        for s in ss:
            m_new = jnp.maximum(m_new, jnp.max(s, axis=0, keepdims=True))
        alpha = jnp.exp2(m - m_new)
        l = alpha * l
        acc = alpha * acc
        for s, j in zip(ss, js):
            p = jnp.exp2(s - m_new)
            l = l + jnp.sum(p, axis=0, keepdims=True)
            acc = acc + _dot(vt_s[j], p.astype(BF16))
        return m_new, l, acc

    def group_update(carry, blocks):
        m, l, acc = carry
        m_new = m
        for j, thunk, off in blocks:
            s = thunk()
            ref = m if off is None else m + off
            p = jnp.exp2(s - ref)
            l = l + jnp.sum(p, axis=0, keepdims=True)
            acc = acc + _dot(vt_s[j], p.astype(BF16))
            smax = jnp.max(s, axis=0, keepdims=True)
            m_new = jnp.maximum(m_new, smax if off is None else smax - off)
        alpha = jnp.exp2(m - m_new)
        fast = (m_new, alpha * l, alpha * acc)

        def redo():
            c = carry
            for j, thunk, off in blocks:
                s = thunk()
                c = update(c, [s if off is None else s - off], [j])
            return c

        return lax.cond(jnp.max(m_new - m) <= MAX_LEAD, lambda: fast, redo)

    j_near = jnp.maximum(i - 1, 0)
    edge = LANES
    assert edge >= MAX_DISTANCE and tq >= 2 * edge

    def near_scores():
        s = scores(j_near)
        corner = bias_s[1, tq - edge:tq, 0:edge]
        bot = s[tq - edge:]
        bot = jnp.concatenate([bot[:, :edge] + corner, bot[:, edge:tq],
                               bot[:, tq:tq + edge] + corner, bot[:, tq + edge:]], axis=1)
        return jnp.concatenate([s[:tq - edge], bot], axis=0)

    s_diag = biased(scores(i), bias_s[0])
    m0 = jnp.max(s_diag[:DIFF_DH], axis=0, keepdims=True)
    carry = (m0, jnp.zeros((1, 2 * tq), F32), jnp.zeros((DIFF_DV, 2 * tq), F32))
    no_near = jnp.where(i == 0, -NEG_BIG, 0.0)
    carry = group_update(carry, [(i, lambda: s_diag, None), (j_near, near_scores, no_near)])
    n_far = jnp.maximum(i - 1, 0)
    done = 0
    for width in FAR_UNROLLS:
        def body(g, c, width=width, done=done):
            js = [done + width * g + u for u in range(width)]
            return group_update(c, [(j, functools.partial(scores, j), None) for j in js])
        n_groups = (n_far - done) // width
        carry = lax.fori_loop(0, n_groups, body, carry)
        done = done + n_groups * width
    m, l, acc = carry

    lam = (jnp.exp(jnp.sum(lam_ref[0:1, :] * lam_ref[1:2, :], axis=-1, keepdims=True))
           - jnp.exp(jnp.sum(lam_ref[2:3, :] * lam_ref[3:4, :], axis=-1, keepdims=True)) + lam_init)
    ot = acc[:, :tq] / l[:, :tq] - lam * (acc[:, tq:] / l[:, tq:])
    o = jnp.concatenate([ot[:, j * LANES:(j + 1) * LANES].T for j in range(tq // LANES)], axis=0)
    o_ref[0] = (_rms(o, nw_ref[...], DIFF_NORM_EPS) * (1.0 - lam_init)).astype(BF16)


def _attn(dd, lams, bias_tab, nw, B, T, tq, lam_init):
    assert T % tq == 0
    nq = T // tq
    dd3 = dd.reshape(B, T, dd.shape[1])
    kern = functools.partial(_attn_kernel, tq=tq, lam_init=lam_init)
    out = pl.pallas_call(
        kern,
        grid=(B, DIFF_HEADS, nq),
        in_specs=[
            pl.BlockSpec((4, DIFF_DH), lambda b, h, i: (0, 0)),
            pl.BlockSpec((1, 2, 1, 2 * tq), lambda b, h, i: (h, 0, 0, 0)),
            pl.BlockSpec((1, tq, LANES), lambda b, h, i: (b, i, h)),
            pl.BlockSpec((1, T, LANES), lambda b, h, i: (b, 0, DIFF_HEADS + h)),
            pl.BlockSpec((1, T, LANES), lambda b, h, i: (b, 0, 2 * DIFF_HEADS + h)),
            pl.BlockSpec((1, DIFF_DV), lambda b, h, i: (0, 0)),
        ],
        out_specs=pl.BlockSpec((1, tq, DIFF_DV), lambda b, h, i: (b, i, h)),
        out_shape=jax.ShapeDtypeStruct((B, T, DIFF_V), BF16),
        scratch_shapes=[pltpu.VMEM((2, tq, tq), F32),
                        pltpu.VMEM((nq, DIFF_DV, tq), BF16)],
        compiler_params=pltpu.CompilerParams(
            dimension_semantics=("parallel", "parallel", "arbitrary"),
            vmem_limit_bytes=VMEM_LIMIT),
        name="diff_attn",
    )(lams, bias_tab, dd3, dd3, dd3, nw.reshape(1, DIFF_DV))
    return out.reshape(B * T, DIFF_V)


def _t5_bucket(rel):
    n = jnp.maximum(rel, 0)
    max_exact = NUM_BUCKETS // 2
    nf = jnp.maximum(n, 1).astype(F32)
    large = max_exact + (jnp.log(nf / max_exact) / math.log(MAX_DISTANCE / max_exact)
                         * (NUM_BUCKETS - max_exact)).astype(jnp.int32)
    large = jnp.minimum(large, NUM_BUCKETS - 1)
    return jnp.where(n < max_exact, n, large)


def _bias_tables(rel_bias, tq):
    assert tq + 1 >= MAX_DISTANCE
    m = jnp.arange(2 * tq)
    far = rel_bias[NUM_BUCKETS - 1].astype(F32)
    tabs = []
    for d in (0, tq):
        rel = jnp.where(m < tq, d + m, d + m - 2 * tq)
        b = (rel_bias[_t5_bucket(rel)].astype(F32) - far) * math.log2(math.e)
        tabs.append(jnp.where((rel >= 0)[:, None], b, NEG_BIG).T)
    return jnp.stack(tabs, axis=1)[:, :, None, :]


def _ffn_kernel(x_ref, ola_ref, od_ref, wout_ref, nw_ref, wgu_ref, wd_ref, fnw_ref,
                out_ref, *, final_norm):
    d_ff = wd_ref.shape[0]
    y = (x_ref[...] + _dot(ola_ref[...], wout_ref[0:LA_V, :])
         + _dot(od_ref[...], wout_ref[LA_V:LA_V + DIFF_V, :]))
    h = _rms(y, nw_ref[...], NORM_EPS).astype(BF16)
    gate = _dot(h, wgu_ref[:, 0:d_ff])
    up = _dot(h, wgu_ref[:, d_ff:2 * d_ff])
    y = y + _dot((_silu(gate) * up).astype(BF16), wd_ref[...])
    if final_norm:
        y = _rms(y, fnw_ref[...], NORM_EPS)
    out_ref[...] = y


def _ffn(xf, o_la, o_d, wout, nw, wgu, wdn, fnw, tm, final_norm):
    M, D = xf.shape
    kern = functools.partial(_ffn_kernel, final_norm=final_norm)
    resident = dict(pipeline_mode=pl.Buffered(1))
    return pl.pallas_call(
        kern,
        grid=(M // tm,),
        in_specs=[
            pl.BlockSpec((tm, D), lambda i: (i, 0)),
            pl.BlockSpec((tm, LA_V), lambda i: (i, 0)),
            pl.BlockSpec((tm, DIFF_V), lambda i: (i, 0)),
            pl.BlockSpec(wout.shape, lambda i: (0, 0), **resident),
            pl.BlockSpec((1, D), lambda i: (0, 0)),
            pl.BlockSpec(wgu.shape, lambda i: (0, 0), **resident),
            pl.BlockSpec(wdn.shape, lambda i: (0, 0), **resident),
            pl.BlockSpec((1, D), lambda i: (0, 0)),
        ],
        out_specs=pl.BlockSpec((tm, D), lambda i: (i, 0)),
        out_shape=jax.ShapeDtypeStruct((M, D), F32),
        compiler_params=pltpu.CompilerParams(
            dimension_semantics=("parallel",), vmem_limit_bytes=VMEM_LIMIT),
        name="out_proj_ffn",
    )(xf, o_la, o_d, wout, nw.reshape(1, D), wgu, wdn, fnw.reshape(1, D))


def _pick(n, pref):
    return pref if n % pref == 0 else n


def kernel(x, attn_norm_w, w_in, conv_w, a_log, dt_bias, la_norm_w, lambda_q1, lambda_k1,
           lambda_q2, lambda_k2, diff_norm_w, rel_bias, w_out, ffn_norm_w, w_gate_up,
           w_down, final_norm_w):
    B, T, D = x.shape
    depth = w_in.shape[0]
    M = B * T
    d_ff = w_down.shape[1]
    tm_proj = _pick(M, 512)
    tm_ffn = _pick(M, 512)
    tb = _pick(T, 256)
    chunk = 64
    tq = _pick(T, 512)

    n_la = 2 * LA_QK + 2 * LA_V
    bias_tab = _bias_tables(rel_bias, tq)
    lane_pad = BA_PAD - 2 * LA_HEADS
    dscale = jnp.concatenate([jnp.full((DIFF_QK,), DIFF_DH ** -0.5 * math.log2(math.e), F32),
                              jnp.ones((DIFF_QK + DIFF_V,), F32)]).reshape(1, -1)

    xf = x.reshape(M, D)
    for l in range(depth):
        w = w_in[l]
        wla = jnp.concatenate(
            [w[:, :n_la + 2 * LA_HEADS], jnp.zeros((D, lane_pad), w.dtype)], axis=1).astype(BF16)
        wd = w[:, n_la + 2 * LA_HEADS:].astype(BF16)
        la, dd = _in_proj(xf, attn_norm_w[l], wla, wd, dscale, tm_proj)

        avec = jnp.concatenate([jnp.zeros((LA_HEADS,), F32), a_log[l].astype(F32),
                                jnp.zeros((lane_pad,), F32)]).reshape(1, BA_PAD)
        dtvec = jnp.concatenate([jnp.zeros((LA_HEADS,), F32), dt_bias[l].astype(F32),
                                 jnp.zeros((lane_pad,), F32)]).reshape(1, BA_PAD)
        o_la = _gdn(la, conv_w[l], avec, dtvec, la_norm_w[l], B, T, tb, chunk)

        lam_init = 0.8 - 0.6 * math.exp(-0.3 * l)
        lams = jnp.stack([lambda_q1[l], lambda_k1[l], lambda_q2[l], lambda_k2[l]]).astype(F32)
        o_d = _attn(dd, lams, bias_tab, diff_norm_w[l], B, T, tq, lam_init)

        xf = _ffn(xf, o_la, o_d, w_out[l].astype(BF16), ffn_norm_w[l],
                  w_gate_up[l].astype(BF16), w_down[l].astype(BF16), final_norm_w,
                  tm_ffn, final_norm=(l == depth - 1))
    return xf.reshape(B, T, D)
```

```python
import functools
import math

import jax
import jax.numpy as jnp
from jax import lax
from jax.experimental import pallas as pl
from jax.experimental.pallas import tpu as pltpu

LA_HEADS = 4
LA_DK = 128
LA_DV = 128
LA_QK = LA_HEADS * LA_DK
LA_V = LA_HEADS * LA_DV
CONV_K = 4
DIFF_HEADS = 4
DIFF_DH = 64
DIFF_DV = 2 * DIFF_DH
DIFF_QK = DIFF_HEADS * 2 * DIFF_DH
DIFF_V = DIFF_HEADS * DIFF_DV
NUM_BUCKETS = 32
MAX_DISTANCE = 128
NORM_EPS = 1e-6
DIFF_NORM_EPS = 1e-5
L2_EPS = 1e-6

LANES = 128
SUBLANES = 8
BA_PAD = LANES
LA_WIDTH = 2 * LA_QK + 2 * LA_V + BA_PAD
NEG_BIG = -1e30
FAR_UNROLLS = (4, 2, 1)
HEADS_PER_STEP = 2
MAX_LEAD = 40.0

VMEM_LIMIT = 52 * 1024 * 1024

F32 = jnp.float32
BF16 = jnp.bfloat16
HIGHEST = lax.Precision.HIGHEST


def _dot(a, b, precision=None):
    return jnp.dot(a, b, preferred_element_type=F32, precision=precision)


def _dot_nt(a, b, precision=None):
    return lax.dot_general(a, b, (((1,), (1,)), ((), ())),
                           preferred_element_type=F32, precision=precision)


def _dot_tn(a, b, precision=None):
    return lax.dot_general(a, b, (((0,), (0,)), ((), ())),
                           preferred_element_type=F32, precision=precision)


def _rms(x, w, eps):
    return x * lax.rsqrt(jnp.mean(x * x, axis=-1, keepdims=True) + eps) * w


def _silu(x):
    h = 0.5 * x
    return h + h * jnp.tanh(h)


def _softplus(x):
    return jnp.maximum(x, 0.0) + jnp.log1p(jnp.exp(-jnp.abs(x)))


def _in_proj_kernel(x_ref, nw_ref, wla_ref, wd_ref, dscale_ref, la_ref, d_ref):
    h = _rms(x_ref[...], nw_ref[...], NORM_EPS).astype(BF16)
    la_ref[...] = _dot(h, wla_ref[...])
    d_ref[...] = (_dot(h, wd_ref[...]) * dscale_ref[...]).astype(BF16)


def _in_proj(xf, nw, wla, wd, dscale, tm):
    M, D = xf.shape
    return pl.pallas_call(
        _in_proj_kernel,
        grid=(M // tm,),
        in_specs=[
            pl.BlockSpec((tm, D), lambda i: (i, 0)),
            pl.BlockSpec((1, D), lambda i: (0, 0)),
            pl.BlockSpec(wla.shape, lambda i: (0, 0)),
            pl.BlockSpec(wd.shape, lambda i: (0, 0)),
            pl.BlockSpec(dscale.shape, lambda i: (0, 0)),
        ],
        out_specs=[
            pl.BlockSpec((tm, wla.shape[1]), lambda i: (i, 0)),
            pl.BlockSpec((tm, wd.shape[1]), lambda i: (i, 0)),
        ],
        out_shape=[
            jax.ShapeDtypeStruct((M, wla.shape[1]), F32),
            jax.ShapeDtypeStruct((M, wd.shape[1]), BF16),
        ],
        compiler_params=pltpu.CompilerParams(
            dimension_semantics=("parallel",), vmem_limit_bytes=VMEM_LIMIT),
        name="in_proj",
    )(xf, nw.reshape(1, D), wla, wd, dscale)


def _unit_lower_inverse_minus_eye(Ls, C):
    row = lax.broadcasted_iota(jnp.int32, (C, C), 0)
    col = lax.broadcasted_iota(jnp.int32, (C, C), 1)
    xs = None
    s = 1
    while s < C:
        same_pair = (row // (2 * s)) == (col // (2 * s))
        low_left = same_pair & ((row // s) % 2 == 1) & ((col // s) % 2 == 0)
        offs = [jnp.where(low_left, L, 0.0) for L in Ls]
        if xs is None:
            xs = [-m for m in offs]
        else:
            xb = [x.astype(BF16) for x in xs]
            ys = [m + _dot(m.astype(BF16), x) for m, x in zip(offs, xb)]
            xs = [x - y - _dot(x16, y.astype(BF16)) for x, x16, y in zip(xs, xb, ys)]
        s *= 2
    return xs


def _gdn_kernel(qkv_ref, z_ref, ba_ref, convw_ref, avec_ref, dtvec_ref, nw_ref,
                o_ref, xbuf, state, y_s, g_s, gt_s, beta_s, *, TB, C):
    t = pl.program_id(0)
    W = 2 * LA_QK + LA_V
    B = qkv_ref.shape[0]
    nC = TB // C

    @pl.when(t == 0)
    def _():
        xbuf[:, 0:SUBLANES, :] = jnp.zeros((B, SUBLANES, W), F32)
        state[...] = jnp.zeros(state.shape, F32)

    row = lax.broadcasted_iota(jnp.int32, (C, C), 0)
    col = lax.broadcasted_iota(jnp.int32, (C, C), 1)
    causal = row >= col
    strict = row > col
    tril = causal.astype(F32)
    ba_lane = lax.broadcasted_iota(jnp.int32, (1, BA_PAD), 1)
    is_g = (ba_lane >= LA_HEADS) & (ba_lane < 2 * LA_HEADS)

    for b in range(B):
        xbuf[b, SUBLANES:SUBLANES + TB, :] = qkv_ref[b]
        conv = convw_ref[CONV_K - 1:CONV_K, :] * xbuf[b, SUBLANES:SUBLANES + TB, :]
        for i in range(CONV_K - 1):
            off = SUBLANES - (CONV_K - 1) + i
            conv = conv + convw_ref[i:i + 1, :] * xbuf[b, off:off + TB, :]
        xbuf[b, 0:SUBLANES, :] = xbuf[b, TB:TB + SUBLANES, :]
        y = _silu(conv)

        for h in range(LA_HEADS):
            qs = slice(h * LA_DK, (h + 1) * LA_DK)
            ks = slice(LA_QK + h * LA_DK, LA_QK + (h + 1) * LA_DK)
            q = y[:, qs]
            k = y[:, ks]
            y_s[b, :, qs] = (q * lax.rsqrt(jnp.sum(q * q, -1, keepdims=True) + L2_EPS)
                             * (LA_DK ** -0.5))
            y_s[b, :, ks] = k * lax.rsqrt(jnp.sum(k * k, -1, keepdims=True) + L2_EPS)
        y_s[b, :, 2 * LA_QK:] = y[:, 2 * LA_QK:]

        ba = ba_ref[b]
        beta_s[b] = jax.nn.sigmoid(ba)
        g_raw = -jnp.where(is_g, jnp.exp(avec_ref[...]), 0.0) * _softplus(ba + dtvec_ref[...])
        for c in range(nC):
            g_s[b, c * C:(c + 1) * C, :] = _dot(tril, g_raw[c * C:(c + 1) * C, :], HIGHEST)
        for j in range(TB // LANES):
            gt_s[b, :, j * LANES:(j + 1) * LANES] = g_s[b, j * LANES:(j + 1) * LANES, :].T

    probs = [(b, c, h) for b in range(B) for c in range(nC) for h in range(LA_HEADS)]
    index = {p: n for n, p in enumerate(probs)}
    Ls, intras, rhss, qgs, kds, egls = [], [], [], [], [], []
    for b, c, h in probs:
        rs = slice(c * C, (c + 1) * C)
        qn = y_s[b, rs, h * LA_DK:(h + 1) * LA_DK]
        kn = y_s[b, rs, LA_QK + h * LA_DK:LA_QK + (h + 1) * LA_DK]
        v = y_s[b, rs, 2 * LA_QK + h * LA_DV:2 * LA_QK + (h + 1) * LA_DV]
        gb = jnp.broadcast_to(g_s[b, rs, LA_HEADS + h:LA_HEADS + h + 1], (C, LANES))
        beta = jnp.broadcast_to(beta_s[b, rs, h:h + 1], (C, LANES))
        g_row = gt_s[b, LA_HEADS + h:LA_HEADS + h + 1, rs]
        gdiff = gb[:, :C] - g_row
        decay = jnp.where(causal, jnp.exp(jnp.where(causal, gdiff, 0.0)), 0.0)
        kb = kn * beta
        kq = _dot_nt(jnp.concatenate([kb, qn], axis=0).astype(BF16), kn.astype(BF16))
        Ls.append(jnp.where(strict, kq[:C] * decay, 0.0))
        intras.append((kq[C:] * decay).astype(BF16))
        eg = jnp.exp(gb)
        rhss.append(jnp.concatenate([v * beta, kb * eg], axis=1))
        qgs.append((qn * eg).astype(BF16))
        g_last = gb[C - 1:C, :]
        kds.append((kn * jnp.exp(g_last - gb)).astype(BF16))
        egls.append(jnp.exp(g_last))

    xs = _unit_lower_inverse_minus_eye(Ls, C)
    sols = [r + _dot(x.astype(BF16), r.astype(BF16)) for x, r in zip(xs, rhss)]

    streams = [(b, h) for b in range(B) for h in range(LA_HEADS)]
    S = [state[b * LA_HEADS + h] for b, h in streams]
    for c in range(nC):
        ps = [index[(b, c, h)] for b, h in streams]
        r2 = [_dot(jnp.concatenate([sols[p][:, LA_DV:].astype(BF16), qgs[p]], axis=0),
                   S[n].astype(BF16)) for n, p in enumerate(ps)]
        vn = [(sols[p][:, :LA_DV] - r2[n][:C]).astype(BF16) for n, p in enumerate(ps)]
        S = [S[n] * egls[p] + _dot_tn(kds[p], vn[n]) for n, p in enumerate(ps)]
        rs = slice(c * C, (c + 1) * C)
        for n, ((b, h), p) in enumerate(zip(streams, ps)):
            o = r2[n][C:] + _dot(intras[p], vn[n])
            zg = _silu(z_ref[b, rs, h * LA_DV:(h + 1) * LA_DV])
            o_ref[b, rs, h * LA_DV:(h + 1) * LA_DV] = (
                _rms(o, nw_ref[...], NORM_EPS) * zg).astype(BF16)
    for n in range(len(streams)):
        state[n] = S[n]


def _gdn(la, convw, avec, dtvec, nw, B, T, TB, C):
    nT = T // TB
    W = 2 * LA_QK + LA_V
    la3 = la.reshape(B, T, la.shape[1])
    kern = functools.partial(_gdn_kernel, TB=TB, C=C)
    out = pl.pallas_call(
        kern,
        grid=(nT,),
        in_specs=[
            pl.BlockSpec((B, TB, W), lambda t: (0, t, 0)),
            pl.BlockSpec((B, TB, LA_V), lambda t: (0, t, W // LA_V)),
            pl.BlockSpec((B, TB, BA_PAD), lambda t: (0, t, (W + LA_V) // BA_PAD)),
            pl.BlockSpec((CONV_K, W), lambda t: (0, 0)),
            pl.BlockSpec((1, BA_PAD), lambda t: (0, 0)),
            pl.BlockSpec((1, BA_PAD), lambda t: (0, 0)),
            pl.BlockSpec((1, LA_DV), lambda t: (0, 0)),
        ],
        out_specs=pl.BlockSpec((B, TB, LA_V), lambda t: (0, t, 0)),
        out_shape=jax.ShapeDtypeStruct((B, T, LA_V), BF16),
        scratch_shapes=[
            pltpu.VMEM((B, TB + SUBLANES, W), F32),
            pltpu.VMEM((B * LA_HEADS, LA_DK, LA_DV), F32),
            pltpu.VMEM((B, TB, W), F32),
            pltpu.VMEM((B, TB, BA_PAD), F32),
            pltpu.VMEM((B, LANES, TB), F32),
            pltpu.VMEM((B, TB, BA_PAD), F32),
        ],
        compiler_params=pltpu.CompilerParams(
            dimension_semantics=("arbitrary",), vmem_limit_bytes=VMEM_LIMIT),
        name="gdn",
    )(la3, la3, la3, convw, avec, dtvec, nw.reshape(1, LA_DV))
    return out.reshape(B * T, LA_V)


def _transpose_blocks(x):
    n = x.shape[0] // LANES
    return jnp.concatenate([x[j * LANES:(j + 1) * LANES, :].T for j in range(n)], axis=1)


def _attn_kernel(lam_ref, tab_ref, q_ref, k_ref, v_ref, nw_ref, o_ref, bias_s, vt_s, *,
                 tq, lam_init):
    i = pl.program_id(2)
    n_blocks = vt_s.shape[1]
    heads = range(HEADS_PER_STEP)

    def cols(hh):
        return slice(hh * LANES, (hh + 1) * LANES)

    @pl.when(i == 0)
    def _():
        for hh in heads:
            for e in range(2):
                x = jnp.broadcast_to(tab_ref[hh, e], (tq, 2 * tq))
                bias_s[hh, e] = pltpu.roll(x, 0, 1, stride=1, stride_axis=0)[:, :tq]
            for j in range(n_blocks):
                vt_s[hh, j] = _transpose_blocks(
                    v_ref[0, j * tq:(j + 1) * tq, cols(hh)].astype(F32)).astype(BF16)

    q2t = []
    for hh in heads:
        qt = _transpose_blocks(q_ref[0, :, cols(hh)].astype(F32))
        sub = lax.broadcasted_iota(jnp.int32, qt.shape, 0)
        q2t.append(jnp.concatenate([jnp.where(sub < DIFF_DH, qt, 0.0),
                                    jnp.where(sub >= DIFF_DH, qt, 0.0)], axis=1).astype(BF16))

    def scores(hh, j):
        kj = k_ref[0, pl.ds(pl.multiple_of(j * tq, tq), tq), cols(hh)]
        return _dot(kj, q2t[hh])

    def biased(s, bias):
        return jnp.concatenate([s[:, :tq] + bias, s[:, tq:] + bias], axis=1)

    def update(hh, carry, s, j):
        m, l, acc = carry
        m_new = jnp.maximum(m, jnp.max(s, axis=0, keepdims=True))
        alpha = jnp.exp2(m - m_new)
        p = jnp.exp2(s - m_new)
        l = alpha * l + jnp.sum(p, axis=0, keepdims=True)
        acc = alpha * acc + _dot(vt_s[hh, j], p.astype(BF16))
        return m_new, l, acc

    def group_update(carries, blocks):
        ms = [c[0] for c in carries]
        ls = [c[1] for c in carries]
        accs = [c[2] for c in carries]
        m_news = list(ms)
        for u in range(len(blocks[0])):
            for hh in heads:
                j, thunk, off = blocks[hh][u]
                s = thunk()
                ref = ms[hh] if off is None else ms[hh] + off
                p = jnp.exp2(s - ref)
                ls[hh] = ls[hh] + jnp.sum(p, axis=0, keepdims=True)
                accs[hh] = accs[hh] + _dot(vt_s[hh, j], p.astype(BF16))
                smax = jnp.max(s, axis=0, keepdims=True)
                m_news[hh] = jnp.maximum(m_news[hh], smax if off is None else smax - off)
        fast = []
        lead = None
        for hh in heads:
            alpha = jnp.exp2(ms[hh] - m_news[hh])
            fast.append((m_news[hh], alpha * ls[hh], alpha * accs[hh]))
            lead_h = jnp.max(m_news[hh] - ms[hh])
            lead = lead_h if lead is None else jnp.maximum(lead, lead_h)

        def redo():
            out = []
            for hh in heads:
                c = carries[hh]
                for j, thunk, off in blocks[hh]:
                    s = thunk()
                    c = update(hh, c, s if off is None else s - off, j)
                out.append(c)
            return tuple(out)

        return lax.cond(lead <= MAX_LEAD, lambda: tuple(fast), redo)

    j_near = jnp.maximum(i - 1, 0)
    edge = LANES
    assert edge >= MAX_DISTANCE and tq >= 2 * edge

    def near_scores(hh):
        s = scores(hh, j_near)
        corner = bias_s[hh, 1, tq - edge:tq, 0:edge]
        bot = s[tq - edge:]
        bot = jnp.concatenate([bot[:, :edge] + corner, bot[:, edge:tq],
                               bot[:, tq:tq + edge] + corner, bot[:, tq + edge:]], axis=1)
        return jnp.concatenate([s[:tq - edge], bot], axis=0)

    no_near = jnp.where(i == 0, -NEG_BIG, 0.0)
    carries, tail = [], []
    for hh in heads:
        s_diag = biased(scores(hh, i), bias_s[hh, 0])
        m0 = jnp.max(s_diag[:DIFF_DH], axis=0, keepdims=True)
        carries.append((m0, jnp.zeros((1, 2 * tq), F32), jnp.zeros((DIFF_DV, 2 * tq), F32)))
        tail.append([(i, functools.partial(lambda s: s, s_diag), None),
                     (j_near, functools.partial(near_scores, hh), no_near)])
    carries = group_update(tuple(carries), tail)
    n_far = jnp.maximum(i - 1, 0)
    done = 0
    for width in FAR_UNROLLS:
        def body(g, c, width=width, done=done):
            js = [done + width * g + u for u in range(width)]
            return group_update(c, [[(j, functools.partial(scores, hh, j), None) for j in js]
                                    for hh in heads])
        n_groups = (n_far - done) // width
        carries = lax.fori_loop(0, n_groups, body, carries)
        done = done + n_groups * width

    lam = (jnp.exp(jnp.sum(lam_ref[0:1, :] * lam_ref[1:2, :], axis=-1, keepdims=True))
           - jnp.exp(jnp.sum(lam_ref[2:3, :] * lam_ref[3:4, :], axis=-1, keepdims=True)) + lam_init)
    for hh in heads:
        m, l, acc = carries[hh]
        ot = acc[:, :tq] / l[:, :tq] - lam * (acc[:, tq:] / l[:, tq:])
        o = jnp.concatenate([ot[:, j * LANES:(j + 1) * LANES].T for j in range(tq // LANES)],
                            axis=0)
        o_ref[0, :, cols(hh)] = (_rms(o, nw_ref[...], DIFF_NORM_EPS)
                                 * (1.0 - lam_init)).astype(BF16)


def _attn(dd, lams, bias_tab, nw, B, T, tq, lam_init):
    assert T % tq == 0 and DIFF_HEADS % HEADS_PER_STEP == 0
    nq = T // tq
    hw = HEADS_PER_STEP * LANES
    dd3 = dd.reshape(B, T, dd.shape[1])
    kern = functools.partial(_attn_kernel, tq=tq, lam_init=lam_init)
    out = pl.pallas_call(
        kern,
        grid=(B, DIFF_HEADS // HEADS_PER_STEP, nq),
        in_specs=[
            pl.BlockSpec((4, DIFF_DH), lambda b, g, i: (0, 0)),
            pl.BlockSpec((HEADS_PER_STEP, 2, 1, 2 * tq), lambda b, g, i: (g, 0, 0, 0)),
            pl.BlockSpec((1, tq, hw), lambda b, g, i: (b, i, g)),
            pl.BlockSpec((1, T, hw), lambda b, g, i: (b, 0, DIFF_QK // hw + g)),
            pl.BlockSpec((1, T, hw), lambda b, g, i: (b, 0, 2 * DIFF_QK // hw + g)),
            pl.BlockSpec((1, DIFF_DV), lambda b, g, i: (0, 0)),
        ],
        out_specs=pl.BlockSpec((1, tq, hw), lambda b, g, i: (b, i, g)),
        out_shape=jax.ShapeDtypeStruct((B, T, DIFF_V), BF16),
        scratch_shapes=[
            pltpu.VMEM((HEADS_PER_STEP, 2, tq, tq), F32),
            pltpu.VMEM((HEADS_PER_STEP, nq, DIFF_DV, tq), BF16),
        ],
        compiler_params=pltpu.CompilerParams(
            dimension_semantics=("parallel", "parallel", "arbitrary"),
            vmem_limit_bytes=VMEM_LIMIT),
        name="diff_attn",
    )(lams, bias_tab, dd3, dd3, dd3, nw.reshape(1, DIFF_DV))
    return out.reshape(B * T, DIFF_V)


def _t5_bucket(rel):
    n = jnp.maximum(rel, 0)
    max_exact = NUM_BUCKETS // 2
    nf = jnp.maximum(n, 1).astype(F32)
    large = max_exact + (jnp.log(nf / max_exact) / math.log(MAX_DISTANCE / max_exact)
                         * (NUM_BUCKETS - max_exact)).astype(jnp.int32)
    large = jnp.minimum(large, NUM_BUCKETS - 1)
    return jnp.where(n < max_exact, n, large)


def _bias_tables(rel_bias, tq):
    assert tq + 1 >= MAX_DISTANCE
    m = jnp.arange(2 * tq)
    far = rel_bias[NUM_BUCKETS - 1].astype(F32)
    tabs = []
    for d in (0, tq):
        rel = jnp.where(m < tq, d + m, d + m - 2 * tq)
        b = (rel_bias[_t5_bucket(rel)].astype(F32) - far) * math.log2(math.e)
        tabs.append(jnp.where((rel >= 0)[:, None], b, NEG_BIG).T)
    return jnp.stack(tabs, axis=1)[:, :, None, :]


def _ffn_kernel(x_ref, ola_ref, od_ref, wout_ref, nw_ref, wgu_ref, wd_ref, fnw_ref,
                out_ref, *, final_norm):
    d_ff = wd_ref.shape[0]
    y = (x_ref[...] + _dot(ola_ref[...], wout_ref[0:LA_V, :])
         + _dot(od_ref[...], wout_ref[LA_V:LA_V + DIFF_V, :]))
    h = _rms(y, nw_ref[...], NORM_EPS).astype(BF16)
    gate = _dot(h, wgu_ref[:, 0:d_ff])
    up = _dot(h, wgu_ref[:, d_ff:2 * d_ff])
    y = y + _dot((_silu(gate) * up).astype(BF16), wd_ref[...])
    if final_norm:
        y = _rms(y, fnw_ref[...], NORM_EPS)
    out_ref[...] = y


def _ffn(xf, o_la, o_d, wout, nw, wgu, wdn, fnw, tm, final_norm):
    M, D = xf.shape
    kern = functools.partial(_ffn_kernel, final_norm=final_norm)
    resident = dict(pipeline_mode=pl.Buffered(1))
    return pl.pallas_call(
        kern,
        grid=(M // tm,),
        in_specs=[
            pl.BlockSpec((tm, D), lambda i: (i, 0)),
            pl.BlockSpec((tm, LA_V), lambda i: (i, 0)),
            pl.BlockSpec((tm, DIFF_V), lambda i: (i, 0)),
            pl.BlockSpec(wout.shape, lambda i: (0, 0), **resident),
            pl.BlockSpec((1, D), lambda i: (0, 0)),
            pl.BlockSpec(wgu.shape, lambda i: (0, 0), **resident),
            pl.BlockSpec(wdn.shape, lambda i: (0, 0), **resident),
            pl.BlockSpec((1, D), lambda i: (0, 0)),
        ],
        out_specs=pl.BlockSpec((tm, D), lambda i: (i, 0)),
        out_shape=jax.ShapeDtypeStruct((M, D), F32),
        compiler_params=pltpu.CompilerParams(
            dimension_semantics=("parallel",), vmem_limit_bytes=VMEM_LIMIT),
        name="out_proj_ffn",
    )(xf, o_la, o_d, wout, nw.reshape(1, D), wgu, wdn, fnw.reshape(1, D))


def _pick(n, pref):
    return pref if n % pref == 0 else n


def kernel(x, attn_norm_w, w_in, conv_w, a_log, dt_bias, la_norm_w, lambda_q1, lambda_k1,
           lambda_q2, lambda_k2, diff_norm_w, rel_bias, w_out, ffn_norm_w, w_gate_up,
           w_down, final_norm_w):
    B, T, D = x.shape
    depth = w_in.shape[0]
    M = B * T
    d_ff = w_down.shape[1]
    tm_proj = _pick(M, 512)
    tm_ffn = _pick(M, 512)
    tb = _pick(T, 256)
    chunk = 64
    tq = _pick(T, 512)

    n_la = 2 * LA_QK + 2 * LA_V
    bias_tab = _bias_tables(rel_bias, tq)
    lane_pad = BA_PAD - 2 * LA_HEADS
    dscale = jnp.concatenate([jnp.full((DIFF_QK,), DIFF_DH ** -0.5 * math.log2(math.e), F32),
                              jnp.ones((DIFF_QK + DIFF_V,), F32)]).reshape(1, -1)

    xf = x.reshape(M, D)
    for l in range(depth):
        w = w_in[l]
        wla = jnp.concatenate(
            [w[:, :n_la + 2 * LA_HEADS], jnp.zeros((D, lane_pad), w.dtype)], axis=1).astype(BF16)
        wd = w[:, n_la + 2 * LA_HEADS:].astype(BF16)
        la, dd = _in_proj(xf, attn_norm_w[l], wla, wd, dscale, tm_proj)

        avec = jnp.concatenate([jnp.zeros((LA_HEADS,), F32), a_log[l].astype(F32),
                                jnp.zeros((lane_pad,), F32)]).reshape(1, BA_PAD)
        dtvec = jnp.concatenate([jnp.zeros((LA_HEADS,), F32), dt_bias[l].astype(F32),
                                 jnp.zeros((lane_pad,), F32)]).reshape(1, BA_PAD)
        o_la = _gdn(la, conv_w[l], avec, dtvec, la_norm_w[l], B, T, tb, chunk)

        lam_init = 0.8 - 0.6 * math.exp(-0.3 * l)
        lams = jnp.stack([lambda_q1[l], lambda_k1[l], lambda_q2[l], lambda_k2[l]]).astype(F32)
        o_d = _attn(dd, lams, bias_tab, diff_norm_w[l], B, T, tq, lam_init)

        xf = _ffn(xf, o_la, o_d, w_out[l].astype(BF16), ffn_norm_w[l],
                  w_gate_up[l].astype(BF16), w_down[l].astype(BF16), final_norm_w,
                  tm_ffn, final_norm=(l == depth - 1))
    return xf.reshape(B, T, D)
```

```python
import functools
import math

import jax
import jax.numpy as jnp
from jax import lax
from jax.experimental import pallas as pl
from jax.experimental.pallas import tpu as pltpu

LA_HEADS = 4
LA_DK = 128
LA_DV = 128
LA_QK = LA_HEADS * LA_DK
LA_V = LA_HEADS * LA_DV
CONV_K = 4
DIFF_HEADS = 4
DIFF_DH = 64
DIFF_DV = 2 * DIFF_DH
DIFF_QK = DIFF_HEADS * 2 * DIFF_DH
DIFF_V = DIFF_HEADS * DIFF_DV
NUM_BUCKETS = 32
MAX_DISTANCE = 128
NORM_EPS = 1e-6
DIFF_NORM_EPS = 1e-5
L2_EPS = 1e-6

LANES = 128
SUBLANES = 8
BA_PAD = LANES
LA_WIDTH = 2 * LA_QK + 2 * LA_V + BA_PAD
NEG_BIG = -1e30
FAR_UNROLLS = (4, 2, 1)
HEADS_PER_STEP = 2
CONV_SLAB = 256
MAX_LEAD = 40.0

VMEM_LIMIT = 52 * 1024 * 1024

F32 = jnp.float32
BF16 = jnp.bfloat16
HIGHEST = lax.Precision.HIGHEST


def _dot(a, b, precision=None):
    return jnp.dot(a, b, preferred_element_type=F32, precision=precision)


def _dot_nt(a, b, precision=None):
    return lax.dot_general(a, b, (((1,), (1,)), ((), ())),
                           preferred_element_type=F32, precision=precision)


def _dot_tn(a, b, precision=None):
    return lax.dot_general(a, b, (((0,), (0,)), ((), ())),
                           preferred_element_type=F32, precision=precision)


def _rms(x, w, eps):
    return x * lax.rsqrt(jnp.mean(x * x, axis=-1, keepdims=True) + eps) * w


def _silu(x):
    h = 0.5 * x
    return h + h * jnp.tanh(h)


def _softplus(x):
    return jnp.maximum(x, 0.0) + jnp.log1p(jnp.exp(-jnp.abs(x)))


def _in_proj_kernel(x_ref, nw_ref, wqkv_ref, wzba_ref, wd_ref, dscale_ref, convw_ref,
                    la_ref, d_ref, xbuf, *, tiles_per_seq):
    i = pl.program_id(0)
    tm = x_ref.shape[0]
    W = 2 * LA_QK + LA_V

    n_slabs, _, ws = xbuf.shape

    @pl.when(i % tiles_per_seq == 0)
    def _():
        xbuf[:, 0:SUBLANES, :] = jnp.zeros((n_slabs, SUBLANES, ws), F32)

    h = _rms(x_ref[...], nw_ref[...], NORM_EPS).astype(BF16)
    for s in range(n_slabs):
        xbuf[s, SUBLANES:SUBLANES + tm, :] = _dot(h, wqkv_ref[s])
        c0 = s * ws
        conv = convw_ref[CONV_K - 1:CONV_K, c0:c0 + ws] * xbuf[s, SUBLANES:SUBLANES + tm, :]
        for k in range(CONV_K - 1):
            off = SUBLANES - (CONV_K - 1) + k
            conv = conv + convw_ref[k:k + 1, c0:c0 + ws] * xbuf[s, off:off + tm, :]
        xbuf[s, 0:SUBLANES, :] = xbuf[s, tm:tm + SUBLANES, :]
        y = _silu(conv)
        if c0 >= 2 * LA_QK:
            la_ref[:, c0:c0 + ws] = y
            continue
        scale = LA_DK ** -0.5 if c0 < LA_QK else 1.0
        for hd in range(ws // LA_DK):
            yh = y[:, hd * LA_DK:(hd + 1) * LA_DK]
            yn = yh * lax.rsqrt(jnp.sum(yh * yh, -1, keepdims=True) + L2_EPS)
            la_ref[:, c0 + hd * LA_DK:c0 + (hd + 1) * LA_DK] = yn * scale if scale != 1.0 else yn
    d_ref[...] = (_dot(h, wd_ref[...]) * dscale_ref[...]).astype(BF16)
    la_ref[:, W:] = _dot(h, wzba_ref[...])


def _in_proj(xf, nw, wqkv, wzba, wd, dscale, convw, tm, T):
    M, D = xf.shape
    assert T % tm == 0
    W = 2 * LA_QK + LA_V
    kern = functools.partial(_in_proj_kernel, tiles_per_seq=T // tm)
    return pl.pallas_call(
        kern,
        grid=(M // tm,),
        in_specs=[
            pl.BlockSpec((tm, D), lambda i: (i, 0)),
            pl.BlockSpec((1, D), lambda i: (0, 0)),
            pl.BlockSpec(wqkv.shape, lambda i: (0, 0, 0)),
            pl.BlockSpec(wzba.shape, lambda i: (0, 0)),
            pl.BlockSpec(wd.shape, lambda i: (0, 0)),
            pl.BlockSpec(dscale.shape, lambda i: (0, 0)),
            pl.BlockSpec(convw.shape, lambda i: (0, 0)),
        ],
        out_specs=[
            pl.BlockSpec((tm, LA_WIDTH), lambda i: (i, 0)),
            pl.BlockSpec((tm, wd.shape[1]), lambda i: (i, 0)),
        ],
        out_shape=[
            jax.ShapeDtypeStruct((M, LA_WIDTH), F32),
            jax.ShapeDtypeStruct((M, wd.shape[1]), BF16),
        ],
        scratch_shapes=[pltpu.VMEM((wqkv.shape[0], tm + SUBLANES, wqkv.shape[2]), F32)],
        compiler_params=pltpu.CompilerParams(
            dimension_semantics=("arbitrary",), vmem_limit_bytes=VMEM_LIMIT),
        name="in_proj",
    )(xf, nw.reshape(1, D), wqkv, wzba, wd, dscale, convw)


def _unit_lower_inverse_minus_eye(Ls, C):
    row = lax.broadcasted_iota(jnp.int32, (C, C), 0)
    col = lax.broadcasted_iota(jnp.int32, (C, C), 1)
    xs = None
    s = 1
    while s < C:
        same_pair = (row // (2 * s)) == (col // (2 * s))
        low_left = same_pair & ((row // s) % 2 == 1) & ((col // s) % 2 == 0)
        offs = [jnp.where(low_left, L, 0.0) for L in Ls]
        if xs is None:
            xs = [-m for m in offs]
        else:
            xb = [x.astype(BF16) for x in xs]
            ys = [m + _dot(m.astype(BF16), x) for m, x in zip(offs, xb)]
            xs = [x - y - _dot(x16, y.astype(BF16)) for x, x16, y in zip(xs, xb, ys)]
        s *= 2
    return xs


def _gdn_kernel(qkv_ref, z_ref, ba_ref, avec_ref, dtvec_ref, nw_ref,
                o_ref, state, g_s, gt_s, beta_s, *, TB, C):
    t = pl.program_id(0)
    B = qkv_ref.shape[0]
    nC = TB // C

    @pl.when(t == 0)
    def _():
        state[...] = jnp.zeros(state.shape, F32)

    row = lax.broadcasted_iota(jnp.int32, (C, C), 0)
    col = lax.broadcasted_iota(jnp.int32, (C, C), 1)
    causal = row >= col
    strict = row > col
    tril = causal.astype(F32)
    ba_lane = lax.broadcasted_iota(jnp.int32, (1, BA_PAD), 1)
    is_g = (ba_lane >= LA_HEADS) & (ba_lane < 2 * LA_HEADS)

    for b in range(B):
        ba = ba_ref[b]
        beta_s[b] = jax.nn.sigmoid(ba)
        g_raw = -jnp.where(is_g, jnp.exp(avec_ref[...]), 0.0) * _softplus(ba + dtvec_ref[...])
        for c in range(nC):
            g_s[b, c * C:(c + 1) * C, :] = _dot(tril, g_raw[c * C:(c + 1) * C, :], HIGHEST)
        for j in range(TB // LANES):
            gt_s[b, :, j * LANES:(j + 1) * LANES] = g_s[b, j * LANES:(j + 1) * LANES, :].T

    probs = [(b, c, h) for b in range(B) for c in range(nC) for h in range(LA_HEADS)]
    index = {p: n for n, p in enumerate(probs)}
    Ls, intras, rhss, qgs, kds, egls = [], [], [], [], [], []
    for b, c, h in probs:
        rs = slice(c * C, (c + 1) * C)
        qn = qkv_ref[b, rs, h * LA_DK:(h + 1) * LA_DK]
        kn = qkv_ref[b, rs, LA_QK + h * LA_DK:LA_QK + (h + 1) * LA_DK]
        v = qkv_ref[b, rs, 2 * LA_QK + h * LA_DV:2 * LA_QK + (h + 1) * LA_DV]
        gb = jnp.broadcast_to(g_s[b, rs, LA_HEADS + h:LA_HEADS + h + 1], (C, LANES))
        beta = jnp.broadcast_to(beta_s[b, rs, h:h + 1], (C, LANES))
        g_row = gt_s[b, LA_HEADS + h:LA_HEADS + h + 1, rs]
        gdiff = gb[:, :C] - g_row
        decay = jnp.where(causal, jnp.exp(jnp.where(causal, gdiff, 0.0)), 0.0)
        kb = kn * beta
        kq = _dot_nt(jnp.concatenate([kb, qn], axis=0).astype(BF16), kn.astype(BF16))
        Ls.append(jnp.where(strict, kq[:C] * decay, 0.0))
        intras.append((kq[C:] * decay).astype(BF16))
        eg = jnp.exp(gb)
        rhss.append(jnp.concatenate([v * beta, kb * eg], axis=1))
        qgs.append((qn * eg).astype(BF16))
        g_last = gb[C - 1:C, :]
        kds.append((kn * jnp.exp(g_last - gb)).astype(BF16))
        egls.append(jnp.exp(g_last))

    xs = _unit_lower_inverse_minus_eye(Ls, C)
    sols = [r + _dot(x.astype(BF16), r.astype(BF16)) for x, r in zip(xs, rhss)]

    streams = [(b, h) for b in range(B) for h in range(LA_HEADS)]
    S = [state[b * LA_HEADS + h] for b, h in streams]
    for c in range(nC):
        ps = [index[(b, c, h)] for b, h in streams]
        r2 = [_dot(jnp.concatenate([sols[p][:, LA_DV:].astype(BF16), qgs[p]], axis=0),
                   S[n].astype(BF16)) for n, p in enumerate(ps)]
        vn = [(sols[p][:, :LA_DV] - r2[n][:C]).astype(BF16) for n, p in enumerate(ps)]
        S = [S[n] * egls[p] + _dot_tn(kds[p], vn[n]) for n, p in enumerate(ps)]
        rs = slice(c * C, (c + 1) * C)
        for n, ((b, h), p) in enumerate(zip(streams, ps)):
            o = r2[n][C:] + _dot(intras[p], vn[n])
            zg = _silu(z_ref[b, rs, h * LA_DV:(h + 1) * LA_DV])
            o_ref[b, rs, h * LA_DV:(h + 1) * LA_DV] = (
                _rms(o, nw_ref[...], NORM_EPS) * zg).astype(BF16)
    for n in range(len(streams)):
        state[n] = S[n]


def _gdn(la, avec, dtvec, nw, B, T, TB, C):
    nT = T // TB
    W = 2 * LA_QK + LA_V
    la3 = la.reshape(B, T, la.shape[1])
    kern = functools.partial(_gdn_kernel, TB=TB, C=C)
    out = pl.pallas_call(
        kern,
        grid=(nT,),
        in_specs=[
            pl.BlockSpec((B, TB, W), lambda t: (0, t, 0)),
            pl.BlockSpec((B, TB, LA_V), lambda t: (0, t, W // LA_V)),
            pl.BlockSpec((B, TB, BA_PAD), lambda t: (0, t, (W + LA_V) // BA_PAD)),
            pl.BlockSpec((1, BA_PAD), lambda t: (0, 0)),
            pl.BlockSpec((1, BA_PAD), lambda t: (0, 0)),
            pl.BlockSpec((1, LA_DV), lambda t: (0, 0)),
        ],
        out_specs=pl.BlockSpec((B, TB, LA_V), lambda t: (0, t, 0)),
        out_shape=jax.ShapeDtypeStruct((B, T, LA_V), BF16),
        scratch_shapes=[
            pltpu.VMEM((B * LA_HEADS, LA_DK, LA_DV), F32),
            pltpu.VMEM((B, TB, BA_PAD), F32),
            pltpu.VMEM((B, LANES, TB), F32),
            pltpu.VMEM((B, TB, BA_PAD), F32),
        ],
        compiler_params=pltpu.CompilerParams(
            dimension_semantics=("arbitrary",), vmem_limit_bytes=VMEM_LIMIT),
        name="gdn",
    )(la3, la3, la3, avec, dtvec, nw.reshape(1, LA_DV))
    return out.reshape(B * T, LA_V)


def _transpose_blocks(x):
    n = x.shape[0] // LANES
    return jnp.concatenate([x[j * LANES:(j + 1) * LANES, :].T for j in range(n)], axis=1)


def _attn_kernel(lam_ref, tab_ref, q_ref, k_ref, v_ref, nw_ref, o_ref, bias_s, vt_s, *,
                 tq, lam_init):
    i = pl.program_id(2)
    n_blocks = vt_s.shape[1]
    heads = range(HEADS_PER_STEP)

    def cols(hh):
        return slice(hh * LANES, (hh + 1) * LANES)

    @pl.when(i == 0)
    def _():
        for hh in heads:
            for e in range(2):
                x = jnp.broadcast_to(tab_ref[hh, e], (tq, 2 * tq))
                bias_s[hh, e] = pltpu.roll(x, 0, 1, stride=1, stride_axis=0)[:, :tq]
            for j in range(n_blocks):
                vt_s[hh, j] = _transpose_blocks(
                    v_ref[0, j * tq:(j + 1) * tq, cols(hh)].astype(F32)).astype(BF16)

    q2t = []
    for hh in heads:
        qt = _transpose_blocks(q_ref[0, :, cols(hh)].astype(F32))
        sub = lax.broadcasted_iota(jnp.int32, qt.shape, 0)
        q2t.append(jnp.concatenate([jnp.where(sub < DIFF_DH, qt, 0.0),
                                    jnp.where(sub >= DIFF_DH, qt, 0.0)], axis=1).astype(BF16))

    def scores(hh, j):
        kj = k_ref[0, pl.ds(pl.multiple_of(j * tq, tq), tq), cols(hh)]
        return _dot(kj, q2t[hh])

    def biased(s, bias):
        return jnp.concatenate([s[:, :tq] + bias, s[:, tq:] + bias], axis=1)

    def update(hh, carry, s, j):
        m, l, acc = carry
        m_new = jnp.maximum(m, jnp.max(s, axis=0, keepdims=True))
        alpha = jnp.exp2(m - m_new)
        p = jnp.exp2(s - m_new)
        l = alpha * l + jnp.sum(p, axis=0, keepdims=True)
        acc = alpha * acc + _dot(vt_s[hh, j], p.astype(BF16))
        return m_new, l, acc

    def group_update(carries, blocks):
        ms = [c[0] for c in carries]
        ls = [c[1] for c in carries]
        accs = [c[2] for c in carries]
        m_news = list(ms)
        for u in range(len(blocks[0])):
            for hh in heads:
                j, thunk, off = blocks[hh][u]
                s = thunk()
                ref = ms[hh] if off is None else ms[hh] + off
                p = jnp.exp2(s - ref)
                ls[hh] = ls[hh] + jnp.sum(p, axis=0, keepdims=True)
                accs[hh] = accs[hh] + _dot(vt_s[hh, j], p.astype(BF16))
                smax = jnp.max(s, axis=0, keepdims=True)
                m_news[hh] = jnp.maximum(m_news[hh], smax if off is None else smax - off)
        fast = []
        lead = None
        for hh in heads:
            alpha = jnp.exp2(ms[hh] - m_news[hh])
            fast.append((m_news[hh], alpha * ls[hh], alpha * accs[hh]))
            lead_h = jnp.max(m_news[hh] - ms[hh])
            lead = lead_h if lead is None else jnp.maximum(lead, lead_h)

        def redo():
            out = []
            for hh in heads:
                c = carries[hh]
                for j, thunk, off in blocks[hh]:
                    s = thunk()
                    c = update(hh, c, s if off is None else s - off, j)
                out.append(c)
            return tuple(out)

        return lax.cond(lead <= MAX_LEAD, lambda: tuple(fast), redo)

    j_near = jnp.maximum(i - 1, 0)
    edge = LANES
    assert edge >= MAX_DISTANCE and tq >= 2 * edge

    def near_scores(hh):
        s = scores(hh, j_near)
        corner = bias_s[hh, 1, tq - edge:tq, 0:edge]
        bot = s[tq - edge:]
        bot = jnp.concatenate([bot[:, :edge] + corner, bot[:, edge:tq],
                               bot[:, tq:tq + edge] + corner, bot[:, tq + edge:]], axis=1)
        return jnp.concatenate([s[:tq - edge], bot], axis=0)

    no_near = jnp.where(i == 0, -NEG_BIG, 0.0)
    carries, tail = [], []
    for hh in heads:
        s_diag = biased(scores(hh, i), bias_s[hh, 0])
        m0 = jnp.max(s_diag[:DIFF_DH], axis=0, keepdims=True)
        carries.append((m0, jnp.zeros((1, 2 * tq), F32), jnp.zeros((DIFF_DV, 2 * tq), F32)))
        tail.append([(i, functools.partial(lambda s: s, s_diag), None),
                     (j_near, functools.partial(near_scores, hh), no_near)])
    carries = group_update(tuple(carries), tail)
    n_far = jnp.maximum(i - 1, 0)
    done = 0
    for width in FAR_UNROLLS:
        def body(g, c, width=width, done=done):
            js = [done + width * g + u for u in range(width)]
            return group_update(c, [[(j, functools.partial(scores, hh, j), None) for j in js]
                                    for hh in heads])
        n_groups = (n_far - done) // width
        carries = lax.fori_loop(0, n_groups, body, carries)
        done = done + n_groups * width

    lam = (jnp.exp(jnp.sum(lam_ref[0:1, :] * lam_ref[1:2, :], axis=-1, keepdims=True))
           - jnp.exp(jnp.sum(lam_ref[2:3, :] * lam_ref[3:4, :], axis=-1, keepdims=True)) + lam_init)
    for hh in heads:
        m, l, acc = carries[hh]
        ot = acc[:, :tq] / l[:, :tq] - lam * (acc[:, tq:] / l[:, tq:])
        o = jnp.concatenate([ot[:, j * LANES:(j + 1) * LANES].T for j in range(tq // LANES)],
                            axis=0)
        o_ref[0, :, cols(hh)] = (_rms(o, nw_ref[...], DIFF_NORM_EPS)
                                 * (1.0 - lam_init)).astype(BF16)


def _attn(dd, lams, bias_tab, nw, B, T, tq, lam_init):
    assert T % tq == 0 and DIFF_HEADS % HEADS_PER_STEP == 0
    nq = T // tq
    hw = HEADS_PER_STEP * LANES
    dd3 = dd.reshape(B, T, dd.shape[1])
    kern = functools.partial(_attn_kernel, tq=tq, lam_init=lam_init)
    out = pl.pallas_call(
        kern,
        grid=(B, DIFF_HEADS // HEADS_PER_STEP, nq),
        in_specs=[
            pl.BlockSpec((4, DIFF_DH), lambda b, g, i: (0, 0)),
            pl.BlockSpec((HEADS_PER_STEP, 2, 1, 2 * tq), lambda b, g, i: (g, 0, 0, 0)),
            pl.BlockSpec((1, tq, hw), lambda b, g, i: (b, i, g)),
            pl.BlockSpec((1, T, hw), lambda b, g, i: (b, 0, DIFF_QK // hw + g)),
            pl.BlockSpec((1, T, hw), lambda b, g, i: (b, 0, 2 * DIFF_QK // hw + g)),
            pl.BlockSpec((1, DIFF_DV), lambda b, g, i: (0, 0)),
        ],
        out_specs=pl.BlockSpec((1, tq, hw), lambda b, g, i: (b, i, g)),
        out_shape=jax.ShapeDtypeStruct((B, T, DIFF_V), BF16),
        scratch_shapes=[
            pltpu.VMEM((HEADS_PER_STEP, 2, tq, tq), F32),
            pltpu.VMEM((HEADS_PER_STEP, nq, DIFF_DV, tq), BF16),
        ],
        compiler_params=pltpu.CompilerParams(
            dimension_semantics=("parallel", "parallel", "arbitrary"),
            vmem_limit_bytes=VMEM_LIMIT),
        name="diff_attn",
    )(lams, bias_tab, dd3, dd3, dd3, nw.reshape(1, DIFF_DV))
    return out.reshape(B * T, DIFF_V)


def _t5_bucket(rel):
    n = jnp.maximum(rel, 0)
    max_exact = NUM_BUCKETS // 2
    nf = jnp.maximum(n, 1).astype(F32)
    large = max_exact + (jnp.log(nf / max_exact) / math.log(MAX_DISTANCE / max_exact)
                         * (NUM_BUCKETS - max_exact)).astype(jnp.int32)
    large = jnp.minimum(large, NUM_BUCKETS - 1)
    return jnp.where(n < max_exact, n, large)


def _bias_tables(rel_bias, tq):
    assert tq + 1 >= MAX_DISTANCE
    m = jnp.arange(2 * tq)
    far = rel_bias[NUM_BUCKETS - 1].astype(F32)
    tabs = []
    for d in (0, tq):
        rel = jnp.where(m < tq, d + m, d + m - 2 * tq)
        b = (rel_bias[_t5_bucket(rel)].astype(F32) - far) * math.log2(math.e)
        tabs.append(jnp.where((rel >= 0)[:, None], b, NEG_BIG).T)
    return jnp.stack(tabs, axis=1)[:, :, None, :]


def _ffn_kernel(x_ref, ola_ref, od_ref, wout_ref, nw_ref, wgu_ref, wd_ref, fnw_ref,
                out_ref, *, final_norm):
    d_ff = wd_ref.shape[0]
    y = (x_ref[...] + _dot(ola_ref[...], wout_ref[0:LA_V, :])
         + _dot(od_ref[...], wout_ref[LA_V:LA_V + DIFF_V, :]))
    h = _rms(y, nw_ref[...], NORM_EPS).astype(BF16)
    gate = _dot(h, wgu_ref[:, 0:d_ff])
    up = _dot(h, wgu_ref[:, d_ff:2 * d_ff])
    y = y + _dot((_silu(gate) * up).astype(BF16), wd_ref[...])
    if final_norm:
        y = _rms(y, fnw_ref[...], NORM_EPS)
    out_ref[...] = y


def _ffn(xf, o_la, o_d, wout, nw, wgu, wdn, fnw, tm, final_norm):
    M, D = xf.shape
    kern = functools.partial(_ffn_kernel, final_norm=final_norm)
    resident = dict(pipeline_mode=pl.Buffered(1))
    return pl.pallas_call(
        kern,
        grid=(M // tm,),
        in_specs=[
            pl.BlockSpec((tm, D), lambda i: (i, 0)),
            pl.BlockSpec((tm, LA_V), lambda i: (i, 0)),
            pl.BlockSpec((tm, DIFF_V), lambda i: (i, 0)),
            pl.BlockSpec(wout.shape, lambda i: (0, 0), **resident),
            pl.BlockSpec((1, D), lambda i: (0, 0)),
            pl.BlockSpec(wgu.shape, lambda i: (0, 0), **resident),
            pl.BlockSpec(wdn.shape, lambda i: (0, 0), **resident),
            pl.BlockSpec((1, D), lambda i: (0, 0)),
        ],
        out_specs=pl.BlockSpec((tm, D), lambda i: (i, 0)),
        out_shape=jax.ShapeDtypeStruct((M, D), F32),
        compiler_params=pltpu.CompilerParams(
            dimension_semantics=("parallel",), vmem_limit_bytes=VMEM_LIMIT),
        name="out_proj_ffn",
    )(xf, o_la, o_d, wout, nw.reshape(1, D), wgu, wdn, fnw.reshape(1, D))


def _pick(n, pref):
    return pref if n % pref == 0 else n


def kernel(x, attn_norm_w, w_in, conv_w, a_log, dt_bias, la_norm_w, lambda_q1, lambda_k1,
           lambda_q2, lambda_k2, diff_norm_w, rel_bias, w_out, ffn_norm_w, w_gate_up,
           w_down, final_norm_w):
    B, T, D = x.shape
    depth = w_in.shape[0]
    M = B * T
    d_ff = w_down.shape[1]
    tm_proj = _pick(M, 512)
    tm_ffn = _pick(M, 512)
    tb = _pick(T, 256)
    chunk = 64
    tq = _pick(T, 512)

    n_la = 2 * LA_QK + 2 * LA_V
    bias_tab = _bias_tables(rel_bias, tq)
    lane_pad = BA_PAD - 2 * LA_HEADS
    dscale = jnp.concatenate([jnp.full((DIFF_QK,), DIFF_DH ** -0.5 * math.log2(math.e), F32),
                              jnp.ones((DIFF_QK + DIFF_V,), F32)]).reshape(1, -1)

    xf = x.reshape(M, D)
    for l in range(depth):
        w = w_in[l]
        n_qkv = 2 * LA_QK + LA_V
        wqkv = jnp.stack([w[:, s * CONV_SLAB:(s + 1) * CONV_SLAB]
                          for s in range(n_qkv // CONV_SLAB)]).astype(BF16)
        wzba = jnp.concatenate([w[:, n_qkv:n_la + 2 * LA_HEADS],
                                jnp.zeros((D, lane_pad), w.dtype)], axis=1).astype(BF16)
        wd = w[:, n_la + 2 * LA_HEADS:].astype(BF16)
        la, dd = _in_proj(xf, attn_norm_w[l], wqkv, wzba, wd, dscale, conv_w[l], tm_proj, T)

        avec = jnp.concatenate([jnp.zeros((LA_HEADS,), F32), a_log[l].astype(F32),
                                jnp.zeros((lane_pad,), F32)]).reshape(1, BA_PAD)
        dtvec = jnp.concatenate([jnp.zeros((LA_HEADS,), F32), dt_bias[l].astype(F32),
                                 jnp.zeros((lane_pad,), F32)]).reshape(1, BA_PAD)
        o_la = _gdn(la, avec, dtvec, la_norm_w[l], B, T, tb, chunk)

        lam_init = 0.8 - 0.6 * math.exp(-0.3 * l)
        lams = jnp.stack([lambda_q1[l], lambda_k1[l], lambda_q2[l], lambda_k2[l]]).astype(F32)
        o_d = _attn(dd, lams, bias_tab, diff_norm_w[l], B, T, tq, lam_init)

        xf = _ffn(xf, o_la, o_d, w_out[l].astype(BF16), ffn_norm_w[l],
                  w_gate_up[l].astype(BF16), w_down[l].astype(BF16), final_norm_w,
                  tm_ffn, final_norm=(l == depth - 1))
    return xf.reshape(B, T, D)
```

```python
import functools
import math

import jax
import jax.numpy as jnp
from jax import lax
from jax.experimental import pallas as pl
from jax.experimental.pallas import tpu as pltpu

LA_HEADS = 4
LA_DK = 128
LA_DV = 128
LA_QK = LA_HEADS * LA_DK
LA_V = LA_HEADS * LA_DV
CONV_K = 4
DIFF_HEADS = 4
DIFF_DH = 64
DIFF_DV = 2 * DIFF_DH
DIFF_QK = DIFF_HEADS * 2 * DIFF_DH
DIFF_V = DIFF_HEADS * DIFF_DV
NUM_BUCKETS = 32
MAX_DISTANCE = 128
NORM_EPS = 1e-6
DIFF_NORM_EPS = 1e-5
L2_EPS = 1e-6

LANES = 128
SUBLANES = 8
BA_PAD = LANES
LA_WIDTH = 2 * LA_QK + 2 * LA_V + BA_PAD
NEG_BIG = -1e30
FAR_UNROLLS = (4, 2, 1)
HEADS_PER_STEP = 2
CONV_SLAB = 256
MAX_LEAD = 40.0

VMEM_LIMIT = 52 * 1024 * 1024

F32 = jnp.float32
BF16 = jnp.bfloat16
HIGHEST = lax.Precision.HIGHEST


def _dot(a, b, precision=None):
    return jnp.dot(a, b, preferred_element_type=F32, precision=precision)


def _dot_nt(a, b, precision=None):
    return lax.dot_general(a, b, (((1,), (1,)), ((), ())),
                           preferred_element_type=F32, precision=precision)


def _dot_tn(a, b, precision=None):
    return lax.dot_general(a, b, (((0,), (0,)), ((), ())),
                           preferred_element_type=F32, precision=precision)


def _rms(x, w, eps):
    return x * lax.rsqrt(jnp.mean(x * x, axis=-1, keepdims=True) + eps) * w


def _silu(x):
    h = 0.5 * x
    return h + h * jnp.tanh(h)


def _softplus(x):
    return jnp.maximum(x, 0.0) + jnp.log1p(jnp.exp(-jnp.abs(x)))


def _in_proj_kernel(x_ref, nw_ref, wqkv_ref, wzba_ref, wd_ref, dscale_ref, convw_ref,
                    la_ref, d_ref, xbuf, *, tiles_per_seq):
    i = pl.program_id(0)
    tm = x_ref.shape[0]
    W = 2 * LA_QK + LA_V

    n_slabs, _, ws = xbuf.shape

    @pl.when(i % tiles_per_seq == 0)
    def _():
        xbuf[:, 0:SUBLANES, :] = jnp.zeros((n_slabs, SUBLANES, ws), F32)

    h = _rms(x_ref[...], nw_ref[...], NORM_EPS).astype(BF16)
    for s in range(n_slabs):
        xbuf[s, SUBLANES:SUBLANES + tm, :] = _dot(h, wqkv_ref[s])
        c0 = s * ws
        conv = convw_ref[CONV_K - 1:CONV_K, c0:c0 + ws] * xbuf[s, SUBLANES:SUBLANES + tm, :]
        for k in range(CONV_K - 1):
            off = SUBLANES - (CONV_K - 1) + k
            conv = conv + convw_ref[k:k + 1, c0:c0 + ws] * xbuf[s, off:off + tm, :]
        xbuf[s, 0:SUBLANES, :] = xbuf[s, tm:tm + SUBLANES, :]
        y = _silu(conv)
        if c0 >= 2 * LA_QK:
            la_ref[:, c0:c0 + ws] = y
            continue
        scale = LA_DK ** -0.5 if c0 < LA_QK else 1.0
        for hd in range(ws // LA_DK):
            yh = y[:, hd * LA_DK:(hd + 1) * LA_DK]
            yn = yh * lax.rsqrt(jnp.sum(yh * yh, -1, keepdims=True) + L2_EPS)
            la_ref[:, c0 + hd * LA_DK:c0 + (hd + 1) * LA_DK] = yn * scale if scale != 1.0 else yn
    d_ref[...] = (_dot(h, wd_ref[...]) * dscale_ref[...]).astype(BF16)
    la_ref[:, W:] = _dot(h, wzba_ref[...])


def _in_proj(xf, nw, wqkv, wzba, wd, dscale, convw, tm, T):
    M, D = xf.shape
    assert T % tm == 0
    W = 2 * LA_QK + LA_V
    kern = functools.partial(_in_proj_kernel, tiles_per_seq=T // tm)
    return pl.pallas_call(
        kern,
        grid=(M // tm,),
        in_specs=[
            pl.BlockSpec((tm, D), lambda i: (i, 0)),
            pl.BlockSpec((1, D), lambda i: (0, 0)),
            pl.BlockSpec(wqkv.shape, lambda i: (0, 0, 0)),
            pl.BlockSpec(wzba.shape, lambda i: (0, 0)),
            pl.BlockSpec(wd.shape, lambda i: (0, 0)),
            pl.BlockSpec(dscale.shape, lambda i: (0, 0)),
            pl.BlockSpec(convw.shape, lambda i: (0, 0)),
        ],
        out_specs=[
            pl.BlockSpec((tm, LA_WIDTH), lambda i: (i, 0)),
            pl.BlockSpec((tm, wd.shape[1]), lambda i: (i, 0)),
        ],
        out_shape=[
            jax.ShapeDtypeStruct((M, LA_WIDTH), F32),
            jax.ShapeDtypeStruct((M, wd.shape[1]), BF16),
        ],
        scratch_shapes=[pltpu.VMEM((wqkv.shape[0], tm + SUBLANES, wqkv.shape[2]), F32)],
        compiler_params=pltpu.CompilerParams(
            dimension_semantics=("arbitrary",), vmem_limit_bytes=VMEM_LIMIT),
        name="in_proj",
    )(xf, nw.reshape(1, D), wqkv, wzba, wd, dscale, convw)


def _unit_lower_inverse_minus_eye(Ls, C):
    row = lax.broadcasted_iota(jnp.int32, (C, C), 0)
    col = lax.broadcasted_iota(jnp.int32, (C, C), 1)
    xs = None
    s = 1
    while s < C:
        same_pair = (row // (2 * s)) == (col // (2 * s))
        low_left = same_pair & ((row // s) % 2 == 1) & ((col // s) % 2 == 0)
        offs = [jnp.where(low_left, L, 0.0) for L in Ls]
        if xs is None:
            xs = [-m for m in offs]
        else:
            xb = [x.astype(BF16) for x in xs]
            ys = [m + _dot(m.astype(BF16), x) for m, x in zip(offs, xb)]
            xs = [x - y - _dot(x16, y.astype(BF16)) for x, x16, y in zip(xs, xb, ys)]
        s *= 2
    return xs


def _gdn_kernel(qkv_ref, z_ref, ba_ref, avec_ref, dtvec_ref, nw_ref,
                o_ref, state, g_s, gt_s, beta_s, *, TB, C):
    t = pl.program_id(0)
    B = qkv_ref.shape[0]
    nC = TB // C

    @pl.when(t == 0)
    def _():
        state[...] = jnp.zeros(state.shape, F32)

    row = lax.broadcasted_iota(jnp.int32, (C, C), 0)
    col = lax.broadcasted_iota(jnp.int32, (C, C), 1)
    causal = row >= col
    strict = row > col
    tril = causal.astype(F32)
    ba_lane = lax.broadcasted_iota(jnp.int32, (1, BA_PAD), 1)
    is_g = (ba_lane >= LA_HEADS) & (ba_lane < 2 * LA_HEADS)

    for b in range(B):
        ba = ba_ref[b]
        beta_s[b] = jax.nn.sigmoid(ba)
        g_raw = -jnp.where(is_g, jnp.exp(avec_ref[...]), 0.0) * _softplus(ba + dtvec_ref[...])
        for c in range(nC):
            g_s[b, c * C:(c + 1) * C, :] = _dot(tril, g_raw[c * C:(c + 1) * C, :], HIGHEST)
        for j in range(TB // LANES):
            gt_s[b, :, j * LANES:(j + 1) * LANES] = g_s[b, j * LANES:(j + 1) * LANES, :].T

    probs = [(b, c, h) for b in range(B) for c in range(nC) for h in range(LA_HEADS)]
    index = {p: n for n, p in enumerate(probs)}
    Ls, intras, rhss, qgs, kds, egls = [], [], [], [], [], []
    for b, c, h in probs:
        rs = slice(c * C, (c + 1) * C)
        qn = qkv_ref[b, rs, h * LA_DK:(h + 1) * LA_DK]
        kn = qkv_ref[b, rs, LA_QK + h * LA_DK:LA_QK + (h + 1) * LA_DK]
        v = qkv_ref[b, rs, 2 * LA_QK + h * LA_DV:2 * LA_QK + (h + 1) * LA_DV]
        gb = jnp.broadcast_to(g_s[b, rs, LA_HEADS + h:LA_HEADS + h + 1], (C, LANES))
        beta = jnp.broadcast_to(beta_s[b, rs, h:h + 1], (C, LANES))
        g_row = gt_s[b, LA_HEADS + h:LA_HEADS + h + 1, rs]
        gdiff = gb[:, :C] - g_row
        decay = jnp.where(causal, jnp.exp(jnp.where(causal, gdiff, 0.0)), 0.0)
        kb = kn * beta
        kq = _dot_nt(jnp.concatenate([kb, qn], axis=0).astype(BF16), kn.astype(BF16))
        Ls.append(jnp.where(strict, kq[:C] * decay, 0.0))
        intras.append((kq[C:] * decay).astype(BF16))
        eg = jnp.exp(gb)
        rhss.append(jnp.concatenate([v * beta, kb * eg], axis=1))
        qgs.append((qn * eg).astype(BF16))
        g_last = gb[C - 1:C, :]
        kds.append((kn * jnp.exp(g_last - gb)).astype(BF16))
        egls.append(jnp.exp(g_last))

    xs = _unit_lower_inverse_minus_eye(Ls, C)
    sols = [r + _dot(x.astype(BF16), r.astype(BF16)) for x, r in zip(xs, rhss)]

    streams = [(b, h) for b in range(B) for h in range(LA_HEADS)]
    S = [state[b * LA_HEADS + h] for b, h in streams]
    for c in range(nC):
        ps = [index[(b, c, h)] for b, h in streams]
        r2 = [_dot(jnp.concatenate([sols[p][:, LA_DV:].astype(BF16), qgs[p]], axis=0),
                   S[n].astype(BF16)) for n, p in enumerate(ps)]
        vn = [(sols[p][:, :LA_DV] - r2[n][:C]).astype(BF16) for n, p in enumerate(ps)]
        S = [S[n] * egls[p] + _dot_tn(kds[p], vn[n]) for n, p in enumerate(ps)]
        rs = slice(c * C, (c + 1) * C)
        for n, ((b, h), p) in enumerate(zip(streams, ps)):
            o = r2[n][C:] + _dot(intras[p], vn[n])
            zg = _silu(z_ref[b, rs, h * LA_DV:(h + 1) * LA_DV])
            o_ref[b, rs, h * LA_DV:(h + 1) * LA_DV] = (
                _rms(o, nw_ref[...], NORM_EPS) * zg).astype(BF16)
    for n in range(len(streams)):
        state[n] = S[n]


def _gdn(la, avec, dtvec, nw, B, T, TB, C):
    nT = T // TB
    W = 2 * LA_QK + LA_V
    la3 = la.reshape(B, T, la.shape[1])
    kern = functools.partial(_gdn_kernel, TB=TB, C=C)
    out = pl.pallas_call(
        kern,
        grid=(nT,),
        in_specs=[
            pl.BlockSpec((B, TB, W), lambda t: (0, t, 0)),
            pl.BlockSpec((B, TB, LA_V), lambda t: (0, t, W // LA_V)),
            pl.BlockSpec((B, TB, BA_PAD), lambda t: (0, t, (W + LA_V) // BA_PAD)),
            pl.BlockSpec((1, BA_PAD), lambda t: (0, 0)),
            pl.BlockSpec((1, BA_PAD), lambda t: (0, 0)),
            pl.BlockSpec((1, LA_DV), lambda t: (0, 0)),
        ],
        out_specs=pl.BlockSpec((B, TB, LA_V), lambda t: (0, t, 0)),
        out_shape=jax.ShapeDtypeStruct((B, T, LA_V), BF16),
        scratch_shapes=[
            pltpu.VMEM((B * LA_HEADS, LA_DK, LA_DV), F32),
            pltpu.VMEM((B, TB, BA_PAD), F32),
            pltpu.VMEM((B, LANES, TB), F32),
            pltpu.VMEM((B, TB, BA_PAD), F32),
        ],
        compiler_params=pltpu.CompilerParams(
            dimension_semantics=("arbitrary",), vmem_limit_bytes=VMEM_LIMIT),
        name="gdn",
    )(la3, la3, la3, avec, dtvec, nw.reshape(1, LA_DV))
    return out.reshape(B * T, LA_V)


def _transpose_blocks(x):
    n = x.shape[0] // LANES
    return jnp.concatenate([x[j * LANES:(j + 1) * LANES, :].T for j in range(n)], axis=1)


def _attn_kernel(lam_ref, tab_ref, q_ref, k_ref, v_ref, nw_ref, o_ref, bias_s, vt_s, *,
                 tq, lam_init):
    i = pl.program_id(2)
    n_blocks = vt_s.shape[1]
    heads = range(HEADS_PER_STEP)

    def cols(hh):
        return slice(hh * LANES, (hh + 1) * LANES)

    @pl.when(i == 0)
    def _():
        for hh in heads:
            for e in range(2):
                x = jnp.broadcast_to(tab_ref[hh, e], (tq, 2 * tq))
                bias_s[hh, e] = pltpu.roll(x, 0, 1, stride=1, stride_axis=0)[:, :tq]
            for j in range(n_blocks):
                vt_s[hh, j] = _transpose_blocks(
                    v_ref[0, j * tq:(j + 1) * tq, cols(hh)].astype(F32)).astype(BF16)

    q2t = []
    for hh in heads:
        qt = _transpose_blocks(q_ref[0, :, cols(hh)].astype(F32))
        sub = lax.broadcasted_iota(jnp.int32, qt.shape, 0)
        q2t.append(jnp.concatenate([jnp.where(sub < DIFF_DH, qt, 0.0),
                                    jnp.where(sub >= DIFF_DH, qt, 0.0)], axis=1).astype(BF16))

    def scores(hh, j):
        kj = k_ref[0, pl.ds(pl.multiple_of(j * tq, tq), tq), cols(hh)]
        return _dot(kj, q2t[hh])

    def biased(s, bias):
        return jnp.concatenate([s[:, :tq] + bias, s[:, tq:] + bias], axis=1)

    def update(hh, carry, s, j):
        m, l, acc = carry
        m_new = jnp.maximum(m, jnp.max(s, axis=0, keepdims=True))
        alpha = jnp.exp2(m - m_new)
        p = jnp.exp2(s - m_new)
        l = alpha * l + jnp.sum(p, axis=0, keepdims=True)
        acc = alpha * acc + _dot(vt_s[hh, j], p.astype(BF16))
        return m_new, l, acc

    def group_update(carries, blocks):
        ms = [c[0] for c in carries]
        ls = [c[1] for c in carries]
        accs = [c[2] for c in carries]
        m_news = list(ms)
        for u in range(len(blocks[0])):
            for hh in heads:
                j, thunk, off = blocks[hh][u]
                s = thunk()
                ref = ms[hh] if off is None else ms[hh] + off
                p = jnp.exp2(s - ref)
                ls[hh] = ls[hh] + jnp.sum(p, axis=0, keepdims=True)
                accs[hh] = accs[hh] + _dot(vt_s[hh, j], p.astype(BF16))
                smax = jnp.max(s, axis=0, keepdims=True)
                m_news[hh] = jnp.maximum(m_news[hh], smax if off is None else smax - off)
        fast = []
        lead = None
        for hh in heads:
            alpha = jnp.exp2(ms[hh] - m_news[hh])
            fast.append((m_news[hh], alpha * ls[hh], alpha * accs[hh]))
            lead_h = jnp.max(m_news[hh] - ms[hh])
            lead = lead_h if lead is None else jnp.maximum(lead, lead_h)

        def redo():
            out = []
            for hh in heads:
                c = carries[hh]
                for j, thunk, off in blocks[hh]:
                    s = thunk()
                    c = update(hh, c, s if off is None else s - off, j)
                out.append(c)
            return tuple(out)

        return lax.cond(lead <= MAX_LEAD, lambda: tuple(fast), redo)

    j_near = jnp.maximum(i - 1, 0)
    edge = LANES
    assert edge >= MAX_DISTANCE and tq >= 2 * edge

    def near_scores(hh):
        s = scores(hh, j_near)
        corner = bias_s[hh, 1, tq - edge:tq, 0:edge]
        bot = s[tq - edge:]
        bot = jnp.concatenate([bot[:, :edge] + corner, bot[:, edge:tq],
                               bot[:, tq:tq + edge] + corner, bot[:, tq + edge:]], axis=1)
        return jnp.concatenate([s[:tq - edge], bot], axis=0)

    no_near = jnp.where(i == 0, -NEG_BIG, 0.0)
    carries, tail = [], []
    for hh in heads:
        s_diag = biased(scores(hh, i), bias_s[hh, 0])
        m0 = jnp.max(s_diag[:DIFF_DH], axis=0, keepdims=True)
        carries.append((m0, jnp.zeros((1, 2 * tq), F32), jnp.zeros((DIFF_DV, 2 * tq), F32)))
        tail.append([(i, functools.partial(lambda s: s, s_diag), None),
                     (j_near, functools.partial(near_scores, hh), no_near)])
    carries = group_update(tuple(carries), tail)
    n_far = jnp.maximum(i - 1, 0)
    done = 0
    for width in FAR_UNROLLS:
        def body(g, c, width=width, done=done):
            js = [done + width * g + u for u in range(width)]
            return group_update(c, [[(j, functools.partial(scores, hh, j), None) for j in js]
                                    for hh in heads])
        n_groups = (n_far - done) // width
        carries = lax.fori_loop(0, n_groups, body, carries)
        done = done + n_groups * width

    lam = (jnp.exp(jnp.sum(lam_ref[0:1, :] * lam_ref[1:2, :], axis=-1, keepdims=True))
           - jnp.exp(jnp.sum(lam_ref[2:3, :] * lam_ref[3:4, :], axis=-1, keepdims=True)) + lam_init)
    for hh in heads:
        m, l, acc = carries[hh]
        inv_l = 1.0 / l
        ot = acc[:, :tq] * inv_l[:, :tq] - lam * (acc[:, tq:] * inv_l[:, tq:])
        o = jnp.concatenate([ot[:, j * LANES:(j + 1) * LANES].T for j in range(tq // LANES)],
                            axis=0)
        o_ref[0, :, cols(hh)] = (_rms(o, nw_ref[...], DIFF_NORM_EPS)
                                 * (1.0 - lam_init)).astype(BF16)


def _attn(dd, lams, bias_tab, nw, B, T, tq, lam_init):
    assert T % tq == 0 and DIFF_HEADS % HEADS_PER_STEP == 0
    nq = T // tq
    hw = HEADS_PER_STEP * LANES
    dd3 = dd.reshape(B, T, dd.shape[1])
    kern = functools.partial(_attn_kernel, tq=tq, lam_init=lam_init)
    out = pl.pallas_call(
        kern,
        grid=(B, DIFF_HEADS // HEADS_PER_STEP, nq),
        in_specs=[
            pl.BlockSpec((4, DIFF_DH), lambda b, g, i: (0, 0)),
            pl.BlockSpec((HEADS_PER_STEP, 2, 1, 2 * tq), lambda b, g, i: (g, 0, 0, 0)),
            pl.BlockSpec((1, tq, hw), lambda b, g, i: (b, i, g)),
            pl.BlockSpec((1, T, hw), lambda b, g, i: (b, 0, DIFF_QK // hw + g)),
            pl.BlockSpec((1, T, hw), lambda b, g, i: (b, 0, 2 * DIFF_QK // hw + g)),
            pl.BlockSpec((1, DIFF_DV), lambda b, g, i: (0, 0)),
        ],
        out_specs=pl.BlockSpec((1, tq, hw), lambda b, g, i: (b, i, g)),
        out_shape=jax.ShapeDtypeStruct((B, T, DIFF_V), BF16),
        scratch_shapes=[
            pltpu.VMEM((HEADS_PER_STEP, 2, tq, tq), F32),
            pltpu.VMEM((HEADS_PER_STEP, nq, DIFF_DV, tq), BF16),
        ],
        compiler_params=pltpu.CompilerParams(
            dimension_semantics=("parallel", "parallel", "arbitrary"),
            vmem_limit_bytes=VMEM_LIMIT),
        name="diff_attn",
    )(lams, bias_tab, dd3, dd3, dd3, nw.reshape(1, DIFF_DV))
    return out.reshape(B * T, DIFF_V)


def _t5_bucket(rel):
    n = jnp.maximum(rel, 0)
    max_exact = NUM_BUCKETS // 2
    nf = jnp.maximum(n, 1).astype(F32)
    large = max_exact + (jnp.log(nf / max_exact) / math.log(MAX_DISTANCE / max_exact)
                         * (NUM_BUCKETS - max_exact)).astype(jnp.int32)
    large = jnp.minimum(large, NUM_BUCKETS - 1)
    return jnp.where(n < max_exact, n, large)


def _bias_tables(rel_bias, tq):
    assert tq + 1 >= MAX_DISTANCE
    m = jnp.arange(2 * tq)
    far = rel_bias[NUM_BUCKETS - 1].astype(F32)
    tabs = []
    for d in (0, tq):
        rel = jnp.where(m < tq, d + m, d + m - 2 * tq)
        b = (rel_bias[_t5_bucket(rel)].astype(F32) - far) * math.log2(math.e)
        tabs.append(jnp.where((rel >= 0)[:, None], b, NEG_BIG).T)
    return jnp.stack(tabs, axis=1)[:, :, None, :]


def _ffn_kernel(x_ref, ola_ref, od_ref, wout_ref, nw_ref, wgu_ref, wd_ref, fnw_ref,
                out_ref, *, final_norm):
    d_ff = wd_ref.shape[0]
    y = (x_ref[...] + _dot(ola_ref[...], wout_ref[0:LA_V, :])
         + _dot(od_ref[...], wout_ref[LA_V:LA_V + DIFF_V, :]))
    h = _rms(y, nw_ref[...], NORM_EPS).astype(BF16)
    gate = _dot(h, wgu_ref[:, 0:d_ff])
    up = _dot(h, wgu_ref[:, d_ff:2 * d_ff])
    y = y + _dot((_silu(gate) * up).astype(BF16), wd_ref[...])
    if final_norm:
        y = _rms(y, fnw_ref[...], NORM_EPS)
    out_ref[...] = y


def _ffn(xf, o_la, o_d, wout, nw, wgu, wdn, fnw, tm, final_norm):
    M, D = xf.shape
    kern = functools.partial(_ffn_kernel, final_norm=final_norm)
    resident = dict(pipeline_mode=pl.Buffered(1))
    return pl.pallas_call(
        kern,
        grid=(M // tm,),
        in_specs=[
            pl.BlockSpec((tm, D), lambda i: (i, 0)),
            pl.BlockSpec((tm, LA_V), lambda i: (i, 0)),
            pl.BlockSpec((tm, DIFF_V), lambda i: (i, 0)),
            pl.BlockSpec(wout.shape, lambda i: (0, 0), **resident),
            pl.BlockSpec((1, D), lambda i: (0, 0)),
            pl.BlockSpec(wgu.shape, lambda i: (0, 0), **resident),
            pl.BlockSpec(wdn.shape, lambda i: (0, 0), **resident),
            pl.BlockSpec((1, D), lambda i: (0, 0)),
        ],
        out_specs=pl.BlockSpec((tm, D), lambda i: (i, 0)),
        out_shape=jax.ShapeDtypeStruct((M, D), F32),
        compiler_params=pltpu.CompilerParams(
            dimension_semantics=("parallel",), vmem_limit_bytes=VMEM_LIMIT),
        name="out_proj_ffn",
    )(xf, o_la, o_d, wout, nw.reshape(1, D), wgu, wdn, fnw.reshape(1, D))


def _pick(n, pref):
    return pref if n % pref == 0 else n


def kernel(x, attn_norm_w, w_in, conv_w, a_log, dt_bias, la_norm_w, lambda_q1, lambda_k1,
           lambda_q2, lambda_k2, diff_norm_w, rel_bias, w_out, ffn_norm_w, w_gate_up,
           w_down, final_norm_w):
    B, T, D = x.shape
    depth = w_in.shape[0]
    M = B * T
    d_ff = w_down.shape[1]
    tm_proj = _pick(M, 512)
    tm_ffn = _pick(M, 512)
    tb = _pick(T, 256)
    chunk = 64
    tq = _pick(T, 512)

    n_la = 2 * LA_QK + 2 * LA_V
    bias_tab = _bias_tables(rel_bias, tq)
    lane_pad = BA_PAD - 2 * LA_HEADS
    dscale = jnp.concatenate([jnp.full((DIFF_QK,), DIFF_DH ** -0.5 * math.log2(math.e), F32),
                              jnp.ones((DIFF_QK + DIFF_V,), F32)]).reshape(1, -1)

    w_in_bf = lax.optimization_barrier(w_in.astype(BF16))
    n_qkv = 2 * LA_QK + LA_V

    xf = x.reshape(M, D)
    for l in range(depth):
        w = w_in_bf[l]
        wqkv = jnp.stack([w[:, s * CONV_SLAB:(s + 1) * CONV_SLAB]
                          for s in range(n_qkv // CONV_SLAB)])
        wzba = jnp.concatenate([w[:, n_qkv:n_la + 2 * LA_HEADS],
                                jnp.zeros((D, lane_pad), w.dtype)], axis=1)
        wd = w[:, n_la + 2 * LA_HEADS:]
        la, dd = _in_proj(xf, attn_norm_w[l], wqkv, wzba, wd, dscale, conv_w[l], tm_proj, T)

        avec = jnp.concatenate([jnp.zeros((LA_HEADS,), F32), a_log[l].astype(F32),
                                jnp.zeros((lane_pad,), F32)]).reshape(1, BA_PAD)
        dtvec = jnp.concatenate([jnp.zeros((LA_HEADS,), F32), dt_bias[l].astype(F32),
                                 jnp.zeros((lane_pad,), F32)]).reshape(1, BA_PAD)
        o_la = _gdn(la, avec, dtvec, la_norm_w[l], B, T, tb, chunk)

        lam_init = 0.8 - 0.6 * math.exp(-0.3 * l)
        lams = jnp.stack([lambda_q1[l], lambda_k1[l], lambda_q2[l], lambda_k2[l]]).astype(F32)
        o_d = _attn(dd, lams, bias_tab, diff_norm_w[l], B, T, tq, lam_init)

        xf = _ffn(xf, o_la, o_d, w_out[l].astype(BF16), ffn_norm_w[l],
                  w_gate_up[l].astype(BF16), w_down[l].astype(BF16), final_norm_w,
                  tm_ffn, final_norm=(l == depth - 1))
    return xf.reshape(B, T, D)
```

```python
import functools
import math

import jax
import jax.numpy as jnp
from jax import lax
from jax.experimental import pallas as pl
from jax.experimental.pallas import tpu as pltpu

LA_HEADS = 4
LA_DK = 128
LA_DV = 128
LA_QK = LA_HEADS * LA_DK
LA_V = LA_HEADS * LA_DV
CONV_K = 4
DIFF_HEADS = 4
DIFF_DH = 64
DIFF_DV = 2 * DIFF_DH
DIFF_QK = DIFF_HEADS * 2 * DIFF_DH
DIFF_V = DIFF_HEADS * DIFF_DV
NUM_BUCKETS = 32
MAX_DISTANCE = 128
NORM_EPS = 1e-6
DIFF_NORM_EPS = 1e-5
L2_EPS = 1e-6

LANES = 128
SUBLANES = 8
BA_PAD = LANES
LA_WIDTH = 2 * LA_QK + 2 * LA_V + BA_PAD
NEG_BIG = -1e30
FAR_UNROLLS = (4, 2, 1)
HEADS_PER_STEP = 2
CONV_SLAB = 256
MAX_LEAD = 40.0

VMEM_LIMIT = 52 * 1024 * 1024

F32 = jnp.float32
BF16 = jnp.bfloat16
HIGHEST = lax.Precision.HIGHEST


def _dot(a, b, precision=None):
    return jnp.dot(a, b, preferred_element_type=F32, precision=precision)


def _dot_nt(a, b, precision=None):
    return lax.dot_general(a, b, (((1,), (1,)), ((), ())),
                           preferred_element_type=F32, precision=precision)


def _dot_tn(a, b, precision=None):
    return lax.dot_general(a, b, (((0,), (0,)), ((), ())),
                           preferred_element_type=F32, precision=precision)


def _rms(x, w, eps):
    return x * lax.rsqrt(jnp.mean(x * x, axis=-1, keepdims=True) + eps) * w


def _silu(x):
    h = 0.5 * x
    return h + h * jnp.tanh(h)


def _softplus(x):
    return jnp.maximum(x, 0.0) + jnp.log1p(jnp.exp(-jnp.abs(x)))


def _in_proj_kernel(x_ref, nw_ref, wqkv_ref, wzba_ref, wd_ref, dscale_ref, convw_ref,
                    la_ref, d_ref, xbuf, *, tiles_per_seq):
    i = pl.program_id(0)
    tm = x_ref.shape[0]
    W = 2 * LA_QK + LA_V

    n_slabs, _, ws = xbuf.shape

    @pl.when(i % tiles_per_seq == 0)
    def _():
        xbuf[:, 0:SUBLANES, :] = jnp.zeros((n_slabs, SUBLANES, ws), F32)

    h = _rms(x_ref[...], nw_ref[...], NORM_EPS).astype(BF16)
    for s in range(n_slabs):
        xbuf[s, SUBLANES:SUBLANES + tm, :] = _dot(h, wqkv_ref[s])
        c0 = s * ws
        conv = convw_ref[CONV_K - 1:CONV_K, c0:c0 + ws] * xbuf[s, SUBLANES:SUBLANES + tm, :]
        for k in range(CONV_K - 1):
            off = SUBLANES - (CONV_K - 1) + k
            conv = conv + convw_ref[k:k + 1, c0:c0 + ws] * xbuf[s, off:off + tm, :]
        xbuf[s, 0:SUBLANES, :] = xbuf[s, tm:tm + SUBLANES, :]
        y = _silu(conv)
        if c0 >= 2 * LA_QK:
            la_ref[:, c0:c0 + ws] = y
            continue
        scale = LA_DK ** -0.5 if c0 < LA_QK else 1.0
        for hd in range(ws // LA_DK):
            yh = y[:, hd * LA_DK:(hd + 1) * LA_DK]
            yn = yh * lax.rsqrt(jnp.sum(yh * yh, -1, keepdims=True) + L2_EPS)
            la_ref[:, c0 + hd * LA_DK:c0 + (hd + 1) * LA_DK] = yn * scale if scale != 1.0 else yn
    d_ref[...] = (_dot(h, wd_ref[...]) * dscale_ref[...]).astype(BF16)
    la_ref[:, W:] = _dot(h, wzba_ref[...])


def _in_proj(xf, nw, wqkv, wzba, wd, dscale, convw, tm, T):
    M, D = xf.shape
    assert T % tm == 0
    W = 2 * LA_QK + LA_V
    kern = functools.partial(_in_proj_kernel, tiles_per_seq=T // tm)
    return pl.pallas_call(
        kern,
        grid=(M // tm,),
        in_specs=[
            pl.BlockSpec((tm, D), lambda i: (i, 0)),
            pl.BlockSpec((1, D), lambda i: (0, 0)),
            pl.BlockSpec(wqkv.shape, lambda i: (0, 0, 0)),
            pl.BlockSpec(wzba.shape, lambda i: (0, 0)),
            pl.BlockSpec(wd.shape, lambda i: (0, 0)),
            pl.BlockSpec(dscale.shape, lambda i: (0, 0)),
            pl.BlockSpec(convw.shape, lambda i: (0, 0)),
        ],
        out_specs=[
            pl.BlockSpec((tm, LA_WIDTH), lambda i: (i, 0)),
            pl.BlockSpec((tm, wd.shape[1]), lambda i: (i, 0)),
        ],
        out_shape=[
            jax.ShapeDtypeStruct((M, LA_WIDTH), F32),
            jax.ShapeDtypeStruct((M, wd.shape[1]), BF16),
        ],
        scratch_shapes=[pltpu.VMEM((wqkv.shape[0], tm + SUBLANES, wqkv.shape[2]), F32)],
        compiler_params=pltpu.CompilerParams(
            dimension_semantics=("arbitrary",), vmem_limit_bytes=VMEM_LIMIT),
        name="in_proj",
    )(xf, nw.reshape(1, D), wqkv, wzba, wd, dscale, convw)


def _unit_lower_inverse_minus_eye(Ls, C):
    row = lax.broadcasted_iota(jnp.int32, (C, C), 0)
    col = lax.broadcasted_iota(jnp.int32, (C, C), 1)
    xs = None
    s = 1
    while s < C:
        same_pair = (row // (2 * s)) == (col // (2 * s))
        low_left = same_pair & ((row // s) % 2 == 1) & ((col // s) % 2 == 0)
        offs = [jnp.where(low_left, L, 0.0) for L in Ls]
        if xs is None:
            xs = [-m for m in offs]
        else:
            xb = [x.astype(BF16) for x in xs]
            ys = [m + _dot(m.astype(BF16), x) for m, x in zip(offs, xb)]
            xs = [x - y - _dot(x16, y.astype(BF16)) for x, x16, y in zip(xs, xb, ys)]
        s *= 2
    return xs


def _gdn_kernel(qkv_ref, z_ref, ba_ref, avec_ref, dtvec_ref, nw_ref,
                o_ref, state, g_s, gt_s, beta_s, *, TB, C):
    t = pl.program_id(0)
    B = qkv_ref.shape[0]
    nC = TB // C

    @pl.when(t == 0)
    def _():
        state[...] = jnp.zeros(state.shape, F32)

    row = lax.broadcasted_iota(jnp.int32, (C, C), 0)
    col = lax.broadcasted_iota(jnp.int32, (C, C), 1)
    causal = row >= col
    strict = row > col
    tril = causal.astype(F32)
    ba_lane = lax.broadcasted_iota(jnp.int32, (1, BA_PAD), 1)
    is_g = (ba_lane >= LA_HEADS) & (ba_lane < 2 * LA_HEADS)

    for b in range(B):
        ba = ba_ref[b]
        beta_s[b] = jax.nn.sigmoid(ba)
        g_raw = -jnp.where(is_g, jnp.exp(avec_ref[...]), 0.0) * _softplus(ba + dtvec_ref[...])
        for c in range(nC):
            g_s[b, c * C:(c + 1) * C, :] = _dot(tril, g_raw[c * C:(c + 1) * C, :], HIGHEST)
        for j in range(TB // LANES):
            gt_s[b, :, j * LANES:(j + 1) * LANES] = g_s[b, j * LANES:(j + 1) * LANES, :].T

    probs = [(b, c, h) for b in range(B) for c in range(nC) for h in range(LA_HEADS)]
    index = {p: n for n, p in enumerate(probs)}
    Ls, intras, rhss, qgs, kds, egls = [], [], [], [], [], []
    for b, c, h in probs:
        rs = slice(c * C, (c + 1) * C)
        qn = qkv_ref[b, rs, h * LA_DK:(h + 1) * LA_DK]
        kn = qkv_ref[b, rs, LA_QK + h * LA_DK:LA_QK + (h + 1) * LA_DK]
        v = qkv_ref[b, rs, 2 * LA_QK + h * LA_DV:2 * LA_QK + (h + 1) * LA_DV]
        gb = jnp.broadcast_to(g_s[b, rs, LA_HEADS + h:LA_HEADS + h + 1], (C, LANES))
        beta = jnp.broadcast_to(beta_s[b, rs, h:h + 1], (C, LANES))
        g_row = gt_s[b, LA_HEADS + h:LA_HEADS + h + 1, rs]
        gdiff = gb[:, :C] - g_row
        decay = jnp.where(causal, jnp.exp(jnp.where(causal, gdiff, 0.0)), 0.0)
        kb = kn * beta
        kq = _dot_nt(jnp.concatenate([kb, qn], axis=0).astype(BF16), kn.astype(BF16))
        Ls.append(jnp.where(strict, kq[:C] * decay, 0.0))
        intras.append((kq[C:] * decay).astype(BF16))
        eg = jnp.exp(gb)
        rhss.append(jnp.concatenate([v * beta, kb * eg], axis=1))
        qgs.append((qn * eg).astype(BF16))
        g_last = gb[C - 1:C, :]
        kds.append((kn * jnp.exp(g_last - gb)).astype(BF16))
        egls.append(jnp.exp(g_last))

    xs = _unit_lower_inverse_minus_eye(Ls, C)
    sols = [r + _dot(x.astype(BF16), r.astype(BF16)) for x, r in zip(xs, rhss)]

    streams = [(b, h) for b in range(B) for h in range(LA_HEADS)]
    S = [state[b * LA_HEADS + h] for b, h in streams]
    for c in range(nC):
        ps = [index[(b, c, h)] for b, h in streams]
        r2 = [_dot(jnp.concatenate([sols[p][:, LA_DV:].astype(BF16), qgs[p]], axis=0),
                   S[n].astype(BF16)) for n, p in enumerate(ps)]
        vn = [(sols[p][:, :LA_DV] - r2[n][:C]).astype(BF16) for n, p in enumerate(ps)]
        S = [S[n] * egls[p] + _dot_tn(kds[p], vn[n]) for n, p in enumerate(ps)]
        rs = slice(c * C, (c + 1) * C)
        for n, ((b, h), p) in enumerate(zip(streams, ps)):
            o = r2[n][C:] + _dot(intras[p], vn[n])
            zg = _silu(z_ref[b, rs, h * LA_DV:(h + 1) * LA_DV])
            o_ref[b, rs, h * LA_DV:(h + 1) * LA_DV] = (
                _rms(o, nw_ref[...], NORM_EPS) * zg).astype(BF16)
    for n in range(len(streams)):
        state[n] = S[n]


def _gdn(la, avec, dtvec, nw, B, T, TB, C):
    nT = T // TB
    W = 2 * LA_QK + LA_V
    la3 = la.reshape(B, T, la.shape[1])
    kern = functools.partial(_gdn_kernel, TB=TB, C=C)
    out = pl.pallas_call(
        kern,
        grid=(nT,),
        in_specs=[
            pl.BlockSpec((B, TB, W), lambda t: (0, t, 0)),
            pl.BlockSpec((B, TB, LA_V), lambda t: (0, t, W // LA_V)),
            pl.BlockSpec((B, TB, BA_PAD), lambda t: (0, t, (W + LA_V) // BA_PAD)),
            pl.BlockSpec((1, BA_PAD), lambda t: (0, 0)),
            pl.BlockSpec((1, BA_PAD), lambda t: (0, 0)),
            pl.BlockSpec((1, LA_DV), lambda t: (0, 0)),
        ],
        out_specs=pl.BlockSpec((B, TB, LA_V), lambda t: (0, t, 0)),
        out_shape=jax.ShapeDtypeStruct((B, T, LA_V), BF16),
        scratch_shapes=[
            pltpu.VMEM((B * LA_HEADS, LA_DK, LA_DV), F32),
            pltpu.VMEM((B, TB, BA_PAD), F32),
            pltpu.VMEM((B, LANES, TB), F32),
            pltpu.VMEM((B, TB, BA_PAD), F32),
        ],
        compiler_params=pltpu.CompilerParams(
            dimension_semantics=("arbitrary",), vmem_limit_bytes=VMEM_LIMIT),
        name="gdn",
    )(la3, la3, la3, avec, dtvec, nw.reshape(1, LA_DV))
    return out.reshape(B * T, LA_V)


def _transpose_blocks(x):
    n = x.shape[0] // LANES
    return jnp.concatenate([x[j * LANES:(j + 1) * LANES, :].T for j in range(n)], axis=1)


def _attn_kernel(lam_ref, tab_ref, q_ref, k_ref, v_ref, nw_ref, o_ref, bias_s, vt_s, *,
                 tq, lam_init):
    i = pl.program_id(2)
    n_blocks = vt_s.shape[1]
    heads = range(HEADS_PER_STEP)

    def cols(hh):
        return slice(hh * LANES, (hh + 1) * LANES)

    @pl.when(i == 0)
    def _():
        for hh in heads:
            for e in range(2):
                x = jnp.broadcast_to(tab_ref[hh, e], (tq, 2 * tq))
                bias_s[hh, e] = pltpu.roll(x, 0, 1, stride=1, stride_axis=0)[:, :tq]
            for j in range(n_blocks):
                vt_s[hh, j] = _transpose_blocks(
                    v_ref[0, j * tq:(j + 1) * tq, cols(hh)].astype(F32)).astype(BF16)

    q2t = []
    for hh in heads:
        qt = _transpose_blocks(q_ref[0, :, cols(hh)].astype(F32))
        sub = lax.broadcasted_iota(jnp.int32, qt.shape, 0)
        q2t.append(jnp.concatenate([jnp.where(sub < DIFF_DH, qt, 0.0),
                                    jnp.where(sub >= DIFF_DH, qt, 0.0)], axis=1).astype(BF16))

    def scores(hh, j):
        kj = k_ref[0, pl.ds(pl.multiple_of(j * tq, tq), tq), cols(hh)]
        return _dot(kj, q2t[hh])

    def biased(s, bias):
        return jnp.concatenate([s[:, :tq] + bias, s[:, tq:] + bias], axis=1)

    def update(hh, carry, s, j):
        m, l, acc = carry
        m_new = jnp.maximum(m, jnp.max(s, axis=0, keepdims=True))
        alpha = jnp.exp2(m - m_new)
        p = jnp.exp2(s - m_new)
        l = alpha * l + jnp.sum(p, axis=0, keepdims=True)
        acc = alpha * acc + _dot(vt_s[hh, j], p.astype(BF16))
        return m_new, l, acc

    def group_update(carries, blocks):
        ms = [c[0] for c in carries]
        ls = [c[1] for c in carries]
        accs = [c[2] for c in carries]
        m_news = list(ms)
        for u in range(len(blocks[0])):
            for hh in heads:
                j, thunk, off = blocks[hh][u]
                s = thunk()
                ref = ms[hh] if off is None else ms[hh] + off
                p = jnp.exp2(s - ref)
                ls[hh] = ls[hh] + jnp.sum(p, axis=0, keepdims=True)
                accs[hh] = accs[hh] + _dot(vt_s[hh, j], p.astype(BF16))
                smax = jnp.max(s, axis=0, keepdims=True)
                m_news[hh] = jnp.maximum(m_news[hh], smax if off is None else smax - off)
        fast = []
        lead = None
        for hh in heads:
            alpha = jnp.exp2(ms[hh] - m_news[hh])
            fast.append((m_news[hh], alpha * ls[hh], alpha * accs[hh]))
            lead_h = jnp.max(m_news[hh] - ms[hh])
            lead = lead_h if lead is None else jnp.maximum(lead, lead_h)

        def redo():
            out = []
            for hh in heads:
                c = carries[hh]
                for j, thunk, off in blocks[hh]:
                    s = thunk()
                    c = update(hh, c, s if off is None else s - off, j)
                out.append(c)
            return tuple(out)

        return lax.cond(lead <= MAX_LEAD, lambda: tuple(fast), redo)

    j_near = jnp.maximum(i - 1, 0)
    edge = LANES
    assert edge >= MAX_DISTANCE and tq >= 2 * edge

    def near_scores(hh):
        s = scores(hh, j_near)
        corner = bias_s[hh, 1, tq - edge:tq, 0:edge]
        bot = s[tq - edge:]
        bot = jnp.concatenate([bot[:, :edge] + corner, bot[:, edge:tq],
                               bot[:, tq:tq + edge] + corner, bot[:, tq + edge:]], axis=1)
        return jnp.concatenate([s[:tq - edge], bot], axis=0)

    no_near = jnp.where(i == 0, -NEG_BIG, 0.0)
    carries, tail = [], []
    for hh in heads:
        s_diag = biased(scores(hh, i), bias_s[hh, 0])
        m0 = jnp.max(s_diag[:DIFF_DH], axis=0, keepdims=True)
        carries.append((m0, jnp.zeros((1, 2 * tq), F32), jnp.zeros((DIFF_DV, 2 * tq), F32)))
        tail.append([(i, functools.partial(lambda s: s, s_diag), None),
                     (j_near, functools.partial(near_scores, hh), no_near)])
    carries = group_update(tuple(carries), tail)
    n_far = jnp.maximum(i - 1, 0)
    done = 0
    for width in FAR_UNROLLS:
        def body(g, c, width=width, done=done):
            js = [done + width * g + u for u in range(width)]
            return group_update(c, [[(j, functools.partial(scores, hh, j), None) for j in js]
                                    for hh in heads])
        n_groups = (n_far - done) // width
        carries = lax.fori_loop(0, n_groups, body, carries)
        done = done + n_groups * width

    lam = (jnp.exp(jnp.sum(lam_ref[0:1, :] * lam_ref[1:2, :], axis=-1, keepdims=True))
           - jnp.exp(jnp.sum(lam_ref[2:3, :] * lam_ref[3:4, :], axis=-1, keepdims=True)) + lam_init)
    for hh in heads:
        m, l, acc = carries[hh]
        inv_l = 1.0 / l
        ot = acc[:, :tq] * inv_l[:, :tq] - lam * (acc[:, tq:] * inv_l[:, tq:])
        o = jnp.concatenate([ot[:, j * LANES:(j + 1) * LANES].T for j in range(tq // LANES)],
                            axis=0)
        o_ref[0, :, cols(hh)] = (_rms(o, nw_ref[...], DIFF_NORM_EPS)
                                 * (1.0 - lam_init)).astype(BF16)


def _attn(dd, lams, bias_tab, nw, B, T, tq, lam_init):
    assert T % tq == 0 and DIFF_HEADS % HEADS_PER_STEP == 0
    nq = T // tq
    hw = HEADS_PER_STEP * LANES
    dd3 = dd.reshape(B, T, dd.shape[1])
    kern = functools.partial(_attn_kernel, tq=tq, lam_init=lam_init)
    out = pl.pallas_call(
        kern,
        grid=(B, DIFF_HEADS // HEADS_PER_STEP, nq),
        in_specs=[
            pl.BlockSpec((4, DIFF_DH), lambda b, g, i: (0, 0)),
            pl.BlockSpec((HEADS_PER_STEP, 2, 1, 2 * tq), lambda b, g, i: (g, 0, 0, 0)),
            pl.BlockSpec((1, tq, hw), lambda b, g, i: (b, i, g)),
            pl.BlockSpec((1, T, hw), lambda b, g, i: (b, 0, DIFF_QK // hw + g)),
            pl.BlockSpec((1, T, hw), lambda b, g, i: (b, 0, 2 * DIFF_QK // hw + g)),
            pl.BlockSpec((1, DIFF_DV), lambda b, g, i: (0, 0)),
        ],
        out_specs=pl.BlockSpec((1, tq, hw), lambda b, g, i: (b, i, g)),
        out_shape=jax.ShapeDtypeStruct((B, T, DIFF_V), BF16),
        scratch_shapes=[
            pltpu.VMEM((HEADS_PER_STEP, 2, tq, tq), F32),
            pltpu.VMEM((HEADS_PER_STEP, nq, DIFF_DV, tq), BF16),
        ],
        compiler_params=pltpu.CompilerParams(
            dimension_semantics=("parallel", "parallel", "arbitrary"),
            vmem_limit_bytes=VMEM_LIMIT),
        name="diff_attn",
    )(lams, bias_tab, dd3, dd3, dd3, nw.reshape(1, DIFF_DV))
    return out.reshape(B * T, DIFF_V)


def _t5_bucket(rel):
    n = jnp.maximum(rel, 0)
    max_exact = NUM_BUCKETS // 2
    nf = jnp.maximum(n, 1).astype(F32)
    large = max_exact + (jnp.log(nf / max_exact) / math.log(MAX_DISTANCE / max_exact)
                         * (NUM_BUCKETS - max_exact)).astype(jnp.int32)
    large = jnp.minimum(large, NUM_BUCKETS - 1)
    return jnp.where(n < max_exact, n, large)


def _bias_tables(rel_bias, tq):
    assert tq + 1 >= MAX_DISTANCE
    m = jnp.arange(2 * tq)
    far = rel_bias[NUM_BUCKETS - 1].astype(F32)
    tabs = []
    for d in (0, tq):
        rel = jnp.where(m < tq, d + m, d + m - 2 * tq)
        b = (rel_bias[_t5_bucket(rel)].astype(F32) - far) * math.log2(math.e)
        tabs.append(jnp.where((rel >= 0)[:, None], b, NEG_BIG).T)
    return jnp.stack(tabs, axis=1)[:, :, None, :]


def _ffn_kernel(x_ref, ola_ref, od_ref, wout_ref, nw_ref, wgu_ref, wd_ref, fnw_ref,
                out_ref, *, final_norm):
    d_ff = wd_ref.shape[0]
    y = (x_ref[...] + _dot(ola_ref[...], wout_ref[0:LA_V, :])
         + _dot(od_ref[...], wout_ref[LA_V:LA_V + DIFF_V, :]))
    h = _rms(y, nw_ref[...], NORM_EPS).astype(BF16)
    gate = _dot(h, wgu_ref[:, 0:d_ff])
    up = _dot(h, wgu_ref[:, d_ff:2 * d_ff])
    y = y + _dot((_silu(gate) * up).astype(BF16), wd_ref[...])
    if final_norm:
        y = _rms(y, fnw_ref[...], NORM_EPS)
    out_ref[...] = y


def _ffn(xf, o_la, o_d, wout, nw, wgu, wdn, fnw, layer, tm, final_norm):
    M, D = xf.shape
    kern = functools.partial(_ffn_kernel, final_norm=final_norm)
    resident = dict(pipeline_mode=pl.Buffered(1))

    def layer_slab(w):
        return pl.BlockSpec((None,) + w.shape[1:], lambda i: (layer, 0, 0), **resident)

    return pl.pallas_call(
        kern,
        grid=(M // tm,),
        in_specs=[
            pl.BlockSpec((tm, D), lambda i: (i, 0)),
            pl.BlockSpec((tm, LA_V), lambda i: (i, 0)),
            pl.BlockSpec((tm, DIFF_V), lambda i: (i, 0)),
            layer_slab(wout),
            pl.BlockSpec((1, D), lambda i: (0, 0)),
            layer_slab(wgu),
            layer_slab(wdn),
            pl.BlockSpec((1, D), lambda i: (0, 0)),
        ],
        out_specs=pl.BlockSpec((tm, D), lambda i: (i, 0)),
        out_shape=jax.ShapeDtypeStruct((M, D), F32),
        compiler_params=pltpu.CompilerParams(
            dimension_semantics=("parallel",), vmem_limit_bytes=VMEM_LIMIT),
        name="out_proj_ffn",
    )(xf, o_la, o_d, wout, nw.reshape(1, D), wgu, wdn, fnw.reshape(1, D))


def _pick(n, pref):
    return pref if n % pref == 0 else n


def kernel(x, attn_norm_w, w_in, conv_w, a_log, dt_bias, la_norm_w, lambda_q1, lambda_k1,
           lambda_q2, lambda_k2, diff_norm_w, rel_bias, w_out, ffn_norm_w, w_gate_up,
           w_down, final_norm_w):
    B, T, D = x.shape
    depth = w_in.shape[0]
    M = B * T
    d_ff = w_down.shape[1]
    tm_proj = _pick(M, 512)
    tm_ffn = _pick(M, 512)
    tb = _pick(T, 256)
    chunk = 64
    tq = _pick(T, 512)

    n_la = 2 * LA_QK + 2 * LA_V
    bias_tab = _bias_tables(rel_bias, tq)
    lane_pad = BA_PAD - 2 * LA_HEADS
    dscale = jnp.concatenate([jnp.full((DIFF_QK,), DIFF_DH ** -0.5 * math.log2(math.e), F32),
                              jnp.ones((DIFF_QK + DIFF_V,), F32)]).reshape(1, -1)

    w_in_bf = lax.optimization_barrier(w_in.astype(BF16))
    w_out_bf = w_out.astype(BF16)
    w_gate_up_bf = w_gate_up.astype(BF16)
    w_down_bf = w_down.astype(BF16)
    n_qkv = 2 * LA_QK + LA_V

    xf = x.reshape(M, D)
    for l in range(depth):
        w = w_in_bf[l]
        wqkv = jnp.stack([w[:, s * CONV_SLAB:(s + 1) * CONV_SLAB]
                          for s in range(n_qkv // CONV_SLAB)])
        wzba = jnp.concatenate([w[:, n_qkv:n_la + 2 * LA_HEADS],
                                jnp.zeros((D, lane_pad), w.dtype)], axis=1)
        wd = w[:, n_la + 2 * LA_HEADS:]
        la, dd = _in_proj(xf, attn_norm_w[l], wqkv, wzba, wd, dscale, conv_w[l], tm_proj, T)

        avec = jnp.concatenate([jnp.zeros((LA_HEADS,), F32), a_log[l].astype(F32),
                                jnp.zeros((lane_pad,), F32)]).reshape(1, BA_PAD)
        dtvec = jnp.concatenate([jnp.zeros((LA_HEADS,), F32), dt_bias[l].astype(F32),
                                 jnp.zeros((lane_pad,), F32)]).reshape(1, BA_PAD)
        o_la = _gdn(la, avec, dtvec, la_norm_w[l], B, T, tb, chunk)

        lam_init = 0.8 - 0.6 * math.exp(-0.3 * l)
        lams = jnp.stack([lambda_q1[l], lambda_k1[l], lambda_q2[l], lambda_k2[l]]).astype(F32)
        o_d = _attn(dd, lams, bias_tab, diff_norm_w[l], B, T, tq, lam_init)

        xf = _ffn(xf, o_la, o_d, w_out_bf, ffn_norm_w[l], w_gate_up_bf, w_down_bf, final_norm_w,
                  l, tm_ffn, final_norm=(l == depth - 1))
    return xf.reshape(B, T, D)
```

```python
import functools
import math

import jax
import jax.numpy as jnp
from jax import lax
from jax.experimental import pallas as pl
from jax.experimental.pallas import tpu as pltpu

LA_HEADS = 4
LA_DK = 128
LA_DV = 128
LA_QK = LA_HEADS * LA_DK
LA_V = LA_HEADS * LA_DV
CONV_K = 4
DIFF_HEADS = 4
DIFF_DH = 64
DIFF_DV = 2 * DIFF_DH
DIFF_QK = DIFF_HEADS * 2 * DIFF_DH
DIFF_V = DIFF_HEADS * DIFF_DV
NUM_BUCKETS = 32
MAX_DISTANCE = 128
NORM_EPS = 1e-6
DIFF_NORM_EPS = 1e-5
L2_EPS = 1e-6

LANES = 128
SUBLANES = 8
BA_PAD = LANES
LA_WIDTH = 2 * LA_QK + 2 * LA_V + BA_PAD
NEG_BIG = -1e30
FAR_UNROLLS = (4, 2, 1)
HEADS_PER_STEP = 2
CONV_SLAB = 256
MAX_LEAD = 40.0

VMEM_LIMIT = 52 * 1024 * 1024

F32 = jnp.float32
BF16 = jnp.bfloat16
HIGHEST = lax.Precision.HIGHEST


def _dot(a, b, precision=None):
    return jnp.dot(a, b, preferred_element_type=F32, precision=precision)


def _dot_nt(a, b, precision=None):
    return lax.dot_general(a, b, (((1,), (1,)), ((), ())),
                           preferred_element_type=F32, precision=precision)


def _dot_tn(a, b, precision=None):
    return lax.dot_general(a, b, (((0,), (0,)), ((), ())),
                           preferred_element_type=F32, precision=precision)


def _rms(x, w, eps):
    return x * lax.rsqrt(jnp.mean(x * x, axis=-1, keepdims=True) + eps) * w


def _silu(x):
    h = 0.5 * x
    return h + h * jnp.tanh(h)


def _softplus(x):
    return jnp.maximum(x, 0.0) + jnp.log1p(jnp.exp(-jnp.abs(x)))


def _in_proj_kernel(x_ref, nw_ref, wqkv_ref, wzba_ref, wd_ref, dscale_ref, convw_ref,
                    la_ref, d_ref, xbuf, *, tiles_per_seq):
    i = pl.program_id(0)
    tm = x_ref.shape[0]
    W = 2 * LA_QK + LA_V

    n_slabs, _, ws = xbuf.shape

    @pl.when(i % tiles_per_seq == 0)
    def _():
        xbuf[:, 0:SUBLANES, :] = jnp.zeros((n_slabs, SUBLANES, ws), F32)

    h = _rms(x_ref[...], nw_ref[...], NORM_EPS).astype(BF16)
    for s in range(n_slabs):
        xbuf[s, SUBLANES:SUBLANES + tm, :] = _dot(h, wqkv_ref[s])
        c0 = s * ws
        conv = convw_ref[CONV_K - 1:CONV_K, c0:c0 + ws] * xbuf[s, SUBLANES:SUBLANES + tm, :]
        for k in range(CONV_K - 1):
            off = SUBLANES - (CONV_K - 1) + k
            conv = conv + convw_ref[k:k + 1, c0:c0 + ws] * xbuf[s, off:off + tm, :]
        xbuf[s, 0:SUBLANES, :] = xbuf[s, tm:tm + SUBLANES, :]
        y = _silu(conv)
        if c0 >= 2 * LA_QK:
            la_ref[:, c0:c0 + ws] = y
            continue
        scale = LA_DK ** -0.5 if c0 < LA_QK else 1.0
        for hd in range(ws // LA_DK):
            yh = y[:, hd * LA_DK:(hd + 1) * LA_DK]
            yn = yh * lax.rsqrt(jnp.sum(yh * yh, -1, keepdims=True) + L2_EPS)
            la_ref[:, c0 + hd * LA_DK:c0 + (hd + 1) * LA_DK] = yn * scale if scale != 1.0 else yn
    d_ref[...] = (_dot(h, wd_ref[...]) * dscale_ref[...]).astype(BF16)
    la_ref[:, W:] = _dot(h, wzba_ref[...])


def _in_proj(xf, nw, wqkv, wzba, wd, dscale, convw, tm, T):
    M, D = xf.shape
    assert T % tm == 0
    W = 2 * LA_QK + LA_V
    kern = functools.partial(_in_proj_kernel, tiles_per_seq=T // tm)
    return pl.pallas_call(
        kern,
        grid=(M // tm,),
        in_specs=[
            pl.BlockSpec((tm, D), lambda i: (i, 0)),
            pl.BlockSpec((1, D), lambda i: (0, 0)),
            pl.BlockSpec(wqkv.shape, lambda i: (0, 0, 0)),
            pl.BlockSpec(wzba.shape, lambda i: (0, 0)),
            pl.BlockSpec(wd.shape, lambda i: (0, 0)),
            pl.BlockSpec(dscale.shape, lambda i: (0, 0)),
            pl.BlockSpec(convw.shape, lambda i: (0, 0)),
        ],
        out_specs=[
            pl.BlockSpec((tm, LA_WIDTH), lambda i: (i, 0)),
            pl.BlockSpec((tm, wd.shape[1]), lambda i: (i, 0)),
        ],
        out_shape=[
            jax.ShapeDtypeStruct((M, LA_WIDTH), F32),
            jax.ShapeDtypeStruct((M, wd.shape[1]), BF16),
        ],
        scratch_shapes=[pltpu.VMEM((wqkv.shape[0], tm + SUBLANES, wqkv.shape[2]), F32)],
        compiler_params=pltpu.CompilerParams(
            dimension_semantics=("arbitrary",), vmem_limit_bytes=VMEM_LIMIT),
        name="in_proj",
    )(xf, nw.reshape(1, D), wqkv, wzba, wd, dscale, convw)


def _unit_lower_inverse_minus_eye(Ls, C):
    row = lax.broadcasted_iota(jnp.int32, (C, C), 0)
    col = lax.broadcasted_iota(jnp.int32, (C, C), 1)
    xs = None
    s = 1
    while s < C:
        same_pair = (row // (2 * s)) == (col // (2 * s))
        low_left = same_pair & ((row // s) % 2 == 1) & ((col // s) % 2 == 0)
        offs = [jnp.where(low_left, L, 0.0) for L in Ls]
        if xs is None:
            xs = [-m for m in offs]
        else:
            xb = [x.astype(BF16) for x in xs]
            ys = [m + _dot(m.astype(BF16), x) for m, x in zip(offs, xb)]
            xs = [x - y - _dot(x16, y.astype(BF16)) for x, x16, y in zip(xs, xb, ys)]
        s *= 2
    return xs


def _gdn_kernel(qkv_ref, z_ref, ba_ref, avec_ref, dtvec_ref, nw_ref,
                o_ref, state, g_s, gt_s, beta_s, *, TB, C):
    t = pl.program_id(0)
    B = qkv_ref.shape[0]
    nC = TB // C

    @pl.when(t == 0)
    def _():
        state[...] = jnp.zeros(state.shape, F32)

    row = lax.broadcasted_iota(jnp.int32, (C, C), 0)
    col = lax.broadcasted_iota(jnp.int32, (C, C), 1)
    causal = row >= col
    strict = row > col
    tril = causal.astype(F32)
    ba_lane = lax.broadcasted_iota(jnp.int32, (1, BA_PAD), 1)
    is_g = (ba_lane >= LA_HEADS) & (ba_lane < 2 * LA_HEADS)

    for b in range(B):
        ba = ba_ref[b]
        beta_s[b] = jax.nn.sigmoid(ba)
        g_raw = -jnp.where(is_g, jnp.exp(avec_ref[...]), 0.0) * _softplus(ba + dtvec_ref[...])
        for c in range(nC):
            g_s[b, c * C:(c + 1) * C, :] = _dot(tril, g_raw[c * C:(c + 1) * C, :], HIGHEST)
        for j in range(TB // LANES):
            gt_s[b, :, j * LANES:(j + 1) * LANES] = g_s[b, j * LANES:(j + 1) * LANES, :].T

    probs = [(b, c, h) for b in range(B) for c in range(nC) for h in range(LA_HEADS)]
    index = {p: n for n, p in enumerate(probs)}
    Ls, intras, rhss, qgs, kds, egls = [], [], [], [], [], []
    for b, c, h in probs:
        rs = slice(c * C, (c + 1) * C)
        qn = qkv_ref[b, rs, h * LA_DK:(h + 1) * LA_DK]
        kn = qkv_ref[b, rs, LA_QK + h * LA_DK:LA_QK + (h + 1) * LA_DK]
        v = qkv_ref[b, rs, 2 * LA_QK + h * LA_DV:2 * LA_QK + (h + 1) * LA_DV]
        gb = jnp.broadcast_to(g_s[b, rs, LA_HEADS + h:LA_HEADS + h + 1], (C, LANES))
        beta = jnp.broadcast_to(beta_s[b, rs, h:h + 1], (C, LANES))
        g_row = gt_s[b, LA_HEADS + h:LA_HEADS + h + 1, rs]
        gdiff = gb[:, :C] - g_row
        decay = jnp.where(causal, jnp.exp(jnp.where(causal, gdiff, 0.0)), 0.0)
        kb = kn * beta
        kq = _dot_nt(jnp.concatenate([kb, qn], axis=0).astype(BF16), kn.astype(BF16))
        Ls.append(jnp.where(strict, kq[:C] * decay, 0.0))
        intras.append((kq[C:] * decay).astype(BF16))
        eg = jnp.exp(gb)
        rhss.append(jnp.concatenate([v * beta, kb * eg], axis=1))
        qgs.append((qn * eg).astype(BF16))
        g_last = gb[C - 1:C, :]
        kds.append((kn * jnp.exp(g_last - gb)).astype(BF16))
        egls.append(jnp.exp(g_last))

    xs = _unit_lower_inverse_minus_eye(Ls, C)
    sols = [r + _dot(x.astype(BF16), r.astype(BF16)) for x, r in zip(xs, rhss)]

    streams = [(b, h) for b in range(B) for h in range(LA_HEADS)]
    S = [state[b * LA_HEADS + h] for b, h in streams]
    for c in range(nC):
        ps = [index[(b, c, h)] for b, h in streams]
        r2 = [_dot(jnp.concatenate([sols[p][:, LA_DV:].astype(BF16), qgs[p]], axis=0),
                   S[n].astype(BF16)) for n, p in enumerate(ps)]
        vn = [(sols[p][:, :LA_DV] - r2[n][:C]).astype(BF16) for n, p in enumerate(ps)]
        S = [S[n] * egls[p] + _dot_tn(kds[p], vn[n]) for n, p in enumerate(ps)]
        rs = slice(c * C, (c + 1) * C)
        for n, ((b, h), p) in enumerate(zip(streams, ps)):
            o = r2[n][C:] + _dot(intras[p], vn[n])
            zg = _silu(z_ref[b, rs, h * LA_DV:(h + 1) * LA_DV])
            o_ref[b, rs, h * LA_DV:(h + 1) * LA_DV] = (
                _rms(o, nw_ref[...], NORM_EPS) * zg).astype(BF16)
    for n in range(len(streams)):
        state[n] = S[n]


def _gdn(la, avec, dtvec, nw, B, T, TB, C):
    nT = T // TB
    W = 2 * LA_QK + LA_V
    la3 = la.reshape(B, T, la.shape[1])
    kern = functools.partial(_gdn_kernel, TB=TB, C=C)
    out = pl.pallas_call(
        kern,
        grid=(nT,),
        in_specs=[
            pl.BlockSpec((B, TB, W), lambda t: (0, t, 0)),
            pl.BlockSpec((B, TB, LA_V), lambda t: (0, t, W // LA_V)),
            pl.BlockSpec((B, TB, BA_PAD), lambda t: (0, t, (W + LA_V) // BA_PAD)),
            pl.BlockSpec((1, BA_PAD), lambda t: (0, 0)),
            pl.BlockSpec((1, BA_PAD), lambda t: (0, 0)),
            pl.BlockSpec((1, LA_DV), lambda t: (0, 0)),
        ],
        out_specs=pl.BlockSpec((B, TB, LA_V), lambda t: (0, t, 0)),
        out_shape=jax.ShapeDtypeStruct((B, T, LA_V), BF16),
        scratch_shapes=[
            pltpu.VMEM((B * LA_HEADS, LA_DK, LA_DV), F32),
            pltpu.VMEM((B, TB, BA_PAD), F32),
            pltpu.VMEM((B, LANES, TB), F32),
            pltpu.VMEM((B, TB, BA_PAD), F32),
        ],
        compiler_params=pltpu.CompilerParams(
            dimension_semantics=("arbitrary",), vmem_limit_bytes=VMEM_LIMIT),
        name="gdn",
    )(la3, la3, la3, avec, dtvec, nw.reshape(1, LA_DV))
    return out.reshape(B * T, LA_V)


def _transpose_blocks(x):
    n = x.shape[0] // LANES
    return jnp.concatenate([x[j * LANES:(j + 1) * LANES, :].T for j in range(n)], axis=1)


def _attn_kernel(lam_ref, tab_ref, q_ref, k_ref, v_ref, nw_ref, o_ref, bias_s, vt_s, *,
                 tq, lam_init):
    i = pl.program_id(2)
    n_blocks = vt_s.shape[1]
    heads = range(HEADS_PER_STEP)

    def cols(hh):
        return slice(hh * LANES, (hh + 1) * LANES)

    @pl.when(i == 0)
    def _():
        for hh in heads:
            for e in range(2):
                x = jnp.broadcast_to(tab_ref[hh, e], (tq, 2 * tq))
                bias_s[hh, e] = pltpu.roll(x, 0, 1, stride=1, stride_axis=0)[:, :tq]
            for j in range(n_blocks):
                vt_s[hh, j] = _transpose_blocks(
                    v_ref[0, j * tq:(j + 1) * tq, cols(hh)].astype(F32)).astype(BF16)

    q2t = []
    for hh in heads:
        qt = _transpose_blocks(q_ref[0, :, cols(hh)].astype(F32))
        sub = lax.broadcasted_iota(jnp.int32, qt.shape, 0)
        q2t.append(jnp.concatenate([jnp.where(sub < DIFF_DH, qt, 0.0),
                                    jnp.where(sub >= DIFF_DH, qt, 0.0)], axis=1).astype(BF16))

    def scores(hh, j):
        kj = k_ref[0, pl.ds(pl.multiple_of(j * tq, tq), tq), cols(hh)]
        return _dot(kj, q2t[hh])

    def biased(s, bias):
        return jnp.concatenate([s[:, :tq] + bias, s[:, tq:] + bias], axis=1)

    def update(hh, carry, s, j):
        m, l, acc = carry
        m_new = jnp.maximum(m, jnp.max(s, axis=0, keepdims=True))
        alpha = jnp.exp2(m - m_new)
        p = jnp.exp2(s - m_new)
        l = alpha * l + jnp.sum(p, axis=0, keepdims=True)
        acc = alpha * acc + _dot(vt_s[hh, j], p.astype(BF16))
        return m_new, l, acc

    def group_update(carries, blocks):
        ms = [c[0] for c in carries]
        ls = [c[1] for c in carries]
        accs = [c[2] for c in carries]
        m_news = list(ms)
        for u in range(len(blocks[0])):
            for hh in heads:
                j, thunk, off = blocks[hh][u]
                s = thunk()
                ref = ms[hh] if off is None else ms[hh] + off
                p = jnp.exp2(s - ref)
                ls[hh] = ls[hh] + jnp.sum(p, axis=0, keepdims=True)
                accs[hh] = accs[hh] + _dot(vt_s[hh, j], p.astype(BF16))
                smax = jnp.max(s, axis=0, keepdims=True)
                m_news[hh] = jnp.maximum(m_news[hh], smax if off is None else smax - off)
        fast = []
        lead = None
        for hh in heads:
            alpha = jnp.exp2(ms[hh] - m_news[hh])
            fast.append((m_news[hh], alpha * ls[hh], alpha * accs[hh]))
            lead_h = jnp.max(m_news[hh] - ms[hh])
            lead = lead_h if lead is None else jnp.maximum(lead, lead_h)

        def redo():
            out = []
            for hh in heads:
                c = carries[hh]
                for j, thunk, off in blocks[hh]:
                    s = thunk()
                    c = update(hh, c, s if off is None else s - off, j)
                out.append(c)
            return tuple(out)

        return lax.cond(lead <= MAX_LEAD, lambda: tuple(fast), redo)

    j_near = jnp.maximum(i - 1, 0)
    edge = LANES
    assert edge >= MAX_DISTANCE and tq >= 2 * edge

    def near_scores(hh):
        s = scores(hh, j_near)
        corner = bias_s[hh, 1, tq - edge:tq, 0:edge]
        bot = s[tq - edge:]
        bot = jnp.concatenate([bot[:, :edge] + corner, bot[:, edge:tq],
                               bot[:, tq:tq + edge] + corner, bot[:, tq + edge:]], axis=1)
        return jnp.concatenate([s[:tq - edge], bot], axis=0)

    no_near = jnp.where(i == 0, -NEG_BIG, 0.0)
    carries, tail = [], []
    for hh in heads:
        s_diag = biased(scores(hh, i), bias_s[hh, 0])
        m0 = jnp.max(s_diag[:DIFF_DH], axis=0, keepdims=True)
        carries.append((m0, jnp.zeros((1, 2 * tq), F32), jnp.zeros((DIFF_DV, 2 * tq), F32)))
        tail.append([(i, functools.partial(lambda s: s, s_diag), None),
                     (j_near, functools.partial(near_scores, hh), no_near)])
    carries = group_update(tuple(carries), tail)
    n_far = jnp.maximum(i - 1, 0)
    done = 0
    for width in FAR_UNROLLS:
        def body(g, c, width=width, done=done):
            js = [done + width * g + u for u in range(width)]
            return group_update(c, [[(j, functools.partial(scores, hh, j), None) for j in js]
                                    for hh in heads])
        n_groups = (n_far - done) // width
        carries = lax.fori_loop(0, n_groups, body, carries)
        done = done + n_groups * width

    lam = (jnp.exp(jnp.sum(lam_ref[0:1, :] * lam_ref[1:2, :], axis=-1, keepdims=True))
           - jnp.exp(jnp.sum(lam_ref[2:3, :] * lam_ref[3:4, :], axis=-1, keepdims=True)) + lam_init)
    for hh in heads:
        m, l, acc = carries[hh]
        inv_l = 1.0 / l
        ot = acc[:, :tq] * inv_l[:, :tq] - lam * (acc[:, tq:] * inv_l[:, tq:])
        o = jnp.concatenate([ot[:, j * LANES:(j + 1) * LANES].T for j in range(tq // LANES)],
                            axis=0)
        o_ref[0, :, cols(hh)] = (_rms(o, nw_ref[...], DIFF_NORM_EPS)
                                 * (1.0 - lam_init)).astype(BF16)


def _attn(dd, lams, bias_tab, nw, B, T, tq, lam_init):
    assert T % tq == 0 and DIFF_HEADS % HEADS_PER_STEP == 0
    nq = T // tq
    hw = HEADS_PER_STEP * LANES
    dd3 = dd.reshape(B, T, dd.shape[1])
    kern = functools.partial(_attn_kernel, tq=tq, lam_init=lam_init)
    out = pl.pallas_call(
        kern,
        grid=(B, DIFF_HEADS // HEADS_PER_STEP, nq),
        in_specs=[
            pl.BlockSpec((4, DIFF_DH), lambda b, g, i: (0, 0)),
            pl.BlockSpec((HEADS_PER_STEP, 2, 1, 2 * tq), lambda b, g, i: (g, 0, 0, 0)),
            pl.BlockSpec((1, tq, hw), lambda b, g, i: (b, i, g)),
            pl.BlockSpec((1, T, hw), lambda b, g, i: (b, 0, DIFF_QK // hw + g)),
            pl.BlockSpec((1, T, hw), lambda b, g, i: (b, 0, 2 * DIFF_QK // hw + g)),
            pl.BlockSpec((1, DIFF_DV), lambda b, g, i: (0, 0)),
        ],
        out_specs=pl.BlockSpec((1, tq, hw), lambda b, g, i: (b, i, g)),
        out_shape=jax.ShapeDtypeStruct((B, T, DIFF_V), BF16),
        scratch_shapes=[
            pltpu.VMEM((HEADS_PER_STEP, 2, tq, tq), F32),
            pltpu.VMEM((HEADS_PER_STEP, nq, DIFF_DV, tq), BF16),
        ],
        compiler_params=pltpu.CompilerParams(
            dimension_semantics=("parallel", "parallel", "arbitrary"),
            vmem_limit_bytes=VMEM_LIMIT),
        name="diff_attn",
    )(lams, bias_tab, dd3, dd3, dd3, nw.reshape(1, DIFF_DV))
    return out.reshape(B * T, DIFF_V)


def _t5_bucket(rel):
    n = jnp.maximum(rel, 0)
    max_exact = NUM_BUCKETS // 2
    nf = jnp.maximum(n, 1).astype(F32)
    large = max_exact + (jnp.log(nf / max_exact) / math.log(MAX_DISTANCE / max_exact)
                         * (NUM_BUCKETS - max_exact)).astype(jnp.int32)
    large = jnp.minimum(large, NUM_BUCKETS - 1)
    return jnp.where(n < max_exact, n, large)


def _bias_tables(rel_bias, tq):
    assert tq + 1 >= MAX_DISTANCE
    m = jnp.arange(2 * tq)
    far = rel_bias[NUM_BUCKETS - 1].astype(F32)
    tabs = []
    for d in (0, tq):
        rel = jnp.where(m < tq, d + m, d + m - 2 * tq)
        b = (rel_bias[_t5_bucket(rel)].astype(F32) - far) * math.log2(math.e)
        tabs.append(jnp.where((rel >= 0)[:, None], b, NEG_BIG).T)
    return jnp.stack(tabs, axis=1)[:, :, None, :]


def _ffn_kernel(x_ref, ola_ref, od_ref, wout_ref, nw_ref, wgu_ref, wd_ref, fnw_ref,
                out_ref, *, final_norm):
    d_ff = wd_ref.shape[0]
    y = (x_ref[...] + _dot(ola_ref[...], wout_ref[0:LA_V, :])
         + _dot(od_ref[...], wout_ref[LA_V:LA_V + DIFF_V, :]))
    h = _rms(y, nw_ref[...], NORM_EPS).astype(BF16)
    gate = _dot(h, wgu_ref[:, 0:d_ff])
    up = _dot(h, wgu_ref[:, d_ff:2 * d_ff])
    y = y + _dot((_silu(gate) * up).astype(BF16), wd_ref[...])
    if final_norm:
        y = _rms(y, fnw_ref[...], NORM_EPS)
    out_ref[...] = y


def _ffn(xf, o_la, o_d, wout, nw, wgu, wdn, fnw, layer, tm, final_norm):
    M, D = xf.shape
    kern = functools.partial(_ffn_kernel, final_norm=final_norm)
    resident = dict(pipeline_mode=pl.Buffered(1))

    def layer_slab(w):
        return pl.BlockSpec((None,) + w.shape[1:], lambda i: (layer, 0, 0), **resident)

    return pl.pallas_call(
        kern,
        grid=(M // tm,),
        in_specs=[
            pl.BlockSpec((tm, D), lambda i: (i, 0)),
            pl.BlockSpec((tm, LA_V), lambda i: (i, 0)),
            pl.BlockSpec((tm, DIFF_V), lambda i: (i, 0)),
            layer_slab(wout),
            pl.BlockSpec((1, D), lambda i: (0, 0)),
            layer_slab(wgu),
            layer_slab(wdn),
            pl.BlockSpec((1, D), lambda i: (0, 0)),
        ],
        out_specs=pl.BlockSpec((tm, D), lambda i: (i, 0)),
        out_shape=jax.ShapeDtypeStruct((M, D), F32),
        compiler_params=pltpu.CompilerParams(
            dimension_semantics=("parallel",), vmem_limit_bytes=VMEM_LIMIT),
        name="out_proj_ffn",
    )(xf, o_la, o_d, wout, nw.reshape(1, D), wgu, wdn, fnw.reshape(1, D))


def _pick(n, pref):
    return pref if n % pref == 0 else n


def kernel(x, attn_norm_w, w_in, conv_w, a_log, dt_bias, la_norm_w, lambda_q1, lambda_k1,
           lambda_q2, lambda_k2, diff_norm_w, rel_bias, w_out, ffn_norm_w, w_gate_up,
           w_down, final_norm_w):
    B, T, D = x.shape
    depth = w_in.shape[0]
    M = B * T
    d_ff = w_down.shape[1]
    tm_proj = _pick(M, 512)
    tm_ffn = _pick(M, 512)
    tb = _pick(T, 256)
    chunk = 128
    tq = _pick(T, 512)

    n_la = 2 * LA_QK + 2 * LA_V
    bias_tab = _bias_tables(rel_bias, tq)
    lane_pad = BA_PAD - 2 * LA_HEADS
    dscale = jnp.concatenate([jnp.full((DIFF_QK,), DIFF_DH ** -0.5 * math.log2(math.e), F32),
                              jnp.ones((DIFF_QK + DIFF_V,), F32)]).reshape(1, -1)

    w_in_bf = lax.optimization_barrier(w_in.astype(BF16))
    w_out_bf = w_out.astype(BF16)
    w_gate_up_bf = w_gate_up.astype(BF16)
    w_down_bf = w_down.astype(BF16)
    n_qkv = 2 * LA_QK + LA_V

    xf = x.reshape(M, D)
    for l in range(depth):
        w = w_in_bf[l]
        wqkv = jnp.stack([w[:, s * CONV_SLAB:(s + 1) * CONV_SLAB]
                          for s in range(n_qkv // CONV_SLAB)])
        wzba = jnp.concatenate([w[:, n_qkv:n_la + 2 * LA_HEADS],
                                jnp.zeros((D, lane_pad), w.dtype)], axis=1)
        wd = w[:, n_la + 2 * LA_HEADS:]
        la, dd = _in_proj(xf, attn_norm_w[l], wqkv, wzba, wd, dscale, conv_w[l], tm_proj, T)

        avec = jnp.concatenate([jnp.zeros((LA_HEADS,), F32), a_log[l].astype(F32),
                                jnp.zeros((lane_pad,), F32)]).reshape(1, BA_PAD)
        dtvec = jnp.concatenate([jnp.zeros((LA_HEADS,), F32), dt_bias[l].astype(F32),
                                 jnp.zeros((lane_pad,), F32)]).reshape(1, BA_PAD)
        o_la = _gdn(la, avec, dtvec, la_norm_w[l], B, T, tb, chunk)

        lam_init = 0.8 - 0.6 * math.exp(-0.3 * l)
        lams = jnp.stack([lambda_q1[l], lambda_k1[l], lambda_q2[l], lambda_k2[l]]).astype(F32)
        o_d = _attn(dd, lams, bias_tab, diff_norm_w[l], B, T, tq, lam_init)

        xf = _ffn(xf, o_la, o_d, w_out_bf, ffn_norm_w[l], w_gate_up_bf, w_down_bf, final_norm_w,
                  l, tm_ffn, final_norm=(l == depth - 1))
    return xf.reshape(B, T, D)
```

```python
import functools
import math

import jax
import jax.numpy as jnp
from jax import lax
from jax.experimental import pallas as pl
from jax.experimental.pallas import tpu as pltpu

LA_HEADS = 4
LA_DK = 128
LA_DV = 128
LA_QK = LA_HEADS * LA_DK
LA_V = LA_HEADS * LA_DV
CONV_K = 4
DIFF_HEADS = 4
DIFF_DH = 64
DIFF_DV = 2 * DIFF_DH
DIFF_QK = DIFF_HEADS * 2 * DIFF_DH
DIFF_V = DIFF_HEADS * DIFF_DV
NUM_BUCKETS = 32
MAX_DISTANCE = 128
NORM_EPS = 1e-6
DIFF_NORM_EPS = 1e-5
L2_EPS = 1e-6

LANES = 128
SUBLANES = 8
BA_PAD = LANES
LA_WIDTH = 2 * LA_QK + 2 * LA_V + BA_PAD
NEG_BIG = -1e30
FAR_UNROLLS = (4, 2, 1)
HEADS_PER_STEP = 2
CONV_SLAB = 256
MAX_LEAD = 40.0

VMEM_LIMIT = 52 * 1024 * 1024

F32 = jnp.float32
BF16 = jnp.bfloat16
HIGHEST = lax.Precision.HIGHEST


def _dot(a, b, precision=None):
    return jnp.dot(a, b, preferred_element_type=F32, precision=precision)


def _dot_nt(a, b, precision=None):
    return lax.dot_general(a, b, (((1,), (1,)), ((), ())),
                           preferred_element_type=F32, precision=precision)


def _dot_tn(a, b, precision=None):
    return lax.dot_general(a, b, (((0,), (0,)), ((), ())),
                           preferred_element_type=F32, precision=precision)


def _rms(x, w, eps):
    return x * lax.rsqrt(jnp.mean(x * x, axis=-1, keepdims=True) + eps) * w


def _silu(x):
    h = 0.5 * x
    return h + h * jnp.tanh(h)


def _softplus(x):
    return jnp.maximum(x, 0.0) + jnp.log1p(jnp.exp(-jnp.abs(x)))


def _in_proj_kernel(x_ref, nw_ref, wqkv_ref, wzba_ref, wqt_ref, wkv_ref, convw_ref,
                    la_ref, qt_ref, kv_ref, xbuf, *, tiles_per_seq):
    i = pl.program_id(0)
    tm = x_ref.shape[0]
    W = 2 * LA_QK + LA_V

    n_slabs, _, ws = xbuf.shape

    @pl.when(i % tiles_per_seq == 0)
    def _():
        xbuf[:, 0:SUBLANES, :] = jnp.zeros((n_slabs, SUBLANES, ws), F32)

    h = _rms(x_ref[...], nw_ref[...], NORM_EPS).astype(BF16)
    for s in range(n_slabs):
        xbuf[s, SUBLANES:SUBLANES + tm, :] = _dot(h, wqkv_ref[s])
        c0 = s * ws
        conv = convw_ref[CONV_K - 1:CONV_K, c0:c0 + ws] * xbuf[s, SUBLANES:SUBLANES + tm, :]
        for k in range(CONV_K - 1):
            off = SUBLANES - (CONV_K - 1) + k
            conv = conv + convw_ref[k:k + 1, c0:c0 + ws] * xbuf[s, off:off + tm, :]
        xbuf[s, 0:SUBLANES, :] = xbuf[s, tm:tm + SUBLANES, :]
        y = _silu(conv)
        if c0 >= 2 * LA_QK:
            la_ref[:, c0:c0 + ws] = y
            continue
        scale = LA_DK ** -0.5 if c0 < LA_QK else 1.0
        for hd in range(ws // LA_DK):
            yh = y[:, hd * LA_DK:(hd + 1) * LA_DK]
            yn = yh * lax.rsqrt(jnp.sum(yh * yh, -1, keepdims=True) + L2_EPS)
            la_ref[:, c0 + hd * LA_DK:c0 + (hd + 1) * LA_DK] = yn * scale if scale != 1.0 else yn
    qt_ref[...] = (_dot_nt(wqt_ref[...], h) * (DIFF_DH ** -0.5 * math.log2(math.e))).astype(BF16)
    kv_ref[...] = _dot(h, wkv_ref[...]).astype(BF16)
    la_ref[:, W:] = _dot(h, wzba_ref[...])


def _in_proj(xf, nw, wqkv, wzba, wqt, wkv, convw, tm, T):
    M, D = xf.shape
    assert T % tm == 0
    kern = functools.partial(_in_proj_kernel, tiles_per_seq=T // tm)
    return pl.pallas_call(
        kern,
        grid=(M // tm,),
        in_specs=[
            pl.BlockSpec((tm, D), lambda i: (i, 0)),
            pl.BlockSpec((1, D), lambda i: (0, 0)),
            pl.BlockSpec(wqkv.shape, lambda i: (0, 0, 0)),
            pl.BlockSpec(wzba.shape, lambda i: (0, 0)),
            pl.BlockSpec(wqt.shape, lambda i: (0, 0)),
            pl.BlockSpec(wkv.shape, lambda i: (0, 0)),
            pl.BlockSpec(convw.shape, lambda i: (0, 0)),
        ],
        out_specs=[
            pl.BlockSpec((tm, LA_WIDTH), lambda i: (i, 0)),
            pl.BlockSpec((wqt.shape[0], tm), lambda i: (0, i)),
            pl.BlockSpec((tm, wkv.shape[1]), lambda i: (i, 0)),
        ],
        out_shape=[
            jax.ShapeDtypeStruct((M, LA_WIDTH), F32),
            jax.ShapeDtypeStruct((wqt.shape[0], M), BF16),
            jax.ShapeDtypeStruct((M, wkv.shape[1]), BF16),
        ],
        scratch_shapes=[pltpu.VMEM((wqkv.shape[0], tm + SUBLANES, wqkv.shape[2]), F32)],
        compiler_params=pltpu.CompilerParams(
            dimension_semantics=("arbitrary",), vmem_limit_bytes=VMEM_LIMIT),
        name="in_proj",
    )(xf, nw.reshape(1, D), wqkv, wzba, wqt, wkv, convw)


def _unit_lower_inverse_minus_eye(Ls, C):
    row = lax.broadcasted_iota(jnp.int32, (C, C), 0)
    col = lax.broadcasted_iota(jnp.int32, (C, C), 1)
    xs = None
    s = 1
    while s < C:
        same_pair = (row // (2 * s)) == (col // (2 * s))
        low_left = same_pair & ((row // s) % 2 == 1) & ((col // s) % 2 == 0)
        offs = [jnp.where(low_left, L, 0.0) for L in Ls]
        if xs is None:
            xs = [-m for m in offs]
        else:
            xb = [x.astype(BF16) for x in xs]
            ys = [m + _dot(m.astype(BF16), x) for m, x in zip(offs, xb)]
            xs = [x - y - _dot(x16, y.astype(BF16)) for x, x16, y in zip(xs, xb, ys)]
        s *= 2
    return xs


def _gdn_kernel(qkv_ref, z_ref, ba_ref, avec_ref, dtvec_ref, nw_ref,
                o_ref, state, g_s, gt_s, beta_s, *, TB, C):
    t = pl.program_id(0)
    B = qkv_ref.shape[0]
    nC = TB // C

    @pl.when(t == 0)
    def _():
        state[...] = jnp.zeros(state.shape, F32)

    row = lax.broadcasted_iota(jnp.int32, (C, C), 0)
    col = lax.broadcasted_iota(jnp.int32, (C, C), 1)
    causal = row >= col
    strict = row > col
    tril = causal.astype(F32)
    ba_lane = lax.broadcasted_iota(jnp.int32, (1, BA_PAD), 1)
    is_g = (ba_lane >= LA_HEADS) & (ba_lane < 2 * LA_HEADS)

    for b in range(B):
        ba = ba_ref[b]
        beta_s[b] = jax.nn.sigmoid(ba)
        g_raw = -jnp.where(is_g, jnp.exp(avec_ref[...]), 0.0) * _softplus(ba + dtvec_ref[...])
        for c in range(nC):
            g_s[b, c * C:(c + 1) * C, :] = _dot(tril, g_raw[c * C:(c + 1) * C, :], HIGHEST)
        for j in range(TB // LANES):
            gt_s[b, :, j * LANES:(j + 1) * LANES] = g_s[b, j * LANES:(j + 1) * LANES, :].T

    probs = [(b, c, h) for b in range(B) for c in range(nC) for h in range(LA_HEADS)]
    index = {p: n for n, p in enumerate(probs)}
    Ls, intras, rhss, qgs, kds, egls = [], [], [], [], [], []
    for b, c, h in probs:
        rs = slice(c * C, (c + 1) * C)
        qn = qkv_ref[b, rs, h * LA_DK:(h + 1) * LA_DK]
        kn = qkv_ref[b, rs, LA_QK + h * LA_DK:LA_QK + (h + 1) * LA_DK]
        v = qkv_ref[b, rs, 2 * LA_QK + h * LA_DV:2 * LA_QK + (h + 1) * LA_DV]
        gb = jnp.broadcast_to(g_s[b, rs, LA_HEADS + h:LA_HEADS + h + 1], (C, LANES))
        beta = jnp.broadcast_to(beta_s[b, rs, h:h + 1], (C, LANES))
        g_row = gt_s[b, LA_HEADS + h:LA_HEADS + h + 1, rs]
        gdiff = gb[:, :C] - g_row
        decay = jnp.where(causal, jnp.exp(jnp.where(causal, gdiff, 0.0)), 0.0)
        kb = kn * beta
        kq = _dot_nt(jnp.concatenate([kb, qn], axis=0).astype(BF16), kn.astype(BF16))
        Ls.append(jnp.where(strict, kq[:C] * decay, 0.0))
        intras.append((kq[C:] * decay).astype(BF16))
        eg = jnp.exp(gb)
        rhss.append(jnp.concatenate([v * beta, kb * eg], axis=1))
        qgs.append((qn * eg).astype(BF16))
        g_last = gb[C - 1:C, :]
        kds.append((kn * jnp.exp(g_last - gb)).astype(BF16))
        egls.append(jnp.exp(g_last))

    xs = _unit_lower_inverse_minus_eye(Ls, C)
    sols = [r + _dot(x.astype(BF16), r.astype(BF16)) for x, r in zip(xs, rhss)]

    streams = [(b, h) for b in range(B) for h in range(LA_HEADS)]
    S = [state[b * LA_HEADS + h] for b, h in streams]
    for c in range(nC):
        ps = [index[(b, c, h)] for b, h in streams]
        r2 = [_dot(jnp.concatenate([sols[p][:, LA_DV:].astype(BF16), qgs[p]], axis=0),
                   S[n].astype(BF16)) for n, p in enumerate(ps)]
        vn = [(sols[p][:, :LA_DV] - r2[n][:C]).astype(BF16) for n, p in enumerate(ps)]
        S = [S[n] * egls[p] + _dot_tn(kds[p], vn[n]) for n, p in enumerate(ps)]
        rs = slice(c * C, (c + 1) * C)
        for n, ((b, h), p) in enumerate(zip(streams, ps)):
            o = r2[n][C:] + _dot(intras[p], vn[n])
            zg = _silu(z_ref[b, rs, h * LA_DV:(h + 1) * LA_DV])
            o_ref[b, rs, h * LA_DV:(h + 1) * LA_DV] = (
                _rms(o, nw_ref[...], NORM_EPS) * zg).astype(BF16)
    for n in range(len(streams)):
        state[n] = S[n]


def _gdn(la, avec, dtvec, nw, B, T, TB, C):
    nT = T // TB
    W = 2 * LA_QK + LA_V
    la3 = la.reshape(B, T, la.shape[1])
    kern = functools.partial(_gdn_kernel, TB=TB, C=C)
    out = pl.pallas_call(
        kern,
        grid=(nT,),
        in_specs=[
            pl.BlockSpec((B, TB, W), lambda t: (0, t, 0)),
            pl.BlockSpec((B, TB, LA_V), lambda t: (0, t, W // LA_V)),
            pl.BlockSpec((B, TB, BA_PAD), lambda t: (0, t, (W + LA_V) // BA_PAD)),
            pl.BlockSpec((1, BA_PAD), lambda t: (0, 0)),
            pl.BlockSpec((1, BA_PAD), lambda t: (0, 0)),
            pl.BlockSpec((1, LA_DV), lambda t: (0, 0)),
        ],
        out_specs=pl.BlockSpec((B, TB, LA_V), lambda t: (0, t, 0)),
        out_shape=jax.ShapeDtypeStruct((B, T, LA_V), BF16),
        scratch_shapes=[
            pltpu.VMEM((B * LA_HEADS, LA_DK, LA_DV), F32),
            pltpu.VMEM((B, TB, BA_PAD), F32),
            pltpu.VMEM((B, LANES, TB), F32),
            pltpu.VMEM((B, TB, BA_PAD), F32),
        ],
        compiler_params=pltpu.CompilerParams(
            dimension_semantics=("arbitrary",), vmem_limit_bytes=VMEM_LIMIT),
        name="gdn",
    )(la3, la3, la3, avec, dtvec, nw.reshape(1, LA_DV))
    return out.reshape(B * T, LA_V)


def _transpose_blocks(x):
    n = x.shape[0] // LANES
    return jnp.concatenate([x[j * LANES:(j + 1) * LANES, :].T for j in range(n)], axis=1)


def _attn_kernel(lam_ref, tab_ref, q_ref, k_ref, v_ref, nw_ref, o_ref, bias_s, vt_s, *,
                 tq, lam_init):
    i = pl.program_id(2)
    n_blocks = vt_s.shape[1]
    heads = range(HEADS_PER_STEP)

    def cols(hh):
        return slice(hh * LANES, (hh + 1) * LANES)

    @pl.when(i == 0)
    def _():
        for hh in heads:
            for e in range(2):
                x = jnp.broadcast_to(tab_ref[hh, e], (tq, 2 * tq))
                bias_s[hh, e] = pltpu.roll(x, 0, 1, stride=1, stride_axis=0)[:, :tq]
            for j in range(n_blocks):
                vt_s[hh, j] = _transpose_blocks(
                    v_ref[0, j * tq:(j + 1) * tq, cols(hh)].astype(F32)).astype(BF16)

    q2t = []
    for hh in heads:
        qt = q_ref[hh * LANES:(hh + 1) * LANES, :].astype(F32)
        sub = lax.broadcasted_iota(jnp.int32, qt.shape, 0)
        q2t.append(jnp.concatenate([jnp.where(sub < DIFF_DH, qt, 0.0),
                                    jnp.where(sub >= DIFF_DH, qt, 0.0)], axis=1).astype(BF16))

    def scores(hh, j):
        kj = k_ref[0, pl.ds(pl.multiple_of(j * tq, tq), tq), cols(hh)]
        return _dot(kj, q2t[hh])

    def biased(s, bias):
        return jnp.concatenate([s[:, :tq] + bias, s[:, tq:] + bias], axis=1)

    def update(hh, carry, s, j):
        m, l, acc = carry
        m_new = jnp.maximum(m, jnp.max(s, axis=0, keepdims=True))
        alpha = jnp.exp2(m - m_new)
        p = jnp.exp2(s - m_new)
        l = alpha * l + jnp.sum(p, axis=0, keepdims=True)
        acc = alpha * acc + _dot(vt_s[hh, j], p.astype(BF16))
        return m_new, l, acc

    def group_update(carries, blocks):
        ms = [c[0] for c in carries]
        ls = [c[1] for c in carries]
        accs = [c[2] for c in carries]
        m_news = list(ms)
        for u in range(len(blocks[0])):
            for hh in heads:
                j, thunk, off = blocks[hh][u]
                s = thunk()
                ref = ms[hh] if off is None else ms[hh] + off
                p = jnp.exp2(s - ref)
                ls[hh] = ls[hh] + jnp.sum(p, axis=0, keepdims=True)
                accs[hh] = accs[hh] + _dot(vt_s[hh, j], p.astype(BF16))
                smax = jnp.max(s, axis=0, keepdims=True)
                m_news[hh] = jnp.maximum(m_news[hh], smax if off is None else smax - off)
        fast = []
        lead = None
        for hh in heads:
            alpha = jnp.exp2(ms[hh] - m_news[hh])
            fast.append((m_news[hh], alpha * ls[hh], alpha * accs[hh]))
            lead_h = jnp.max(m_news[hh] - ms[hh])
            lead = lead_h if lead is None else jnp.maximum(lead, lead_h)

        def redo():
            out = []
            for hh in heads:
                c = carries[hh]
                for j, thunk, off in blocks[hh]:
                    s = thunk()
                    c = update(hh, c, s if off is None else s - off, j)
                out.append(c)
            return tuple(out)

        return lax.cond(lead <= MAX_LEAD, lambda: tuple(fast), redo)

    j_near = jnp.maximum(i - 1, 0)
    edge = LANES
    assert edge >= MAX_DISTANCE and tq >= 2 * edge

    def near_scores(hh):
        s = scores(hh, j_near)
        corner = bias_s[hh, 1, tq - edge:tq, 0:edge]
        bot = s[tq - edge:]
        bot = jnp.concatenate([bot[:, :edge] + corner, bot[:, edge:tq],
                               bot[:, tq:tq + edge] + corner, bot[:, tq + edge:]], axis=1)
        return jnp.concatenate([s[:tq - edge], bot], axis=0)

    no_near = jnp.where(i == 0, -NEG_BIG, 0.0)
    carries, tail = [], []
    for hh in heads:
        s_diag = biased(scores(hh, i), bias_s[hh, 0])
        m0 = jnp.max(s_diag[:DIFF_DH], axis=0, keepdims=True)
        carries.append((m0, jnp.zeros((1, 2 * tq), F32), jnp.zeros((DIFF_DV, 2 * tq), F32)))
        tail.append([(i, functools.partial(lambda s: s, s_diag), None),
                     (j_near, functools.partial(near_scores, hh), no_near)])
    carries = group_update(tuple(carries), tail)
    n_far = jnp.maximum(i - 1, 0)
    done = 0
    for width in FAR_UNROLLS:
        def body(g, c, width=width, done=done):
            js = [done + width * g + u for u in range(width)]
            return group_update(c, [[(j, functools.partial(scores, hh, j), None) for j in js]
                                    for hh in heads])
        n_groups = (n_far - done) // width
        carries = lax.fori_loop(0, n_groups, body, carries)
        done = done + n_groups * width

    lam = (jnp.exp(jnp.sum(lam_ref[0:1, :] * lam_ref[1:2, :], axis=-1, keepdims=True))
           - jnp.exp(jnp.sum(lam_ref[2:3, :] * lam_ref[3:4, :], axis=-1, keepdims=True)) + lam_init)
    for hh in heads:
        m, l, acc = carries[hh]
        inv_l = 1.0 / l
        ot = acc[:, :tq] * inv_l[:, :tq] - lam * (acc[:, tq:] * inv_l[:, tq:])
        gain = jnp.concatenate([nw_ref[...]] * (tq // LANES), axis=1)
        ms = jnp.mean(ot * ot, axis=0, keepdims=True)
        o_ref[hh * LANES:(hh + 1) * LANES, :] = (
            ot * lax.rsqrt(ms + DIFF_NORM_EPS) * gain * (1.0 - lam_init)).astype(BF16)


def _attn(qt, kv, lams, bias_tab, nw, B, T, tq, lam_init):
    assert T % tq == 0 and DIFF_HEADS % HEADS_PER_STEP == 0
    nq = T // tq
    hw = HEADS_PER_STEP * LANES
    kv3 = kv.reshape(B, T, kv.shape[1])
    kern = functools.partial(_attn_kernel, tq=tq, lam_init=lam_init)
    return pl.pallas_call(
        kern,
        grid=(B, DIFF_HEADS // HEADS_PER_STEP, nq),
        in_specs=[
            pl.BlockSpec((4, DIFF_DH), lambda b, g, i: (0, 0)),
            pl.BlockSpec((HEADS_PER_STEP, 2, 1, 2 * tq), lambda b, g, i: (g, 0, 0, 0)),
            pl.BlockSpec((hw, tq), lambda b, g, i: (g, b * nq + i)),
            pl.BlockSpec((1, T, hw), lambda b, g, i: (b, 0, g)),
            pl.BlockSpec((1, T, hw), lambda b, g, i: (b, 0, DIFF_QK // hw + g)),
            pl.BlockSpec((DIFF_DV, LANES), lambda b, g, i: (0, 0)),
        ],
        out_specs=pl.BlockSpec((hw, tq), lambda b, g, i: (g, b * nq + i)),
        out_shape=jax.ShapeDtypeStruct((DIFF_V, B * T), BF16),
        scratch_shapes=[
            pltpu.VMEM((HEADS_PER_STEP, 2, tq, tq), F32),
            pltpu.VMEM((HEADS_PER_STEP, nq, DIFF_DV, tq), BF16),
        ],
        compiler_params=pltpu.CompilerParams(
            dimension_semantics=("parallel", "parallel", "arbitrary"),
            vmem_limit_bytes=VMEM_LIMIT),
        name="diff_attn",
    )(lams, bias_tab, qt, kv3, kv3, jnp.broadcast_to(nw.astype(F32)[:, None], (DIFF_DV, LANES)))


def _t5_bucket(rel):
    n = jnp.maximum(rel, 0)
    max_exact = NUM_BUCKETS // 2
    nf = jnp.maximum(n, 1).astype(F32)
    large = max_exact + (jnp.log(nf / max_exact) / math.log(MAX_DISTANCE / max_exact)
                         * (NUM_BUCKETS - max_exact)).astype(jnp.int32)
    large = jnp.minimum(large, NUM_BUCKETS - 1)
    return jnp.where(n < max_exact, n, large)


def _bias_tables(rel_bias, tq):
    assert tq + 1 >= MAX_DISTANCE
    m = jnp.arange(2 * tq)
    far = rel_bias[NUM_BUCKETS - 1].astype(F32)
    tabs = []
    for d in (0, tq):
        rel = jnp.where(m < tq, d + m, d + m - 2 * tq)
        b = (rel_bias[_t5_bucket(rel)].astype(F32) - far) * math.log2(math.e)
        tabs.append(jnp.where((rel >= 0)[:, None], b, NEG_BIG).T)
    return jnp.stack(tabs, axis=1)[:, :, None, :]


def _ffn_kernel(x_ref, ola_ref, od_ref, wout_ref, nw_ref, wgu_ref, wd_ref, fnw_ref,
                out_ref, *, final_norm):
    d_ff = wd_ref.shape[0]
    y = (x_ref[...] + _dot(ola_ref[...], wout_ref[0:LA_V, :])
         + _dot_tn(od_ref[...], wout_ref[LA_V:LA_V + DIFF_V, :]))
    h = _rms(y, nw_ref[...], NORM_EPS).astype(BF16)
    gate = _dot(h, wgu_ref[:, 0:d_ff])
    up = _dot(h, wgu_ref[:, d_ff:2 * d_ff])
    y = y + _dot((_silu(gate) * up).astype(BF16), wd_ref[...])
    if final_norm:
        y = _rms(y, fnw_ref[...], NORM_EPS)
    out_ref[...] = y


def _ffn(xf, o_la, o_d, wout, nw, wgu, wdn, fnw, layer, tm, final_norm):
    M, D = xf.shape
    kern = functools.partial(_ffn_kernel, final_norm=final_norm)
    resident = dict(pipeline_mode=pl.Buffered(1))

    def layer_slab(w):
        return pl.BlockSpec((None,) + w.shape[1:], lambda i: (layer, 0, 0), **resident)

    return pl.pallas_call(
        kern,
        grid=(M // tm,),
        in_specs=[
            pl.BlockSpec((tm, D), lambda i: (i, 0)),
            pl.BlockSpec((tm, LA_V), lambda i: (i, 0)),
            pl.BlockSpec((DIFF_V, tm), lambda i: (0, i)),
            layer_slab(wout),
            pl.BlockSpec((1, D), lambda i: (0, 0)),
            layer_slab(wgu),
            layer_slab(wdn),
            pl.BlockSpec((1, D), lambda i: (0, 0)),
        ],
        out_specs=pl.BlockSpec((tm, D), lambda i: (i, 0)),
        out_shape=jax.ShapeDtypeStruct((M, D), F32),
        compiler_params=pltpu.CompilerParams(
            dimension_semantics=("parallel",), vmem_limit_bytes=VMEM_LIMIT),
        name="out_proj_ffn",
    )(xf, o_la, o_d, wout, nw.reshape(1, D), wgu, wdn, fnw.reshape(1, D))


def _pick(n, pref):
    return pref if n % pref == 0 else n


def kernel(x, attn_norm_w, w_in, conv_w, a_log, dt_bias, la_norm_w, lambda_q1, lambda_k1,
           lambda_q2, lambda_k2, diff_norm_w, rel_bias, w_out, ffn_norm_w, w_gate_up,
           w_down, final_norm_w):
    B, T, D = x.shape
    depth = w_in.shape[0]
    M = B * T
    d_ff = w_down.shape[1]
    tm_proj = _pick(M, 512)
    tm_ffn = _pick(M, 512)
    tb = _pick(T, 256)
    chunk = 128
    tq = _pick(T, 512)

    n_la = 2 * LA_QK + 2 * LA_V
    bias_tab = _bias_tables(rel_bias, tq)
    lane_pad = BA_PAD - 2 * LA_HEADS

    w_in_bf = lax.optimization_barrier(w_in.astype(BF16))
    w_out_bf = w_out.astype(BF16)
    w_gate_up_bf = w_gate_up.astype(BF16)
    w_down_bf = w_down.astype(BF16)
    n_qkv = 2 * LA_QK + LA_V

    xf = x.reshape(M, D)
    for l in range(depth):
        w = w_in_bf[l]
        wqkv = jnp.stack([w[:, s * CONV_SLAB:(s + 1) * CONV_SLAB]
                          for s in range(n_qkv // CONV_SLAB)])
        wzba = jnp.concatenate([w[:, n_qkv:n_la + 2 * LA_HEADS],
                                jnp.zeros((D, lane_pad), w.dtype)], axis=1)
        n_d = n_la + 2 * LA_HEADS
        wqt = w[:, n_d:n_d + DIFF_QK].T
        wkv = w[:, n_d + DIFF_QK:]
        la, qt, kv = _in_proj(xf, attn_norm_w[l], wqkv, wzba, wqt, wkv, conv_w[l], tm_proj, T)

        avec = jnp.concatenate([jnp.zeros((LA_HEADS,), F32), a_log[l].astype(F32),
                                jnp.zeros((lane_pad,), F32)]).reshape(1, BA_PAD)
        dtvec = jnp.concatenate([jnp.zeros((LA_HEADS,), F32), dt_bias[l].astype(F32),
                                 jnp.zeros((lane_pad,), F32)]).reshape(1, BA_PAD)
        o_la = _gdn(la, avec, dtvec, la_norm_w[l], B, T, tb, chunk)

        lam_init = 0.8 - 0.6 * math.exp(-0.3 * l)
        lams = jnp.stack([lambda_q1[l], lambda_k1[l], lambda_q2[l], lambda_k2[l]]).astype(F32)
        o_d = _attn(qt, kv, lams, bias_tab, diff_norm_w[l], B, T, tq, lam_init)

        xf = _ffn(xf, o_la, o_d, w_out_bf, ffn_norm_w[l], w_gate_up_bf, w_down_bf, final_norm_w,
                  l, tm_ffn, final_norm=(l == depth - 1))
    return xf.reshape(B, T, D)
```

```python
import functools
import math

import jax
import jax.numpy as jnp
from jax import lax
from jax.experimental import pallas as pl
from jax.experimental.pallas import tpu as pltpu

LA_HEADS = 4
LA_DK = 128
LA_DV = 128
LA_QK = LA_HEADS * LA_DK
LA_V = LA_HEADS * LA_DV
CONV_K = 4
DIFF_HEADS = 4
DIFF_DH = 64
DIFF_DV = 2 * DIFF_DH
DIFF_QK = DIFF_HEADS * 2 * DIFF_DH
DIFF_V = DIFF_HEADS * DIFF_DV
NUM_BUCKETS = 32
MAX_DISTANCE = 128
NORM_EPS = 1e-6
DIFF_NORM_EPS = 1e-5
L2_EPS = 1e-6

LANES = 128
SUBLANES = 8
BA_PAD = LANES
LA_WIDTH = 2 * LA_QK + 2 * LA_V + BA_PAD
NEG_BIG = -1e30
FAR_UNROLLS = (4, 2, 1)
HEADS_PER_STEP = 2
CONV_SLAB = 256
MAX_LEAD = 40.0

VMEM_LIMIT = 52 * 1024 * 1024

F32 = jnp.float32
BF16 = jnp.bfloat16
HIGHEST = lax.Precision.HIGHEST


def _dot(a, b, precision=None):
    return jnp.dot(a, b, preferred_element_type=F32, precision=precision)


def _dot_nt(a, b, precision=None):
    return lax.dot_general(a, b, (((1,), (1,)), ((), ())),
                           preferred_element_type=F32, precision=precision)


def _dot_tn(a, b, precision=None):
    return lax.dot_general(a, b, (((0,), (0,)), ((), ())),
                           preferred_element_type=F32, precision=precision)


def _rms(x, w, eps):
    return x * lax.rsqrt(jnp.mean(x * x, axis=-1, keepdims=True) + eps) * w


def _silu(x):
    h = 0.5 * x
    return h + h * jnp.tanh(h)


def _softplus(x):
    return jnp.maximum(x, 0.0) + jnp.log1p(jnp.exp(-jnp.abs(x)))


def _in_proj_kernel(x_ref, nw_ref, wqkv_ref, wzba_ref, wqvt_ref, wk_ref, convw_ref,
                    la_ref, qt_ref, vt_ref, k_ref, xbuf, *, tiles_per_seq):
    i = pl.program_id(0)
    tm = x_ref.shape[0]
    W = 2 * LA_QK + LA_V

    n_slabs, _, ws = xbuf.shape

    @pl.when(i % tiles_per_seq == 0)
    def _():
        xbuf[:, 0:SUBLANES, :] = jnp.zeros((n_slabs, SUBLANES, ws), F32)

    h = _rms(x_ref[...], nw_ref[...], NORM_EPS).astype(BF16)
    for s in range(n_slabs):
        xbuf[s, SUBLANES:SUBLANES + tm, :] = _dot(h, wqkv_ref[s])
        c0 = s * ws
        conv = convw_ref[CONV_K - 1:CONV_K, c0:c0 + ws] * xbuf[s, SUBLANES:SUBLANES + tm, :]
        for k in range(CONV_K - 1):
            off = SUBLANES - (CONV_K - 1) + k
            conv = conv + convw_ref[k:k + 1, c0:c0 + ws] * xbuf[s, off:off + tm, :]
        xbuf[s, 0:SUBLANES, :] = xbuf[s, tm:tm + SUBLANES, :]
        y = _silu(conv)
        if c0 >= 2 * LA_QK:
            la_ref[:, c0:c0 + ws] = y
            continue
        scale = LA_DK ** -0.5 if c0 < LA_QK else 1.0
        for hd in range(ws // LA_DK):
            yh = y[:, hd * LA_DK:(hd + 1) * LA_DK]
            yn = yh * lax.rsqrt(jnp.sum(yh * yh, -1, keepdims=True) + L2_EPS)
            la_ref[:, c0 + hd * LA_DK:c0 + (hd + 1) * LA_DK] = yn * scale if scale != 1.0 else yn
    qvt = _dot_nt(wqvt_ref[...], h)
    qt_ref[...] = (qvt[:DIFF_QK] * (DIFF_DH ** -0.5 * math.log2(math.e))).astype(BF16)
    vt_ref[0] = qvt[DIFF_QK:].astype(BF16)
    k_ref[...] = _dot(h, wk_ref[...]).astype(BF16)
    la_ref[:, W:] = _dot(h, wzba_ref[...])


def _in_proj(xf, nw, wqkv, wzba, wqvt, wk, convw, tm, T):
    M, D = xf.shape
    assert T % tm == 0
    kern = functools.partial(_in_proj_kernel, tiles_per_seq=T // tm)
    return pl.pallas_call(
        kern,
        grid=(M // tm,),
        in_specs=[
            pl.BlockSpec((tm, D), lambda i: (i, 0)),
            pl.BlockSpec((1, D), lambda i: (0, 0)),
            pl.BlockSpec(wqkv.shape, lambda i: (0, 0, 0)),
            pl.BlockSpec(wzba.shape, lambda i: (0, 0)),
            pl.BlockSpec(wqvt.shape, lambda i: (0, 0)),
            pl.BlockSpec(wk.shape, lambda i: (0, 0)),
            pl.BlockSpec(convw.shape, lambda i: (0, 0)),
        ],
        out_specs=[
            pl.BlockSpec((tm, LA_WIDTH), lambda i: (i, 0)),
            pl.BlockSpec((DIFF_QK, tm), lambda i: (0, i)),
            pl.BlockSpec((1, DIFF_V, tm), lambda i: (i, 0, 0)),
            pl.BlockSpec((tm, wk.shape[1]), lambda i: (i, 0)),
        ],
        out_shape=[
            jax.ShapeDtypeStruct((M, LA_WIDTH), F32),
            jax.ShapeDtypeStruct((DIFF_QK, M), BF16),
            jax.ShapeDtypeStruct((M // tm, DIFF_V, tm), BF16),
            jax.ShapeDtypeStruct((M, wk.shape[1]), BF16),
        ],
        scratch_shapes=[pltpu.VMEM((wqkv.shape[0], tm + SUBLANES, wqkv.shape[2]), F32)],
        compiler_params=pltpu.CompilerParams(
            dimension_semantics=("arbitrary",), vmem_limit_bytes=VMEM_LIMIT),
        name="in_proj",
    )(xf, nw.reshape(1, D), wqkv, wzba, wqvt, wk, convw)


def _unit_lower_inverse_minus_eye(Ls, C):
    row = lax.broadcasted_iota(jnp.int32, (C, C), 0)
    col = lax.broadcasted_iota(jnp.int32, (C, C), 1)
    xs = None
    s = 1
    while s < C:
        same_pair = (row // (2 * s)) == (col // (2 * s))
        low_left = same_pair & ((row // s) % 2 == 1) & ((col // s) % 2 == 0)
        offs = [jnp.where(low_left, L, 0.0) for L in Ls]
        if xs is None:
            xs = [-m for m in offs]
        else:
            xb = [x.astype(BF16) for x in xs]
            ys = [m + _dot(m.astype(BF16), x) for m, x in zip(offs, xb)]
            xs = [x - y - _dot(x16, y.astype(BF16)) for x, x16, y in zip(xs, xb, ys)]
        s *= 2
    return xs


def _gdn_kernel(qkv_ref, z_ref, ba_ref, avec_ref, dtvec_ref, nw_ref,
                o_ref, state, g_s, gt_s, beta_s, *, TB, C):
    t = pl.program_id(0)
    B = qkv_ref.shape[0]
    nC = TB // C

    @pl.when(t == 0)
    def _():
        state[...] = jnp.zeros(state.shape, F32)

    row = lax.broadcasted_iota(jnp.int32, (C, C), 0)
    col = lax.broadcasted_iota(jnp.int32, (C, C), 1)
    causal = row >= col
    strict = row > col
    tril = causal.astype(F32)
    ba_lane = lax.broadcasted_iota(jnp.int32, (1, BA_PAD), 1)
    is_g = (ba_lane >= LA_HEADS) & (ba_lane < 2 * LA_HEADS)

    for b in range(B):
        ba = ba_ref[b]
        beta_s[b] = jax.nn.sigmoid(ba)
        g_raw = -jnp.where(is_g, jnp.exp(avec_ref[...]), 0.0) * _softplus(ba + dtvec_ref[...])
        for c in range(nC):
            g_s[b, c * C:(c + 1) * C, :] = _dot(tril, g_raw[c * C:(c + 1) * C, :], HIGHEST)
        for j in range(TB // LANES):
            gt_s[b, :, j * LANES:(j + 1) * LANES] = g_s[b, j * LANES:(j + 1) * LANES, :].T

    probs = [(b, c, h) for b in range(B) for c in range(nC) for h in range(LA_HEADS)]
    index = {p: n for n, p in enumerate(probs)}
    Ls, intras, rhss, qgs, kds, egls = [], [], [], [], [], []
    for b, c, h in probs:
        rs = slice(c * C, (c + 1) * C)
        qn = qkv_ref[b, rs, h * LA_DK:(h + 1) * LA_DK]
        kn = qkv_ref[b, rs, LA_QK + h * LA_DK:LA_QK + (h + 1) * LA_DK]
        v = qkv_ref[b, rs, 2 * LA_QK + h * LA_DV:2 * LA_QK + (h + 1) * LA_DV]
        gb = jnp.broadcast_to(g_s[b, rs, LA_HEADS + h:LA_HEADS + h + 1], (C, LANES))
        beta = jnp.broadcast_to(beta_s[b, rs, h:h + 1], (C, LANES))
        g_row = gt_s[b, LA_HEADS + h:LA_HEADS + h + 1, rs]
        gdiff = gb[:, :C] - g_row
        decay = jnp.where(causal, jnp.exp(jnp.where(causal, gdiff, 0.0)), 0.0)
        kb = kn * beta
        kq = _dot_nt(jnp.concatenate([kb, qn], axis=0).astype(BF16), kn.astype(BF16))
        Ls.append(jnp.where(strict, kq[:C] * decay, 0.0))
        intras.append((kq[C:] * decay).astype(BF16))
        eg = jnp.exp(gb)
        rhss.append(jnp.concatenate([v * beta, kb * eg], axis=1))
        qgs.append((qn * eg).astype(BF16))
        g_last = gb[C - 1:C, :]
        kds.append((kn * jnp.exp(g_last - gb)).astype(BF16))
        egls.append(jnp.exp(g_last))

    xs = _unit_lower_inverse_minus_eye(Ls, C)
    sols = [r + _dot(x.astype(BF16), r.astype(BF16)) for x, r in zip(xs, rhss)]

    streams = [(b, h) for b in range(B) for h in range(LA_HEADS)]
    S = [state[b * LA_HEADS + h] for b, h in streams]
    for c in range(nC):
        ps = [index[(b, c, h)] for b, h in streams]
        r2 = [_dot(jnp.concatenate([sols[p][:, LA_DV:].astype(BF16), qgs[p]], axis=0),
                   S[n].astype(BF16)) for n, p in enumerate(ps)]
        vn = [(sols[p][:, :LA_DV] - r2[n][:C]).astype(BF16) for n, p in enumerate(ps)]
        S = [S[n] * egls[p] + _dot_tn(kds[p], vn[n]) for n, p in enumerate(ps)]
        rs = slice(c * C, (c + 1) * C)
        for n, ((b, h), p) in enumerate(zip(streams, ps)):
            o = r2[n][C:] + _dot(intras[p], vn[n])
            zg = _silu(z_ref[b, rs, h * LA_DV:(h + 1) * LA_DV])
            o_ref[b, rs, h * LA_DV:(h + 1) * LA_DV] = (
                _rms(o, nw_ref[...], NORM_EPS) * zg).astype(BF16)
    for n in range(len(streams)):
        state[n] = S[n]


def _gdn(la, avec, dtvec, nw, B, T, TB, C):
    nT = T // TB
    W = 2 * LA_QK + LA_V
    la3 = la.reshape(B, T, la.shape[1])
    kern = functools.partial(_gdn_kernel, TB=TB, C=C)
    out = pl.pallas_call(
        kern,
        grid=(nT,),
        in_specs=[
            pl.BlockSpec((B, TB, W), lambda t: (0, t, 0)),
            pl.BlockSpec((B, TB, LA_V), lambda t: (0, t, W // LA_V)),
            pl.BlockSpec((B, TB, BA_PAD), lambda t: (0, t, (W + LA_V) // BA_PAD)),
            pl.BlockSpec((1, BA_PAD), lambda t: (0, 0)),
            pl.BlockSpec((1, BA_PAD), lambda t: (0, 0)),
            pl.BlockSpec((1, LA_DV), lambda t: (0, 0)),
        ],
        out_specs=pl.BlockSpec((B, TB, LA_V), lambda t: (0, t, 0)),
        out_shape=jax.ShapeDtypeStruct((B, T, LA_V), BF16),
        scratch_shapes=[
            pltpu.VMEM((B * LA_HEADS, LA_DK, LA_DV), F32),
            pltpu.VMEM((B, TB, BA_PAD), F32),
            pltpu.VMEM((B, LANES, TB), F32),
            pltpu.VMEM((B, TB, BA_PAD), F32),
        ],
        compiler_params=pltpu.CompilerParams(
            dimension_semantics=("arbitrary",), vmem_limit_bytes=VMEM_LIMIT),
        name="gdn",
    )(la3, la3, la3, avec, dtvec, nw.reshape(1, LA_DV))
    return out.reshape(B * T, LA_V)


def _attn_kernel(lam_ref, tab_ref, q_ref, k_ref, vt_ref, nw_ref, o_ref, bias_s, *,
                 tq, lam_init):
    i = pl.program_id(2)
    heads = range(HEADS_PER_STEP)

    def cols(hh):
        return slice(hh * LANES, (hh + 1) * LANES)

    def vt(hh, j):
        return vt_ref[j, hh * LANES:(hh + 1) * LANES, :]

    @pl.when(i == 0)
    def _():
        for hh in heads:
            for e in range(2):
                x = jnp.broadcast_to(tab_ref[hh, e], (tq, 2 * tq))
                bias_s[hh, e] = pltpu.roll(x, 0, 1, stride=1, stride_axis=0)[:, :tq]

    q2t = []
    for hh in heads:
        qt = q_ref[hh * LANES:(hh + 1) * LANES, :].astype(F32)
        sub = lax.broadcasted_iota(jnp.int32, qt.shape, 0)
        q2t.append(jnp.concatenate([jnp.where(sub < DIFF_DH, qt, 0.0),
                                    jnp.where(sub >= DIFF_DH, qt, 0.0)], axis=1).astype(BF16))

    def scores(hh, j):
        kj = k_ref[0, pl.ds(pl.multiple_of(j * tq, tq), tq), cols(hh)]
        return _dot(kj, q2t[hh])

    def biased(s, bias):
        return jnp.concatenate([s[:, :tq] + bias, s[:, tq:] + bias], axis=1)

    def update(hh, carry, s, j):
        m, l, acc = carry
        m_new = jnp.maximum(m, jnp.max(s, axis=0, keepdims=True))
        alpha = jnp.exp2(m - m_new)
        p = jnp.exp2(s - m_new)
        l = alpha * l + jnp.sum(p, axis=0, keepdims=True)
        acc = alpha * acc + _dot(vt(hh, j), p.astype(BF16))
        return m_new, l, acc

    def group_update(carries, blocks):
        ms = [c[0] for c in carries]
        ls = [c[1] for c in carries]
        accs = [c[2] for c in carries]
        m_news = list(ms)
        for u in range(len(blocks[0])):
            for hh in heads:
                j, thunk, off = blocks[hh][u]
                s = thunk()
                ref = ms[hh] if off is None else ms[hh] + off
                p = jnp.exp2(s - ref)
                ls[hh] = ls[hh] + jnp.sum(p, axis=0, keepdims=True)
                accs[hh] = accs[hh] + _dot(vt(hh, j), p.astype(BF16))
                smax = jnp.max(s, axis=0, keepdims=True)
                m_news[hh] = jnp.maximum(m_news[hh], smax if off is None else smax - off)
        fast = []
        lead = None
        for hh in heads:
            alpha = jnp.exp2(ms[hh] - m_news[hh])
            fast.append((m_news[hh], alpha * ls[hh], alpha * accs[hh]))
            lead_h = jnp.max(m_news[hh] - ms[hh])
            lead = lead_h if lead is None else jnp.maximum(lead, lead_h)

        def redo():
            out = []
            for hh in heads:
                c = carries[hh]
                for j, thunk, off in blocks[hh]:
                    s = thunk()
                    c = update(hh, c, s if off is None else s - off, j)
                out.append(c)
            return tuple(out)

        return lax.cond(lead <= MAX_LEAD, lambda: tuple(fast), redo)

    j_near = jnp.maximum(i - 1, 0)
    edge = LANES
    assert edge >= MAX_DISTANCE and tq >= 2 * edge

    def near_scores(hh):
        s = scores(hh, j_near)
        corner = bias_s[hh, 1, tq - edge:tq, 0:edge]
        bot = s[tq - edge:]
        bot = jnp.concatenate([bot[:, :edge] + corner, bot[:, edge:tq],
                               bot[:, tq:tq + edge] + corner, bot[:, tq + edge:]], axis=1)
        return jnp.concatenate([s[:tq - edge], bot], axis=0)

    no_near = jnp.where(i == 0, -NEG_BIG, 0.0)
    carries, tail = [], []
    for hh in heads:
        s_diag = biased(scores(hh, i), bias_s[hh, 0])
        m0 = jnp.max(s_diag[:DIFF_DH], axis=0, keepdims=True)
        carries.append((m0, jnp.zeros((1, 2 * tq), F32), jnp.zeros((DIFF_DV, 2 * tq), F32)))
        tail.append([(i, functools.partial(lambda s: s, s_diag), None),
                     (j_near, functools.partial(near_scores, hh), no_near)])
    carries = group_update(tuple(carries), tail)
    n_far = jnp.maximum(i - 1, 0)
    done = 0
    for width in FAR_UNROLLS:
        def body(g, c, width=width, done=done):
            js = [done + width * g + u for u in range(width)]
            return group_update(c, [[(j, functools.partial(scores, hh, j), None) for j in js]
                                    for hh in heads])
        n_groups = (n_far - done) // width
        carries = lax.fori_loop(0, n_groups, body, carries)
        done = done + n_groups * width

    lam = (jnp.exp(jnp.sum(lam_ref[0:1, :] * lam_ref[1:2, :], axis=-1, keepdims=True))
           - jnp.exp(jnp.sum(lam_ref[2:3, :] * lam_ref[3:4, :], axis=-1, keepdims=True)) + lam_init)
    for hh in heads:
        m, l, acc = carries[hh]
        inv_l = 1.0 / l
        ot = acc[:, :tq] * inv_l[:, :tq] - lam * (acc[:, tq:] * inv_l[:, tq:])
        gain = jnp.concatenate([nw_ref[...]] * (tq // LANES), axis=1)
        ms = jnp.mean(ot * ot, axis=0, keepdims=True)
        o_ref[hh * LANES:(hh + 1) * LANES, :] = (
            ot * lax.rsqrt(ms + DIFF_NORM_EPS) * gain * (1.0 - lam_init)).astype(BF16)


def _attn(qt, vt, k, lams, bias_tab, nw, B, T, tq, lam_init):
    assert T % tq == 0 and DIFF_HEADS % HEADS_PER_STEP == 0 and vt.shape[2] == tq
    nq = T // tq
    hw = HEADS_PER_STEP * LANES
    k3 = k.reshape(B, T, k.shape[1])
    kern = functools.partial(_attn_kernel, tq=tq, lam_init=lam_init)
    return pl.pallas_call(
        kern,
        grid=(B, DIFF_HEADS // HEADS_PER_STEP, nq),
        in_specs=[
            pl.BlockSpec((4, DIFF_DH), lambda b, g, i: (0, 0)),
            pl.BlockSpec((HEADS_PER_STEP, 2, 1, 2 * tq), lambda b, g, i: (g, 0, 0, 0)),
            pl.BlockSpec((hw, tq), lambda b, g, i: (g, b * nq + i)),
            pl.BlockSpec((1, T, hw), lambda b, g, i: (b, 0, g)),
            pl.BlockSpec((nq, hw, tq), lambda b, g, i: (b, g, 0)),
            pl.BlockSpec((DIFF_DV, LANES), lambda b, g, i: (0, 0)),
        ],
        out_specs=pl.BlockSpec((hw, tq), lambda b, g, i: (g, b * nq + i)),
        out_shape=jax.ShapeDtypeStruct((DIFF_V, B * T), BF16),
        scratch_shapes=[
            pltpu.VMEM((HEADS_PER_STEP, 2, tq, tq), F32),
        ],
        compiler_params=pltpu.CompilerParams(
            dimension_semantics=("parallel", "parallel", "arbitrary"),
            vmem_limit_bytes=VMEM_LIMIT),
        name="diff_attn",
    )(lams, bias_tab, qt, k3, vt, jnp.broadcast_to(nw.astype(F32)[:, None], (DIFF_DV, LANES)))


def _t5_bucket(rel):
    n = jnp.maximum(rel, 0)
    max_exact = NUM_BUCKETS // 2
    nf = jnp.maximum(n, 1).astype(F32)
    large = max_exact + (jnp.log(nf / max_exact) / math.log(MAX_DISTANCE / max_exact)
                         * (NUM_BUCKETS - max_exact)).astype(jnp.int32)
    large = jnp.minimum(large, NUM_BUCKETS - 1)
    return jnp.where(n < max_exact, n, large)


def _bias_tables(rel_bias, tq):
    assert tq + 1 >= MAX_DISTANCE
    m = jnp.arange(2 * tq)
    far = rel_bias[NUM_BUCKETS - 1].astype(F32)
    tabs = []
    for d in (0, tq):
        rel = jnp.where(m < tq, d + m, d + m - 2 * tq)
        b = (rel_bias[_t5_bucket(rel)].astype(F32) - far) * math.log2(math.e)
        tabs.append(jnp.where((rel >= 0)[:, None], b, NEG_BIG).T)
    return jnp.stack(tabs, axis=1)[:, :, None, :]


def _ffn_kernel(x_ref, ola_ref, od_ref, wout_ref, nw_ref, wgu_ref, wd_ref, fnw_ref,
                out_ref, *, final_norm):
    d_ff = wd_ref.shape[0]
    y = (x_ref[...] + _dot(ola_ref[...], wout_ref[0:LA_V, :])
         + _dot_tn(od_ref[...], wout_ref[LA_V:LA_V + DIFF_V, :]))
    h = _rms(y, nw_ref[...], NORM_EPS).astype(BF16)
    gate = _dot(h, wgu_ref[:, 0:d_ff])
    up = _dot(h, wgu_ref[:, d_ff:2 * d_ff])
    y = y + _dot((_silu(gate) * up).astype(BF16), wd_ref[...])
    if final_norm:
        y = _rms(y, fnw_ref[...], NORM_EPS)
    out_ref[...] = y


def _ffn(xf, o_la, o_d, wout, nw, wgu, wdn, fnw, layer, tm, final_norm):
    M, D = xf.shape
    kern = functools.partial(_ffn_kernel, final_norm=final_norm)
    resident = dict(pipeline_mode=pl.Buffered(1))

    def layer_slab(w):
        return pl.BlockSpec((None,) + w.shape[1:], lambda i: (layer, 0, 0), **resident)

    return pl.pallas_call(
        kern,
        grid=(M // tm,),
        in_specs=[
            pl.BlockSpec((tm, D), lambda i: (i, 0)),
            pl.BlockSpec((tm, LA_V), lambda i: (i, 0)),
            pl.BlockSpec((DIFF_V, tm), lambda i: (0, i)),
            layer_slab(wout),
            pl.BlockSpec((1, D), lambda i: (0, 0)),
            layer_slab(wgu),
            layer_slab(wdn),
            pl.BlockSpec((1, D), lambda i: (0, 0)),
        ],
        out_specs=pl.BlockSpec((tm, D), lambda i: (i, 0)),
        out_shape=jax.ShapeDtypeStruct((M, D), F32),
        compiler_params=pltpu.CompilerParams(
            dimension_semantics=("parallel",), vmem_limit_bytes=VMEM_LIMIT),
        name="out_proj_ffn",
    )(xf, o_la, o_d, wout, nw.reshape(1, D), wgu, wdn, fnw.reshape(1, D))


def _pick(n, pref):
    return pref if n % pref == 0 else n


def kernel(x, attn_norm_w, w_in, conv_w, a_log, dt_bias, la_norm_w, lambda_q1, lambda_k1,
           lambda_q2, lambda_k2, diff_norm_w, rel_bias, w_out, ffn_norm_w, w_gate_up,
           w_down, final_norm_w):
    B, T, D = x.shape
    depth = w_in.shape[0]
    M = B * T
    tm_proj = _pick(M, 512)
    tm_ffn = _pick(M, 512)
    tb = _pick(T, 256)
    chunk = 128
    tq = _pick(T, 512)

    n_la = 2 * LA_QK + 2 * LA_V
    bias_tab = _bias_tables(rel_bias, tq)
    lane_pad = BA_PAD - 2 * LA_HEADS

    w_in_bf = lax.optimization_barrier(w_in.astype(BF16))
    w_out_bf = w_out.astype(BF16)
    w_gate_up_bf = w_gate_up.astype(BF16)
    w_down_bf = w_down.astype(BF16)
    n_qkv = 2 * LA_QK + LA_V

    xf = x.reshape(M, D)
    for l in range(depth):
        w = w_in_bf[l]
        wqkv = jnp.stack([w[:, s * CONV_SLAB:(s + 1) * CONV_SLAB]
                          for s in range(n_qkv // CONV_SLAB)])
        wzba = jnp.concatenate([w[:, n_qkv:n_la + 2 * LA_HEADS],
                                jnp.zeros((D, lane_pad), w.dtype)], axis=1)
        n_d = n_la + 2 * LA_HEADS
        wqvt = jnp.concatenate([w[:, n_d:n_d + DIFF_QK], w[:, n_d + 2 * DIFF_QK:]], axis=1).T
        wk = w[:, n_d + DIFF_QK:n_d + 2 * DIFF_QK]
        assert tm_proj == tq
        la, qt, vt, k = _in_proj(xf, attn_norm_w[l], wqkv, wzba, wqvt, wk, conv_w[l], tm_proj, T)

        avec = jnp.concatenate([jnp.zeros((LA_HEADS,), F32), a_log[l].astype(F32),
                                jnp.zeros((lane_pad,), F32)]).reshape(1, BA_PAD)
        dtvec = jnp.concatenate([jnp.zeros((LA_HEADS,), F32), dt_bias[l].astype(F32),
                                 jnp.zeros((lane_pad,), F32)]).reshape(1, BA_PAD)
        o_la = _gdn(la, avec, dtvec, la_norm_w[l], B, T, tb, chunk)

        lam_init = 0.8 - 0.6 * math.exp(-0.3 * l)
        lams = jnp.stack([lambda_q1[l], lambda_k1[l], lambda_q2[l], lambda_k2[l]]).astype(F32)
        o_d = _attn(qt, vt, k, lams, bias_tab, diff_norm_w[l], B, T, tq, lam_init)

        xf = _ffn(xf, o_la, o_d, w_out_bf, ffn_norm_w[l], w_gate_up_bf, w_down_bf, final_norm_w,
                  l, tm_ffn, final_norm=(l == depth - 1))
    return xf.reshape(B, T, D)
```

```python
import functools
import math

import jax
import jax.numpy as jnp
from jax import lax
from jax.experimental import pallas as pl
from jax.experimental.pallas import tpu as pltpu

LA_HEADS = 4
LA_DK = 128
LA_DV = 128
LA_QK = LA_HEADS * LA_DK
LA_V = LA_HEADS * LA_DV
CONV_K = 4
DIFF_HEADS = 4
DIFF_DH = 64
DIFF_DV = 2 * DIFF_DH
DIFF_QK = DIFF_HEADS * 2 * DIFF_DH
DIFF_V = DIFF_HEADS * DIFF_DV
NUM_BUCKETS = 32
MAX_DISTANCE = 128
NORM_EPS = 1e-6
DIFF_NORM_EPS = 1e-5
L2_EPS = 1e-6

LANES = 128
SUBLANES = 8
BA_PAD = LANES
LA_WIDTH = 2 * LA_QK + 2 * LA_V + BA_PAD
NEG_BIG = -1e30
FAR_UNROLLS = (4, 2, 1)
FAR_SPAN = 2
HEADS_PER_STEP = 2
CONV_SLAB = 256
MAX_LEAD = 40.0

VMEM_LIMIT = 52 * 1024 * 1024

F32 = jnp.float32
BF16 = jnp.bfloat16
HIGHEST = lax.Precision.HIGHEST


def _dot(a, b, precision=None):
    return jnp.dot(a, b, preferred_element_type=F32, precision=precision)


def _dot_nt(a, b, precision=None):
    return lax.dot_general(a, b, (((1,), (1,)), ((), ())),
                           preferred_element_type=F32, precision=precision)


def _dot_tn(a, b, precision=None):
    return lax.dot_general(a, b, (((0,), (0,)), ((), ())),
                           preferred_element_type=F32, precision=precision)


def _rms(x, w, eps):
    return x * lax.rsqrt(jnp.mean(x * x, axis=-1, keepdims=True) + eps) * w


def _silu(x):
    h = 0.5 * x
    return h + h * jnp.tanh(h)


def _softplus(x):
    return jnp.maximum(x, 0.0) + jnp.log1p(jnp.exp(-jnp.abs(x)))


def _in_proj_kernel(x_ref, nw_ref, wqkv_ref, wzba_ref, wqvt_ref, wk_ref, convw_ref,
                    la_ref, qt_ref, vt_ref, k_ref, xbuf, *, tiles_per_seq):
    i = pl.program_id(0)
    tm = x_ref.shape[0]
    W = 2 * LA_QK + LA_V

    n_slabs, _, ws = xbuf.shape

    @pl.when(i % tiles_per_seq == 0)
    def _():
        xbuf[:, 0:SUBLANES, :] = jnp.zeros((n_slabs, SUBLANES, ws), F32)

    h = _rms(x_ref[...], nw_ref[...], NORM_EPS).astype(BF16)
    for s in range(n_slabs):
        xbuf[s, SUBLANES:SUBLANES + tm, :] = _dot(h, wqkv_ref[s])
        c0 = s * ws
        conv = convw_ref[CONV_K - 1:CONV_K, c0:c0 + ws] * xbuf[s, SUBLANES:SUBLANES + tm, :]
        for k in range(CONV_K - 1):
            off = SUBLANES - (CONV_K - 1) + k
            conv = conv + convw_ref[k:k + 1, c0:c0 + ws] * xbuf[s, off:off + tm, :]
        xbuf[s, 0:SUBLANES, :] = xbuf[s, tm:tm + SUBLANES, :]
        y = _silu(conv)
        if c0 >= 2 * LA_QK:
            la_ref[:, c0:c0 + ws] = y
            continue
        scale = LA_DK ** -0.5 if c0 < LA_QK else 1.0
        for hd in range(ws // LA_DK):
            yh = y[:, hd * LA_DK:(hd + 1) * LA_DK]
            yn = yh * lax.rsqrt(jnp.sum(yh * yh, -1, keepdims=True) + L2_EPS)
            la_ref[:, c0 + hd * LA_DK:c0 + (hd + 1) * LA_DK] = yn * scale if scale != 1.0 else yn
    qvt = _dot_nt(wqvt_ref[...], h)
    qt_ref[...] = (qvt[:DIFF_QK] * (DIFF_DH ** -0.5 * math.log2(math.e))).astype(BF16)
    vt_ref[0] = qvt[DIFF_QK:].astype(BF16)
    k_ref[...] = _dot(h, wk_ref[...]).astype(BF16)
    la_ref[:, W:] = _dot(h, wzba_ref[...])


def _in_proj(xf, nw, wqkv, wzba, wqvt, wk, convw, tm, T):
    M, D = xf.shape
    assert T % tm == 0
    kern = functools.partial(_in_proj_kernel, tiles_per_seq=T // tm)
    return pl.pallas_call(
        kern,
        grid=(M // tm,),
        in_specs=[
            pl.BlockSpec((tm, D), lambda i: (i, 0)),
            pl.BlockSpec((1, D), lambda i: (0, 0)),
            pl.BlockSpec(wqkv.shape, lambda i: (0, 0, 0)),
            pl.BlockSpec(wzba.shape, lambda i: (0, 0)),
            pl.BlockSpec(wqvt.shape, lambda i: (0, 0)),
            pl.BlockSpec(wk.shape, lambda i: (0, 0)),
            pl.BlockSpec(convw.shape, lambda i: (0, 0)),
        ],
        out_specs=[
            pl.BlockSpec((tm, LA_WIDTH), lambda i: (i, 0)),
            pl.BlockSpec((DIFF_QK, tm), lambda i: (0, i)),
            pl.BlockSpec((1, DIFF_V, tm), lambda i: (i, 0, 0)),
            pl.BlockSpec((tm, wk.shape[1]), lambda i: (i, 0)),
        ],
        out_shape=[
            jax.ShapeDtypeStruct((M, LA_WIDTH), F32),
            jax.ShapeDtypeStruct((DIFF_QK, M), BF16),
            jax.ShapeDtypeStruct((M // tm, DIFF_V, tm), BF16),
            jax.ShapeDtypeStruct((M, wk.shape[1]), BF16),
        ],
        scratch_shapes=[pltpu.VMEM((wqkv.shape[0], tm + SUBLANES, wqkv.shape[2]), F32)],
        compiler_params=pltpu.CompilerParams(
            dimension_semantics=("arbitrary",), vmem_limit_bytes=VMEM_LIMIT),
        name="in_proj",
    )(xf, nw.reshape(1, D), wqkv, wzba, wqvt, wk, convw)


def _unit_lower_inverse_minus_eye(Ls, C):
    row = lax.broadcasted_iota(jnp.int32, (C, C), 0)
    col = lax.broadcasted_iota(jnp.int32, (C, C), 1)
    xs = None
    s = 1
    while s < C:
        same_pair = (row // (2 * s)) == (col // (2 * s))
        low_left = same_pair & ((row // s) % 2 == 1) & ((col // s) % 2 == 0)
        offs = [jnp.where(low_left, L, 0.0) for L in Ls]
        if xs is None:
            xs = [-m for m in offs]
        else:
            xb = [x.astype(BF16) for x in xs]
            ys = [m + _dot(m.astype(BF16), x) for m, x in zip(offs, xb)]
            xs = [x - y - _dot(x16, y.astype(BF16)) for x, x16, y in zip(xs, xb, ys)]
        s *= 2
    return xs


def _gdn_kernel(qkv_ref, z_ref, ba_ref, avec_ref, dtvec_ref, nw_ref,
                o_ref, state, g_s, gt_s, beta_s, *, TB, C):
    t = pl.program_id(0)
    B = qkv_ref.shape[0]
    nC = TB // C

    @pl.when(t == 0)
    def _():
        state[...] = jnp.zeros(state.shape, F32)

    row = lax.broadcasted_iota(jnp.int32, (C, C), 0)
    col = lax.broadcasted_iota(jnp.int32, (C, C), 1)
    causal = row >= col
    strict = row > col
    tril = causal.astype(F32)
    ba_lane = lax.broadcasted_iota(jnp.int32, (1, BA_PAD), 1)
    is_g = (ba_lane >= LA_HEADS) & (ba_lane < 2 * LA_HEADS)

    for b in range(B):
        ba = ba_ref[b]
        beta_s[b] = jax.nn.sigmoid(ba)
        g_raw = -jnp.where(is_g, jnp.exp(avec_ref[...]), 0.0) * _softplus(ba + dtvec_ref[...])
        for c in range(nC):
            g_s[b, c * C:(c + 1) * C, :] = _dot(tril, g_raw[c * C:(c + 1) * C, :], HIGHEST)
        for j in range(TB // LANES):
            gt_s[b, :, j * LANES:(j + 1) * LANES] = g_s[b, j * LANES:(j + 1) * LANES, :].T

    probs = [(b, c, h) for b in range(B) for c in range(nC) for h in range(LA_HEADS)]
    index = {p: n for n, p in enumerate(probs)}
    Ls, intras, rhss, qgs, kds, egls = [], [], [], [], [], []
    for b, c, h in probs:
        rs = slice(c * C, (c + 1) * C)
        qn = qkv_ref[b, rs, h * LA_DK:(h + 1) * LA_DK]
        kn = qkv_ref[b, rs, LA_QK + h * LA_DK:LA_QK + (h + 1) * LA_DK]
        v = qkv_ref[b, rs, 2 * LA_QK + h * LA_DV:2 * LA_QK + (h + 1) * LA_DV]
        gb = jnp.broadcast_to(g_s[b, rs, LA_HEADS + h:LA_HEADS + h + 1], (C, LANES))
        beta = jnp.broadcast_to(beta_s[b, rs, h:h + 1], (C, LANES))
        g_row = gt_s[b, LA_HEADS + h:LA_HEADS + h + 1, rs]
        gdiff = gb[:, :C] - g_row
        decay = jnp.where(causal, jnp.exp(jnp.where(causal, gdiff, 0.0)), 0.0)
        kb = kn * beta
        kq = _dot_nt(jnp.concatenate([kb, qn], axis=0).astype(BF16), kn.astype(BF16))
        Ls.append(jnp.where(strict, kq[:C] * decay, 0.0))
        intras.append((kq[C:] * decay).astype(BF16))
        eg = jnp.exp(gb)
        rhss.append(jnp.concatenate([v * beta, kb * eg], axis=1))
        qgs.append((qn * eg).astype(BF16))
        g_last = gb[C - 1:C, :]
        kds.append((kn * jnp.exp(g_last - gb)).astype(BF16))
        egls.append(jnp.exp(g_last))

    xs = _unit_lower_inverse_minus_eye(Ls, C)
    sols = [r + _dot(x.astype(BF16), r.astype(BF16)) for x, r in zip(xs, rhss)]

    streams = [(b, h) for b in range(B) for h in range(LA_HEADS)]
    S = [state[b * LA_HEADS + h] for b, h in streams]
    for c in range(nC):
        ps = [index[(b, c, h)] for b, h in streams]
        r2 = [_dot(jnp.concatenate([sols[p][:, LA_DV:].astype(BF16), qgs[p]], axis=0),
                   S[n].astype(BF16)) for n, p in enumerate(ps)]
        vn = [(sols[p][:, :LA_DV] - r2[n][:C]).astype(BF16) for n, p in enumerate(ps)]
        S = [S[n] * egls[p] + _dot_tn(kds[p], vn[n]) for n, p in enumerate(ps)]
        rs = slice(c * C, (c + 1) * C)
        for n, ((b, h), p) in enumerate(zip(streams, ps)):
            o = r2[n][C:] + _dot(intras[p], vn[n])
            zg = _silu(z_ref[b, rs, h * LA_DV:(h + 1) * LA_DV])
            o_ref[b, rs, h * LA_DV:(h + 1) * LA_DV] = (
                _rms(o, nw_ref[...], NORM_EPS) * zg).astype(BF16)
    for n in range(len(streams)):
        state[n] = S[n]


def _gdn(la, avec, dtvec, nw, B, T, TB, C):
    nT = T // TB
    W = 2 * LA_QK + LA_V
    la3 = la.reshape(B, T, la.shape[1])
    kern = functools.partial(_gdn_kernel, TB=TB, C=C)
    out = pl.pallas_call(
        kern,
        grid=(nT,),
        in_specs=[
            pl.BlockSpec((B, TB, W), lambda t: (0, t, 0)),
            pl.BlockSpec((B, TB, LA_V), lambda t: (0, t, W // LA_V)),
            pl.BlockSpec((B, TB, BA_PAD), lambda t: (0, t, (W + LA_V) // BA_PAD)),
            pl.BlockSpec((1, BA_PAD), lambda t: (0, 0)),
            pl.BlockSpec((1, BA_PAD), lambda t: (0, 0)),
            pl.BlockSpec((1, LA_DV), lambda t: (0, 0)),
        ],
        out_specs=pl.BlockSpec((B, TB, LA_V), lambda t: (0, t, 0)),
        out_shape=jax.ShapeDtypeStruct((B, T, LA_V), BF16),
        scratch_shapes=[
            pltpu.VMEM((B * LA_HEADS, LA_DK, LA_DV), F32),
            pltpu.VMEM((B, TB, BA_PAD), F32),
            pltpu.VMEM((B, LANES, TB), F32),
            pltpu.VMEM((B, TB, BA_PAD), F32),
        ],
        compiler_params=pltpu.CompilerParams(
            dimension_semantics=("arbitrary",), vmem_limit_bytes=VMEM_LIMIT),
        name="gdn",
    )(la3, la3, la3, avec, dtvec, nw.reshape(1, LA_DV))
    return out.reshape(B * T, LA_V)


def _attn_kernel(lam_ref, tab_ref, q_ref, k_ref, vt_ref, nw_ref, o_ref, bias_s, *,
                 tq, lam_init):
    i = pl.program_id(2)
    heads = range(HEADS_PER_STEP)

    def cols(hh):
        return slice(hh * LANES, (hh + 1) * LANES)

    def vt(hh, keys):
        j, span = keys
        return jnp.concatenate([vt_ref[j + u, hh * LANES:(hh + 1) * LANES, :]
                                for u in range(span)], axis=1)

    @pl.when(i == 0)
    def _():
        for hh in heads:
            for e in range(2):
                x = jnp.broadcast_to(tab_ref[hh, e], (tq, 2 * tq))
                bias_s[hh, e] = pltpu.roll(x, 0, 1, stride=1, stride_axis=0)[:, :tq]

    q2t = []
    for hh in heads:
        qt = q_ref[hh * LANES:(hh + 1) * LANES, :].astype(F32)
        sub = lax.broadcasted_iota(jnp.int32, qt.shape, 0)
        q2t.append(jnp.concatenate([jnp.where(sub < DIFF_DH, qt, 0.0),
                                    jnp.where(sub >= DIFF_DH, qt, 0.0)], axis=1).astype(BF16))

    def scores(hh, keys):
        j, span = keys
        kj = k_ref[0, pl.ds(pl.multiple_of(j * tq, tq), span * tq), cols(hh)]
        return _dot(kj, q2t[hh])

    def biased(s, bias):
        return jnp.concatenate([s[:, :tq] + bias, s[:, tq:] + bias], axis=1)

    def update(hh, carry, s, keys):
        m, l, acc = carry
        m_new = jnp.maximum(m, jnp.max(s, axis=0, keepdims=True))
        alpha = jnp.exp2(m - m_new)
        p = jnp.exp2(s - m_new)
        l = alpha * l + jnp.sum(p, axis=0, keepdims=True)
        acc = alpha * acc + _dot(vt(hh, keys), p.astype(BF16))
        return m_new, l, acc

    def group_update(carries, blocks):
        ms = [c[0] for c in carries]
        ls = [c[1] for c in carries]
        accs = [c[2] for c in carries]
        m_news = list(ms)
        for u in range(len(blocks[0])):
            for hh in heads:
                j, thunk, off = blocks[hh][u]
                s = thunk()
                ref = ms[hh] if off is None else ms[hh] + off
                p = jnp.exp2(s - ref)
                ls[hh] = ls[hh] + jnp.sum(p, axis=0, keepdims=True)
                accs[hh] = accs[hh] + _dot(vt(hh, j), p.astype(BF16))
                smax = jnp.max(s, axis=0, keepdims=True)
                m_news[hh] = jnp.maximum(m_news[hh], smax if off is None else smax - off)
        fast = []
        lead = None
        for hh in heads:
            alpha = jnp.exp2(ms[hh] - m_news[hh])
            fast.append((m_news[hh], alpha * ls[hh], alpha * accs[hh]))
            lead_h = jnp.max(m_news[hh] - ms[hh])
            lead = lead_h if lead is None else jnp.maximum(lead, lead_h)

        def redo():
            out = []
            for hh in heads:
                c = carries[hh]
                for j, thunk, off in blocks[hh]:
                    s = thunk()
                    c = update(hh, c, s if off is None else s - off, j)
                out.append(c)
            return tuple(out)

        return lax.cond(lead <= MAX_LEAD, lambda: tuple(fast), redo)

    j_near = jnp.maximum(i - 1, 0)
    edge = LANES
    assert edge >= MAX_DISTANCE and tq >= 2 * edge

    def near_scores(hh):
        s = scores(hh, (j_near, 1))
        corner = bias_s[hh, 1, tq - edge:tq, 0:edge]
        bot = s[tq - edge:]
        bot = jnp.concatenate([bot[:, :edge] + corner, bot[:, edge:tq],
                               bot[:, tq:tq + edge] + corner, bot[:, tq + edge:]], axis=1)
        return jnp.concatenate([s[:tq - edge], bot], axis=0)

    no_near = jnp.where(i == 0, -NEG_BIG, 0.0)
    carries, tail = [], []
    for hh in heads:
        s_diag = biased(scores(hh, (i, 1)), bias_s[hh, 0])
        m0 = jnp.max(s_diag[:DIFF_DH], axis=0, keepdims=True)
        carries.append((m0, jnp.zeros((1, 2 * tq), F32), jnp.zeros((DIFF_DV, 2 * tq), F32)))
        tail.append([((i, 1), functools.partial(lambda s: s, s_diag), None),
                     ((j_near, 1), functools.partial(near_scores, hh), no_near)])
    carries = group_update(tuple(carries), tail)
    n_far = jnp.maximum(i - 1, 0)
    done = 0
    for width in FAR_UNROLLS:
        def body(g, c, width=width, done=done):
            span = min(width, FAR_SPAN)
            ranges = [(done + width * g + u, span) for u in range(0, width, span)]
            return group_update(c, [[(r, functools.partial(scores, hh, r), None) for r in ranges]
                                    for hh in heads])
        n_groups = (n_far - done) // width
        carries = lax.fori_loop(0, n_groups, body, carries)
        done = done + n_groups * width

    lam = (jnp.exp(jnp.sum(lam_ref[0:1, :] * lam_ref[1:2, :], axis=-1, keepdims=True))
           - jnp.exp(jnp.sum(lam_ref[2:3, :] * lam_ref[3:4, :], axis=-1, keepdims=True)) + lam_init)
    for hh in heads:
        m, l, acc = carries[hh]
        inv_l = 1.0 / l
        ot = acc[:, :tq] * inv_l[:, :tq] - lam * (acc[:, tq:] * inv_l[:, tq:])
        gain = jnp.concatenate([nw_ref[...]] * (tq // LANES), axis=1)
        ms = jnp.mean(ot * ot, axis=0, keepdims=True)
        o_ref[hh * LANES:(hh + 1) * LANES, :] = (
            ot * lax.rsqrt(ms + DIFF_NORM_EPS) * gain * (1.0 - lam_init)).astype(BF16)


def _attn(qt, vt, k, lams, bias_tab, nw, B, T, tq, lam_init):
    assert T % tq == 0 and DIFF_HEADS % HEADS_PER_STEP == 0 and vt.shape[2] == tq
    nq = T // tq
    hw = HEADS_PER_STEP * LANES
    k3 = k.reshape(B, T, k.shape[1])
    kern = functools.partial(_attn_kernel, tq=tq, lam_init=lam_init)
    return pl.pallas_call(
        kern,
        grid=(B, DIFF_HEADS // HEADS_PER_STEP, nq),
        in_specs=[
            pl.BlockSpec((4, DIFF_DH), lambda b, g, i: (0, 0)),
            pl.BlockSpec((HEADS_PER_STEP, 2, 1, 2 * tq), lambda b, g, i: (g, 0, 0, 0)),
            pl.BlockSpec((hw, tq), lambda b, g, i: (g, b * nq + i)),
            pl.BlockSpec((1, T, hw), lambda b, g, i: (b, 0, g)),
            pl.BlockSpec((nq, hw, tq), lambda b, g, i: (b, g, 0)),
            pl.BlockSpec((DIFF_DV, LANES), lambda b, g, i: (0, 0)),
        ],
        out_specs=pl.BlockSpec((hw, tq), lambda b, g, i: (g, b * nq + i)),
        out_shape=jax.ShapeDtypeStruct((DIFF_V, B * T), BF16),
        scratch_shapes=[
            pltpu.VMEM((HEADS_PER_STEP, 2, tq, tq), F32),
        ],
        compiler_params=pltpu.CompilerParams(
            dimension_semantics=("parallel", "parallel", "arbitrary"),
            vmem_limit_bytes=VMEM_LIMIT),
        name="diff_attn",
    )(lams, bias_tab, qt, k3, vt, jnp.broadcast_to(nw.astype(F32)[:, None], (DIFF_DV, LANES)))


def _t5_bucket(rel):
    n = jnp.maximum(rel, 0)
    max_exact = NUM_BUCKETS // 2
    nf = jnp.maximum(n, 1).astype(F32)
    large = max_exact + (jnp.log(nf / max_exact) / math.log(MAX_DISTANCE / max_exact)
                         * (NUM_BUCKETS - max_exact)).astype(jnp.int32)
    large = jnp.minimum(large, NUM_BUCKETS - 1)
    return jnp.where(n < max_exact, n, large)


def _bias_tables(rel_bias, tq):
    assert tq + 1 >= MAX_DISTANCE
    m = jnp.arange(2 * tq)
    far = rel_bias[NUM_BUCKETS - 1].astype(F32)
    tabs = []
    for d in (0, tq):
        rel = jnp.where(m < tq, d + m, d + m - 2 * tq)
        b = (rel_bias[_t5_bucket(rel)].astype(F32) - far) * math.log2(math.e)
        tabs.append(jnp.where((rel >= 0)[:, None], b, NEG_BIG).T)
    return jnp.stack(tabs, axis=1)[:, :, None, :]


def _ffn_kernel(x_ref, ola_ref, od_ref, wout_ref, nw_ref, wgu_ref, wd_ref, fnw_ref,
                out_ref, *, final_norm):
    d_ff = wd_ref.shape[0]
    y = (x_ref[...] + _dot(ola_ref[...], wout_ref[0:LA_V, :])
         + _dot_tn(od_ref[...], wout_ref[LA_V:LA_V + DIFF_V, :]))
    h = _rms(y, nw_ref[...], NORM_EPS).astype(BF16)
    gate = _dot(h, wgu_ref[:, 0:d_ff])
    up = _dot(h, wgu_ref[:, d_ff:2 * d_ff])
    y = y + _dot((_silu(gate) * up).astype(BF16), wd_ref[...])
    if final_norm:
        y = _rms(y, fnw_ref[...], NORM_EPS)
    out_ref[...] = y


def _ffn(xf, o_la, o_d, wout, nw, wgu, wdn, fnw, layer, tm, final_norm):
    M, D = xf.shape
    kern = functools.partial(_ffn_kernel, final_norm=final_norm)
    resident = dict(pipeline_mode=pl.Buffered(1))

    def layer_slab(w):
        return pl.BlockSpec((None,) + w.shape[1:], lambda i: (layer, 0, 0), **resident)

    return pl.pallas_call(
        kern,
        grid=(M // tm,),
        in_specs=[
            pl.BlockSpec((tm, D), lambda i: (i, 0)),
            pl.BlockSpec((tm, LA_V), lambda i: (i, 0)),
            pl.BlockSpec((DIFF_V, tm), lambda i: (0, i)),
            layer_slab(wout),
            pl.BlockSpec((1, D), lambda i: (0, 0)),
            layer_slab(wgu),
            layer_slab(wdn),
            pl.BlockSpec((1, D), lambda i: (0, 0)),
        ],
        out_specs=pl.BlockSpec((tm, D), lambda i: (i, 0)),
        out_shape=jax.ShapeDtypeStruct((M, D), F32),
        compiler_params=pltpu.CompilerParams(
            dimension_semantics=("parallel",), vmem_limit_bytes=VMEM_LIMIT),
        name="out_proj_ffn",
    )(xf, o_la, o_d, wout, nw.reshape(1, D), wgu, wdn, fnw.reshape(1, D))


def _pick(n, pref):
    return pref if n % pref == 0 else n


def kernel(x, attn_norm_w, w_in, conv_w, a_log, dt_bias, la_norm_w, lambda_q1, lambda_k1,
           lambda_q2, lambda_k2, diff_norm_w, rel_bias, w_out, ffn_norm_w, w_gate_up,
           w_down, final_norm_w):
    B, T, D = x.shape
    depth = w_in.shape[0]
    M = B * T
    tm_proj = _pick(M, 512)
    tm_ffn = _pick(M, 512)
    tb = _pick(T, 256)
    chunk = 128
    tq = _pick(T, 512)

    n_la = 2 * LA_QK + 2 * LA_V
    bias_tab = _bias_tables(rel_bias, tq)
    lane_pad = BA_PAD - 2 * LA_HEADS

    w_in_bf = lax.optimization_barrier(w_in.astype(BF16))
    w_out_bf = w_out.astype(BF16)
    w_gate_up_bf = w_gate_up.astype(BF16)
    w_down_bf = w_down.astype(BF16)
    n_qkv = 2 * LA_QK + LA_V

    xf = x.reshape(M, D)
    for l in range(depth):
        w = w_in_bf[l]
        wqkv = jnp.stack([w[:, s * CONV_SLAB:(s + 1) * CONV_SLAB]
                          for s in range(n_qkv // CONV_SLAB)])
        wzba = jnp.concatenate([w[:, n_qkv:n_la + 2 * LA_HEADS],
                                jnp.zeros((D, lane_pad), w.dtype)], axis=1)
        n_d = n_la + 2 * LA_HEADS
        wqvt = jnp.concatenate([w[:, n_d:n_d + DIFF_QK], w[:, n_d + 2 * DIFF_QK:]], axis=1).T
        wk = w[:, n_d + DIFF_QK:n_d + 2 * DIFF_QK]
        assert tm_proj == tq
        la, qt, vt, k = _in_proj(xf, attn_norm_w[l], wqkv, wzba, wqvt, wk, conv_w[l], tm_proj, T)

        avec = jnp.concatenate([jnp.zeros((LA_HEADS,), F32), a_log[l].astype(F32),
                                jnp.zeros((lane_pad,), F32)]).reshape(1, BA_PAD)
        dtvec = jnp.concatenate([jnp.zeros((LA_HEADS,), F32), dt_bias[l].astype(F32),
                                 jnp.zeros((lane_pad,), F32)]).reshape(1, BA_PAD)
        o_la = _gdn(la, avec, dtvec, la_norm_w[l], B, T, tb, chunk)

        lam_init = 0.8 - 0.6 * math.exp(-0.3 * l)
        lams = jnp.stack([lambda_q1[l], lambda_k1[l], lambda_q2[l], lambda_k2[l]]).astype(F32)
        o_d = _attn(qt, vt, k, lams, bias_tab, diff_norm_w[l], B, T, tq, lam_init)

        xf = _ffn(xf, o_la, o_d, w_out_bf, ffn_norm_w[l], w_gate_up_bf, w_down_bf, final_norm_w,
                  l, tm_ffn, final_norm=(l == depth - 1))
    return xf.reshape(B, T, D)
```

```python
import functools
import math

import jax
import jax.numpy as jnp
from jax import lax
from jax.experimental import pallas as pl
from jax.experimental.pallas import tpu as pltpu

LA_HEADS = 4
LA_DK = 128
LA_DV = 128
LA_QK = LA_HEADS * LA_DK
LA_V = LA_HEADS * LA_DV
CONV_K = 4
DIFF_HEADS = 4
DIFF_DH = 64
DIFF_DV = 2 * DIFF_DH
DIFF_QK = DIFF_HEADS * 2 * DIFF_DH
DIFF_V = DIFF_HEADS * DIFF_DV
NUM_BUCKETS = 32
MAX_DISTANCE = 128
NORM_EPS = 1e-6
DIFF_NORM_EPS = 1e-5
L2_EPS = 1e-6

LANES = 128
SUBLANES = 8
BA_PAD = LANES
LA_WIDTH = 2 * LA_QK + 2 * LA_V + BA_PAD
NEG_BIG = -1e30
FAR_UNROLLS = (4, 2, 1)
FAR_SPAN = 4
HEADS_PER_STEP = 2
CONV_SLAB = 256
MAX_LEAD = 40.0

VMEM_LIMIT = 52 * 1024 * 1024

F32 = jnp.float32
BF16 = jnp.bfloat16
HIGHEST = lax.Precision.HIGHEST


def _dot(a, b, precision=None):
    return jnp.dot(a, b, preferred_element_type=F32, precision=precision)


def _dot_nt(a, b, precision=None):
    return lax.dot_general(a, b, (((1,), (1,)), ((), ())),
                           preferred_element_type=F32, precision=precision)


def _dot_tn(a, b, precision=None):
    return lax.dot_general(a, b, (((0,), (0,)), ((), ())),
                           preferred_element_type=F32, precision=precision)


def _rms(x, w, eps):
    return x * lax.rsqrt(jnp.mean(x * x, axis=-1, keepdims=True) + eps) * w


def _silu(x):
    h = 0.5 * x
    return h + h * jnp.tanh(h)


def _softplus(x):
    return jnp.maximum(x, 0.0) + jnp.log1p(jnp.exp(-jnp.abs(x)))


def _in_proj_kernel(x_ref, nw_ref, wqkv_ref, wzba_ref, wqvt_ref, wk_ref, convw_ref,
                    la_ref, qt_ref, vt_ref, k_ref, xbuf, *, tiles_per_seq):
    i = pl.program_id(0)
    tm = x_ref.shape[0]
    W = 2 * LA_QK + LA_V

    n_slabs, _, ws = xbuf.shape

    @pl.when(i % tiles_per_seq == 0)
    def _():
        xbuf[:, 0:SUBLANES, :] = jnp.zeros((n_slabs, SUBLANES, ws), F32)

    h = _rms(x_ref[...], nw_ref[...], NORM_EPS).astype(BF16)
    for s in range(n_slabs):
        xbuf[s, SUBLANES:SUBLANES + tm, :] = _dot(h, wqkv_ref[s])
        c0 = s * ws
        conv = convw_ref[CONV_K - 1:CONV_K, c0:c0 + ws] * xbuf[s, SUBLANES:SUBLANES + tm, :]
        for k in range(CONV_K - 1):
            off = SUBLANES - (CONV_K - 1) + k
            conv = conv + convw_ref[k:k + 1, c0:c0 + ws] * xbuf[s, off:off + tm, :]
        xbuf[s, 0:SUBLANES, :] = xbuf[s, tm:tm + SUBLANES, :]
        y = _silu(conv)
        if c0 >= 2 * LA_QK:
            la_ref[:, c0:c0 + ws] = y
            continue
        scale = LA_DK ** -0.5 if c0 < LA_QK else 1.0
        for hd in range(ws // LA_DK):
            yh = y[:, hd * LA_DK:(hd + 1) * LA_DK]
            yn = yh * lax.rsqrt(jnp.sum(yh * yh, -1, keepdims=True) + L2_EPS)
            la_ref[:, c0 + hd * LA_DK:c0 + (hd + 1) * LA_DK] = yn * scale if scale != 1.0 else yn
    qvt = _dot_nt(wqvt_ref[...], h)
    qt_ref[...] = (qvt[:DIFF_QK] * (DIFF_DH ** -0.5 * math.log2(math.e))).astype(BF16)
    vt_ref[0] = qvt[DIFF_QK:].astype(BF16)
    k_ref[...] = _dot(h, wk_ref[...]).astype(BF16)
    la_ref[:, W:] = _dot(h, wzba_ref[...])


def _in_proj(xf, nw, wqkv, wzba, wqvt, wk, convw, tm, T):
    M, D = xf.shape
    assert T % tm == 0
    kern = functools.partial(_in_proj_kernel, tiles_per_seq=T // tm)
    return pl.pallas_call(
        kern,
        grid=(M // tm,),
        in_specs=[
            pl.BlockSpec((tm, D), lambda i: (i, 0)),
            pl.BlockSpec((1, D), lambda i: (0, 0)),
            pl.BlockSpec(wqkv.shape, lambda i: (0, 0, 0)),
            pl.BlockSpec(wzba.shape, lambda i: (0, 0)),
            pl.BlockSpec(wqvt.shape, lambda i: (0, 0)),
            pl.BlockSpec(wk.shape, lambda i: (0, 0)),
            pl.BlockSpec(convw.shape, lambda i: (0, 0)),
        ],
        out_specs=[
            pl.BlockSpec((tm, LA_WIDTH), lambda i: (i, 0)),
            pl.BlockSpec((DIFF_QK, tm), lambda i: (0, i)),
            pl.BlockSpec((1, DIFF_V, tm), lambda i: (i, 0, 0)),
            pl.BlockSpec((tm, wk.shape[1]), lambda i: (i, 0)),
        ],
        out_shape=[
            jax.ShapeDtypeStruct((M, LA_WIDTH), F32),
            jax.ShapeDtypeStruct((DIFF_QK, M), BF16),
            jax.ShapeDtypeStruct((M // tm, DIFF_V, tm), BF16),
            jax.ShapeDtypeStruct((M, wk.shape[1]), BF16),
        ],
        scratch_shapes=[pltpu.VMEM((wqkv.shape[0], tm + SUBLANES, wqkv.shape[2]), F32)],
        compiler_params=pltpu.CompilerParams(
            dimension_semantics=("arbitrary",), vmem_limit_bytes=VMEM_LIMIT),
        name="in_proj",
    )(xf, nw.reshape(1, D), wqkv, wzba, wqvt, wk, convw)


def _unit_lower_inverse_minus_eye(Ls, C):
    row = lax.broadcasted_iota(jnp.int32, (C, C), 0)
    col = lax.broadcasted_iota(jnp.int32, (C, C), 1)
    xs = None
    s = 1
    while s < C:
        same_pair = (row // (2 * s)) == (col // (2 * s))
        low_left = same_pair & ((row // s) % 2 == 1) & ((col // s) % 2 == 0)
        offs = [jnp.where(low_left, L, 0.0) for L in Ls]
        if xs is None:
            xs = [-m for m in offs]
        else:
            xb = [x.astype(BF16) for x in xs]
            ys = [m + _dot(m.astype(BF16), x) for m, x in zip(offs, xb)]
            xs = [x - y - _dot(x16, y.astype(BF16)) for x, x16, y in zip(xs, xb, ys)]
        s *= 2
    return xs


def _gdn_kernel(qkv_ref, z_ref, ba_ref, avec_ref, dtvec_ref, nw_ref,
                o_ref, state, g_s, gt_s, beta_s, *, TB, C):
    t = pl.program_id(0)
    B = qkv_ref.shape[0]
    nC = TB // C

    @pl.when(t == 0)
    def _():
        state[...] = jnp.zeros(state.shape, F32)

    row = lax.broadcasted_iota(jnp.int32, (C, C), 0)
    col = lax.broadcasted_iota(jnp.int32, (C, C), 1)
    causal = row >= col
    strict = row > col
    tril = causal.astype(F32)
    ba_lane = lax.broadcasted_iota(jnp.int32, (1, BA_PAD), 1)
    is_g = (ba_lane >= LA_HEADS) & (ba_lane < 2 * LA_HEADS)

    for b in range(B):
        ba = ba_ref[b]
        beta_s[b] = jax.nn.sigmoid(ba)
        g_raw = -jnp.where(is_g, jnp.exp(avec_ref[...]), 0.0) * _softplus(ba + dtvec_ref[...])
        for c in range(nC):
            g_s[b, c * C:(c + 1) * C, :] = _dot(tril, g_raw[c * C:(c + 1) * C, :], HIGHEST)
        for j in range(TB // LANES):
            gt_s[b, :, j * LANES:(j + 1) * LANES] = g_s[b, j * LANES:(j + 1) * LANES, :].T

    probs = [(b, c, h) for b in range(B) for c in range(nC) for h in range(LA_HEADS)]
    index = {p: n for n, p in enumerate(probs)}
    Ls, intras, rhss, qgs, kds, egls = [], [], [], [], [], []
    for b, c, h in probs:
        rs = slice(c * C, (c + 1) * C)
        qn = qkv_ref[b, rs, h * LA_DK:(h + 1) * LA_DK]
        kn = qkv_ref[b, rs, LA_QK + h * LA_DK:LA_QK + (h + 1) * LA_DK]
        v = qkv_ref[b, rs, 2 * LA_QK + h * LA_DV:2 * LA_QK + (h + 1) * LA_DV]
        gb = jnp.broadcast_to(g_s[b, rs, LA_HEADS + h:LA_HEADS + h + 1], (C, LANES))
        beta = jnp.broadcast_to(beta_s[b, rs, h:h + 1], (C, LANES))
        g_row = gt_s[b, LA_HEADS + h:LA_HEADS + h + 1, rs]
        gdiff = gb[:, :C] - g_row
        decay = jnp.where(causal, jnp.exp(jnp.where(causal, gdiff, 0.0)), 0.0)
        kb = kn * beta
        kq = _dot_nt(jnp.concatenate([kb, qn], axis=0).astype(BF16), kn.astype(BF16))
        Ls.append(jnp.where(strict, kq[:C] * decay, 0.0))
        intras.append((kq[C:] * decay).astype(BF16))
        eg = jnp.exp(gb)
        rhss.append(jnp.concatenate([v * beta, kb * eg], axis=1))
        qgs.append((qn * eg).astype(BF16))
        g_last = gb[C - 1:C, :]
        kds.append((kn * jnp.exp(g_last - gb)).astype(BF16))
        egls.append(jnp.exp(g_last))

    xs = _unit_lower_inverse_minus_eye(Ls, C)
    sols = [r + _dot(x.astype(BF16), r.astype(BF16)) for x, r in zip(xs, rhss)]

    streams = [(b, h) for b in range(B) for h in range(LA_HEADS)]
    S = [state[b * LA_HEADS + h] for b, h in streams]
    for c in range(nC):
        ps = [index[(b, c, h)] for b, h in streams]
        r2 = [_dot(jnp.concatenate([sols[p][:, LA_DV:].astype(BF16), qgs[p]], axis=0),
                   S[n].astype(BF16)) for n, p in enumerate(ps)]
        vn = [(sols[p][:, :LA_DV] - r2[n][:C]).astype(BF16) for n, p in enumerate(ps)]
        S = [S[n] * egls[p] + _dot_tn(kds[p], vn[n]) for n, p in enumerate(ps)]
        rs = slice(c * C, (c + 1) * C)
        for n, ((b, h), p) in enumerate(zip(streams, ps)):
            o = r2[n][C:] + _dot(intras[p], vn[n])
            zg = _silu(z_ref[b, rs, h * LA_DV:(h + 1) * LA_DV])
            o_ref[b, rs, h * LA_DV:(h + 1) * LA_DV] = (
                _rms(o, nw_ref[...], NORM_EPS) * zg).astype(BF16)
    for n in range(len(streams)):
        state[n] = S[n]


def _gdn(la, avec, dtvec, nw, B, T, TB, C):
    nT = T // TB
    W = 2 * LA_QK + LA_V
    la3 = la.reshape(B, T, la.shape[1])
    kern = functools.partial(_gdn_kernel, TB=TB, C=C)
    out = pl.pallas_call(
        kern,
        grid=(nT,),
        in_specs=[
            pl.BlockSpec((B, TB, W), lambda t: (0, t, 0)),
            pl.BlockSpec((B, TB, LA_V), lambda t: (0, t, W // LA_V)),
            pl.BlockSpec((B, TB, BA_PAD), lambda t: (0, t, (W + LA_V) // BA_PAD)),
            pl.BlockSpec((1, BA_PAD), lambda t: (0, 0)),
            pl.BlockSpec((1, BA_PAD), lambda t: (0, 0)),
            pl.BlockSpec((1, LA_DV), lambda t: (0, 0)),
        ],
        out_specs=pl.BlockSpec((B, TB, LA_V), lambda t: (0, t, 0)),
        out_shape=jax.ShapeDtypeStruct((B, T, LA_V), BF16),
        scratch_shapes=[
            pltpu.VMEM((B * LA_HEADS, LA_DK, LA_DV), F32),
            pltpu.VMEM((B, TB, BA_PAD), F32),
            pltpu.VMEM((B, LANES, TB), F32),
            pltpu.VMEM((B, TB, BA_PAD), F32),
        ],
        compiler_params=pltpu.CompilerParams(
            dimension_semantics=("arbitrary",), vmem_limit_bytes=VMEM_LIMIT),
        name="gdn",
    )(la3, la3, la3, avec, dtvec, nw.reshape(1, LA_DV))
    return out.reshape(B * T, LA_V)


def _attn_kernel(lam_ref, tab_ref, q_ref, k_ref, vt_ref, nw_ref, o_ref, bias_s, *,
                 tq, lam_init):
    i = pl.program_id(2)
    heads = range(HEADS_PER_STEP)

    def cols(hh):
        return slice(hh * LANES, (hh + 1) * LANES)

    def vt(hh, keys):
        j, span = keys
        return jnp.concatenate([vt_ref[j + u, hh * LANES:(hh + 1) * LANES, :]
                                for u in range(span)], axis=1)

    @pl.when(i == 0)
    def _():
        for hh in heads:
            for e in range(2):
                x = jnp.broadcast_to(tab_ref[hh, e], (tq, 2 * tq))
                bias_s[hh, e] = pltpu.roll(x, 0, 1, stride=1, stride_axis=0)[:, :tq]

    q2t = []
    for hh in heads:
        qt = q_ref[hh * LANES:(hh + 1) * LANES, :].astype(F32)
        sub = lax.broadcasted_iota(jnp.int32, qt.shape, 0)
        q2t.append(jnp.concatenate([jnp.where(sub < DIFF_DH, qt, 0.0),
                                    jnp.where(sub >= DIFF_DH, qt, 0.0)], axis=1).astype(BF16))

    def scores(hh, keys):
        j, span = keys
        kj = k_ref[0, pl.ds(pl.multiple_of(j * tq, tq), span * tq), cols(hh)]
        return _dot(kj, q2t[hh])

    def biased(s, bias):
        return jnp.concatenate([s[:, :tq] + bias, s[:, tq:] + bias], axis=1)

    def update(hh, carry, s, keys):
        m, l, acc = carry
        m_new = jnp.maximum(m, jnp.max(s, axis=0, keepdims=True))
        alpha = jnp.exp2(m - m_new)
        p = jnp.exp2(s - m_new)
        l = alpha * l + jnp.sum(p, axis=0, keepdims=True)
        acc = alpha * acc + _dot(vt(hh, keys), p.astype(BF16))
        return m_new, l, acc

    def group_update(carries, blocks):
        ms = [c[0] for c in carries]
        ls = [c[1] for c in carries]
        accs = [c[2] for c in carries]
        m_news = list(ms)
        for u in range(len(blocks[0])):
            for hh in heads:
                j, thunk, off = blocks[hh][u]
                s = thunk()
                ref = ms[hh] if off is None else ms[hh] + off
                p = jnp.exp2(s - ref)
                ls[hh] = ls[hh] + jnp.sum(p, axis=0, keepdims=True)
                accs[hh] = accs[hh] + _dot(vt(hh, j), p.astype(BF16))
                smax = jnp.max(s, axis=0, keepdims=True)
                m_news[hh] = jnp.maximum(m_news[hh], smax if off is None else smax - off)
        fast = []
        lead = None
        for hh in heads:
            alpha = jnp.exp2(ms[hh] - m_news[hh])
            fast.append((m_news[hh], alpha * ls[hh], alpha * accs[hh]))
            lead_h = jnp.max(m_news[hh] - ms[hh])
            lead = lead_h if lead is None else jnp.maximum(lead, lead_h)

        def redo():
            out = []
            for hh in heads:
                c = carries[hh]
                for j, thunk, off in blocks[hh]:
                    s = thunk()
                    c = update(hh, c, s if off is None else s - off, j)
                out.append(c)
            return tuple(out)

        return lax.cond(lead <= MAX_LEAD, lambda: tuple(fast), redo)

    j_near = jnp.maximum(i - 1, 0)
    edge = LANES
    assert edge >= MAX_DISTANCE and tq >= 2 * edge

    def near_scores(hh):
        s = scores(hh, (j_near, 1))
        corner = bias_s[hh, 1, tq - edge:tq, 0:edge]
        bot = s[tq - edge:]
        bot = jnp.concatenate([bot[:, :edge] + corner, bot[:, edge:tq],
                               bot[:, tq:tq + edge] + corner, bot[:, tq + edge:]], axis=1)
        return jnp.concatenate([s[:tq - edge], bot], axis=0)

    no_near = jnp.where(i == 0, -NEG_BIG, 0.0)
    carries, tail = [], []
    for hh in heads:
        s_diag = biased(scores(hh, (i, 1)), bias_s[hh, 0])
        m0 = jnp.max(s_diag[:DIFF_DH], axis=0, keepdims=True)
        carries.append((m0, jnp.zeros((1, 2 * tq), F32), jnp.zeros((DIFF_DV, 2 * tq), F32)))
        tail.append([((i, 1), functools.partial(lambda s: s, s_diag), None),
                     ((j_near, 1), functools.partial(near_scores, hh), no_near)])
    carries = group_update(tuple(carries), tail)
    n_far = jnp.maximum(i - 1, 0)
    done = 0
    for width in FAR_UNROLLS:
        def body(g, c, width=width, done=done):
            span = min(width, FAR_SPAN)
            ranges = [(done + width * g + u, span) for u in range(0, width, span)]
            return group_update(c, [[(r, functools.partial(scores, hh, r), None) for r in ranges]
                                    for hh in heads])
        n_groups = (n_far - done) // width
        carries = lax.fori_loop(0, n_groups, body, carries)
        done = done + n_groups * width

    lam = (jnp.exp(jnp.sum(lam_ref[0:1, :] * lam_ref[1:2, :], axis=-1, keepdims=True))
           - jnp.exp(jnp.sum(lam_ref[2:3, :] * lam_ref[3:4, :], axis=-1, keepdims=True)) + lam_init)
    for hh in heads:
        m, l, acc = carries[hh]
        inv_l = 1.0 / l
        ot = acc[:, :tq] * inv_l[:, :tq] - lam * (acc[:, tq:] * inv_l[:, tq:])
        gain = jnp.concatenate([nw_ref[...]] * (tq // LANES), axis=1)
        ms = jnp.mean(ot * ot, axis=0, keepdims=True)
        o_ref[hh * LANES:(hh + 1) * LANES, :] = (
            ot * lax.rsqrt(ms + DIFF_NORM_EPS) * gain * (1.0 - lam_init)).astype(BF16)


def _attn(qt, vt, k, lams, bias_tab, nw, B, T, tq, lam_init):
    assert T % tq == 0 and DIFF_HEADS % HEADS_PER_STEP == 0 and vt.shape[2] == tq
    nq = T // tq
    hw = HEADS_PER_STEP * LANES
    k3 = k.reshape(B, T, k.shape[1])
    kern = functools.partial(_attn_kernel, tq=tq, lam_init=lam_init)
    return pl.pallas_call(
        kern,
        grid=(B, DIFF_HEADS // HEADS_PER_STEP, nq),
        in_specs=[
            pl.BlockSpec((4, DIFF_DH), lambda b, g, i: (0, 0)),
            pl.BlockSpec((HEADS_PER_STEP, 2, 1, 2 * tq), lambda b, g, i: (g, 0, 0, 0)),
            pl.BlockSpec((hw, tq), lambda b, g, i: (g, b * nq + i)),
            pl.BlockSpec((1, T, hw), lambda b, g, i: (b, 0, g)),
            pl.BlockSpec((nq, hw, tq), lambda b, g, i: (b, g, 0)),
            pl.BlockSpec((DIFF_DV, LANES), lambda b, g, i: (0, 0)),
        ],
        out_specs=pl.BlockSpec((hw, tq), lambda b, g, i: (g, b * nq + i)),
        out_shape=jax.ShapeDtypeStruct((DIFF_V, B * T), BF16),
        scratch_shapes=[
            pltpu.VMEM((HEADS_PER_STEP, 2, tq, tq), F32),
        ],
        compiler_params=pltpu.CompilerParams(
            dimension_semantics=("parallel", "parallel", "arbitrary"),
            vmem_limit_bytes=VMEM_LIMIT),
        name="diff_attn",
    )(lams, bias_tab, qt, k3, vt, jnp.broadcast_to(nw.astype(F32)[:, None], (DIFF_DV, LANES)))


def _t5_bucket(rel):
    n = jnp.maximum(rel, 0)
    max_exact = NUM_BUCKETS // 2
    nf = jnp.maximum(n, 1).astype(F32)
    large = max_exact + (jnp.log(nf / max_exact) / math.log(MAX_DISTANCE / max_exact)
                         * (NUM_BUCKETS - max_exact)).astype(jnp.int32)
    large = jnp.minimum(large, NUM_BUCKETS - 1)
    return jnp.where(n < max_exact, n, large)


def _bias_tables(rel_bias, tq):
    assert tq + 1 >= MAX_DISTANCE
    m = jnp.arange(2 * tq)
    far = rel_bias[NUM_BUCKETS - 1].astype(F32)
    tabs = []
    for d in (0, tq):
        rel = jnp.where(m < tq, d + m, d + m - 2 * tq)
        b = (rel_bias[_t5_bucket(rel)].astype(F32) - far) * math.log2(math.e)
        tabs.append(jnp.where((rel >= 0)[:, None], b, NEG_BIG).T)
    return jnp.stack(tabs, axis=1)[:, :, None, :]


def _ffn_kernel(x_ref, ola_ref, od_ref, wout_ref, nw_ref, wgu_ref, wd_ref, fnw_ref,
                out_ref, *, final_norm):
    d_ff = wd_ref.shape[0]
    y = (x_ref[...] + _dot(ola_ref[...], wout_ref[0:LA_V, :])
         + _dot_tn(od_ref[...], wout_ref[LA_V:LA_V + DIFF_V, :]))
    h = _rms(y, nw_ref[...], NORM_EPS).astype(BF16)
    gate = _dot(h, wgu_ref[:, 0:d_ff])
    up = _dot(h, wgu_ref[:, d_ff:2 * d_ff])
    y = y + _dot((_silu(gate) * up).astype(BF16), wd_ref[...])
    if final_norm:
        y = _rms(y, fnw_ref[...], NORM_EPS)
    out_ref[...] = y


def _ffn(xf, o_la, o_d, wout, nw, wgu, wdn, fnw, layer, tm, final_norm):
    M, D = xf.shape
    kern = functools.partial(_ffn_kernel, final_norm=final_norm)
    resident = dict(pipeline_mode=pl.Buffered(1))

    def layer_slab(w):
        return pl.BlockSpec((None,) + w.shape[1:], lambda i: (layer, 0, 0), **resident)

    return pl.pallas_call(
        kern,
        grid=(M // tm,),
        in_specs=[
            pl.BlockSpec((tm, D), lambda i: (i, 0)),
            pl.BlockSpec((tm, LA_V), lambda i: (i, 0)),
            pl.BlockSpec((DIFF_V, tm), lambda i: (0, i)),
            layer_slab(wout),
            pl.BlockSpec((1, D), lambda i: (0, 0)),
            layer_slab(wgu),
            layer_slab(wdn),
            pl.BlockSpec((1, D), lambda i: (0, 0)),
        ],
        out_specs=pl.BlockSpec((tm, D), lambda i: (i, 0)),
        out_shape=jax.ShapeDtypeStruct((M, D), F32),
        compiler_params=pltpu.CompilerParams(
            dimension_semantics=("parallel",), vmem_limit_bytes=VMEM_LIMIT),
        name="out_proj_ffn",
    )(xf, o_la, o_d, wout, nw.reshape(1, D), wgu, wdn, fnw.reshape(1, D))


def _pick(n, pref):
    return pref if n % pref == 0 else n


def kernel(x, attn_norm_w, w_in, conv_w, a_log, dt_bias, la_norm_w, lambda_q1, lambda_k1,
           lambda_q2, lambda_k2, diff_norm_w, rel_bias, w_out, ffn_norm_w, w_gate_up,
           w_down, final_norm_w):
    B, T, D = x.shape
    depth = w_in.shape[0]
    M = B * T
    tm_proj = _pick(M, 512)
    tm_ffn = _pick(M, 512)
    tb = _pick(T, 256)
    chunk = 128
    tq = _pick(T, 512)

    n_la = 2 * LA_QK + 2 * LA_V
    bias_tab = _bias_tables(rel_bias, tq)
    lane_pad = BA_PAD - 2 * LA_HEADS

    w_in_bf = lax.optimization_barrier(w_in.astype(BF16))
    w_out_bf = w_out.astype(BF16)
    w_gate_up_bf = w_gate_up.astype(BF16)
    w_down_bf = w_down.astype(BF16)
    n_qkv = 2 * LA_QK + LA_V

    xf = x.reshape(M, D)
    for l in range(depth):
        w = w_in_bf[l]
        wqkv = jnp.stack([w[:, s * CONV_SLAB:(s + 1) * CONV_SLAB]
                          for s in range(n_qkv // CONV_SLAB)])
        wzba = jnp.concatenate([w[:, n_qkv:n_la + 2 * LA_HEADS],
                                jnp.zeros((D, lane_pad), w.dtype)], axis=1)
        n_d = n_la + 2 * LA_HEADS
        wqvt = jnp.concatenate([w[:, n_d:n_d + DIFF_QK], w[:, n_d + 2 * DIFF_QK:]], axis=1).T
        wk = w[:, n_d + DIFF_QK:n_d + 2 * DIFF_QK]
        assert tm_proj == tq
        la, qt, vt, k = _in_proj(xf, attn_norm_w[l], wqkv, wzba, wqvt, wk, conv_w[l], tm_proj, T)

        avec = jnp.concatenate([jnp.zeros((LA_HEADS,), F32), a_log[l].astype(F32),
                                jnp.zeros((lane_pad,), F32)]).reshape(1, BA_PAD)
        dtvec = jnp.concatenate([jnp.zeros((LA_HEADS,), F32), dt_bias[l].astype(F32),
                                 jnp.zeros((lane_pad,), F32)]).reshape(1, BA_PAD)
        o_la = _gdn(la, avec, dtvec, la_norm_w[l], B, T, tb, chunk)

        lam_init = 0.8 - 0.6 * math.exp(-0.3 * l)
        lams = jnp.stack([lambda_q1[l], lambda_k1[l], lambda_q2[l], lambda_k2[l]]).astype(F32)
        o_d = _attn(qt, vt, k, lams, bias_tab, diff_norm_w[l], B, T, tq, lam_init)

        xf = _ffn(xf, o_la, o_d, w_out_bf, ffn_norm_w[l], w_gate_up_bf, w_down_bf, final_norm_w,
                  l, tm_ffn, final_norm=(l == depth - 1))
    return xf.reshape(B, T, D)
```

```python
import functools
import math

import jax
import jax.numpy as jnp
from jax import lax
from jax.experimental import pallas as pl
from jax.experimental.pallas import tpu as pltpu

LA_HEADS = 4
LA_DK = 128
LA_DV = 128
LA_QK = LA_HEADS * LA_DK
LA_V = LA_HEADS * LA_DV
CONV_K = 4
DIFF_HEADS = 4
DIFF_DH = 64
DIFF_DV = 2 * DIFF_DH
DIFF_QK = DIFF_HEADS * 2 * DIFF_DH
DIFF_V = DIFF_HEADS * DIFF_DV
NUM_BUCKETS = 32
MAX_DISTANCE = 128
NORM_EPS = 1e-6
DIFF_NORM_EPS = 1e-5
L2_EPS = 1e-6

LANES = 128
SUBLANES = 8
BA_PAD = LANES
LA_WIDTH = 2 * LA_QK + 2 * LA_V + BA_PAD
NEG_BIG = -1e30
FAR_UNROLLS = (4, 2, 1)
FAR_SPAN = 4
HEADS_PER_STEP = 2
CONV_SLAB = 512
MAX_LEAD = 40.0

VMEM_LIMIT = 52 * 1024 * 1024

F32 = jnp.float32
BF16 = jnp.bfloat16
HIGHEST = lax.Precision.HIGHEST


def _dot(a, b, precision=None):
    return jnp.dot(a, b, preferred_element_type=F32, precision=precision)


def _dot_nt(a, b, precision=None):
    return lax.dot_general(a, b, (((1,), (1,)), ((), ())),
                           preferred_element_type=F32, precision=precision)


def _dot_tn(a, b, precision=None):
    return lax.dot_general(a, b, (((0,), (0,)), ((), ())),
                           preferred_element_type=F32, precision=precision)


def _rms(x, w, eps):
    return x * lax.rsqrt(jnp.mean(x * x, axis=-1, keepdims=True) + eps) * w


def _silu(x):
    h = 0.5 * x
    return h + h * jnp.tanh(h)


def _softplus(x):
    return jnp.maximum(x, 0.0) + jnp.log1p(jnp.exp(-jnp.abs(x)))


def _in_proj_kernel(x_ref, nw_ref, wqkv_ref, wzba_ref, wqvt_ref, wk_ref, convw_ref,
                    la_ref, qt_ref, vt_ref, k_ref, xbuf, *, tiles_per_seq):
    i = pl.program_id(0)
    tm = x_ref.shape[0]
    W = 2 * LA_QK + LA_V

    n_slabs, _, ws = xbuf.shape

    @pl.when(i % tiles_per_seq == 0)
    def _():
        xbuf[:, 0:SUBLANES, :] = jnp.zeros((n_slabs, SUBLANES, ws), F32)

    h = _rms(x_ref[...], nw_ref[...], NORM_EPS).astype(BF16)
    for s in range(n_slabs):
        xbuf[s, SUBLANES:SUBLANES + tm, :] = _dot(h, wqkv_ref[s])
        c0 = s * ws
        conv = convw_ref[CONV_K - 1:CONV_K, c0:c0 + ws] * xbuf[s, SUBLANES:SUBLANES + tm, :]
        for k in range(CONV_K - 1):
            off = SUBLANES - (CONV_K - 1) + k
            conv = conv + convw_ref[k:k + 1, c0:c0 + ws] * xbuf[s, off:off + tm, :]
        xbuf[s, 0:SUBLANES, :] = xbuf[s, tm:tm + SUBLANES, :]
        y = _silu(conv)
        if c0 >= 2 * LA_QK:
            la_ref[:, c0:c0 + ws] = y
            continue
        scale = LA_DK ** -0.5 if c0 < LA_QK else 1.0
        for hd in range(ws // LA_DK):
            yh = y[:, hd * LA_DK:(hd + 1) * LA_DK]
            yn = yh * lax.rsqrt(jnp.sum(yh * yh, -1, keepdims=True) + L2_EPS)
            la_ref[:, c0 + hd * LA_DK:c0 + (hd + 1) * LA_DK] = yn * scale if scale != 1.0 else yn
    qvt = _dot_nt(wqvt_ref[...], h)
    qt_ref[...] = (qvt[:DIFF_QK] * (DIFF_DH ** -0.5 * math.log2(math.e))).astype(BF16)
    vt_ref[0] = qvt[DIFF_QK:].astype(BF16)
    k_ref[...] = _dot(h, wk_ref[...]).astype(BF16)
    la_ref[:, W:] = _dot(h, wzba_ref[...])


def _in_proj(xf, nw, wqkv, wzba, wqvt, wk, convw, tm, T):
    M, D = xf.shape
    assert T % tm == 0
    kern = functools.partial(_in_proj_kernel, tiles_per_seq=T // tm)
    return pl.pallas_call(
        kern,
        grid=(M // tm,),
        in_specs=[
            pl.BlockSpec((tm, D), lambda i: (i, 0)),
            pl.BlockSpec((1, D), lambda i: (0, 0)),
            pl.BlockSpec(wqkv.shape, lambda i: (0, 0, 0)),
            pl.BlockSpec(wzba.shape, lambda i: (0, 0)),
            pl.BlockSpec(wqvt.shape, lambda i: (0, 0)),
            pl.BlockSpec(wk.shape, lambda i: (0, 0)),
            pl.BlockSpec(convw.shape, lambda i: (0, 0)),
        ],
        out_specs=[
            pl.BlockSpec((tm, LA_WIDTH), lambda i: (i, 0)),
            pl.BlockSpec((DIFF_QK, tm), lambda i: (0, i)),
            pl.BlockSpec((1, DIFF_V, tm), lambda i: (i, 0, 0)),
            pl.BlockSpec((tm, wk.shape[1]), lambda i: (i, 0)),
        ],
        out_shape=[
            jax.ShapeDtypeStruct((M, LA_WIDTH), F32),
            jax.ShapeDtypeStruct((DIFF_QK, M), BF16),
            jax.ShapeDtypeStruct((M // tm, DIFF_V, tm), BF16),
            jax.ShapeDtypeStruct((M, wk.shape[1]), BF16),
        ],
        scratch_shapes=[pltpu.VMEM((wqkv.shape[0], tm + SUBLANES, wqkv.shape[2]), F32)],
        compiler_params=pltpu.CompilerParams(
            dimension_semantics=("arbitrary",), vmem_limit_bytes=VMEM_LIMIT),
        name="in_proj",
    )(xf, nw.reshape(1, D), wqkv, wzba, wqvt, wk, convw)


def _unit_lower_inverse_minus_eye(Ls, C):
    row = lax.broadcasted_iota(jnp.int32, (C, C), 0)
    col = lax.broadcasted_iota(jnp.int32, (C, C), 1)
    xs = None
    s = 1
    while s < C:
        same_pair = (row // (2 * s)) == (col // (2 * s))
        low_left = same_pair & ((row // s) % 2 == 1) & ((col // s) % 2 == 0)
        offs = [jnp.where(low_left, L, 0.0) for L in Ls]
        if xs is None:
            xs = [-m for m in offs]
        else:
            xb = [x.astype(BF16) for x in xs]
            ys = [m + _dot(m.astype(BF16), x) for m, x in zip(offs, xb)]
            xs = [x - y - _dot(x16, y.astype(BF16)) for x, x16, y in zip(xs, xb, ys)]
        s *= 2
    return xs


def _gdn_kernel(qkv_ref, z_ref, ba_ref, avec_ref, dtvec_ref, nw_ref,
                o_ref, state, g_s, gt_s, beta_s, *, TB, C):
    t = pl.program_id(0)
    B = qkv_ref.shape[0]
    nC = TB // C

    @pl.when(t == 0)
    def _():
        state[...] = jnp.zeros(state.shape, F32)

    row = lax.broadcasted_iota(jnp.int32, (C, C), 0)
    col = lax.broadcasted_iota(jnp.int32, (C, C), 1)
    causal = row >= col
    strict = row > col
    tril = causal.astype(F32)
    ba_lane = lax.broadcasted_iota(jnp.int32, (1, BA_PAD), 1)
    is_g = (ba_lane >= LA_HEADS) & (ba_lane < 2 * LA_HEADS)

    for b in range(B):
        ba = ba_ref[b]
        beta_s[b] = jax.nn.sigmoid(ba)
        g_raw = -jnp.where(is_g, jnp.exp(avec_ref[...]), 0.0) * _softplus(ba + dtvec_ref[...])
        for c in range(nC):
            g_s[b, c * C:(c + 1) * C, :] = _dot(tril, g_raw[c * C:(c + 1) * C, :], HIGHEST)
        for j in range(TB // LANES):
            gt_s[b, :, j * LANES:(j + 1) * LANES] = g_s[b, j * LANES:(j + 1) * LANES, :].T

    probs = [(b, c, h) for b in range(B) for c in range(nC) for h in range(LA_HEADS)]
    index = {p: n for n, p in enumerate(probs)}
    Ls, intras, rhss, qgs, kds, egls = [], [], [], [], [], []
    for b, c, h in probs:
        rs = slice(c * C, (c + 1) * C)
        qn = qkv_ref[b, rs, h * LA_DK:(h + 1) * LA_DK]
        kn = qkv_ref[b, rs, LA_QK + h * LA_DK:LA_QK + (h + 1) * LA_DK]
        v = qkv_ref[b, rs, 2 * LA_QK + h * LA_DV:2 * LA_QK + (h + 1) * LA_DV]
        gb = jnp.broadcast_to(g_s[b, rs, LA_HEADS + h:LA_HEADS + h + 1], (C, LANES))
        beta = jnp.broadcast_to(beta_s[b, rs, h:h + 1], (C, LANES))
        g_row = gt_s[b, LA_HEADS + h:LA_HEADS + h + 1, rs]
        gdiff = gb[:, :C] - g_row
        decay = jnp.where(causal, jnp.exp(jnp.where(causal, gdiff, 0.0)), 0.0)
        kb = kn * beta
        kq = _dot_nt(jnp.concatenate([kb, qn], axis=0).astype(BF16), kn.astype(BF16))
        Ls.append(jnp.where(strict, kq[:C] * decay, 0.0))
        intras.append((kq[C:] * decay).astype(BF16))
        eg = jnp.exp(gb)
        rhss.append(jnp.concatenate([v * beta, kb * eg], axis=1))
        qgs.append((qn * eg).astype(BF16))
        g_last = gb[C - 1:C, :]
        kds.append((kn * jnp.exp(g_last - gb)).astype(BF16))
        egls.append(jnp.exp(g_last))

    xs = _unit_lower_inverse_minus_eye(Ls, C)
    sols = [r + _dot(x.astype(BF16), r.astype(BF16)) for x, r in zip(xs, rhss)]

    streams = [(b, h) for b in range(B) for h in range(LA_HEADS)]
    S = [state[b * LA_HEADS + h] for b, h in streams]
    for c in range(nC):
        ps = [index[(b, c, h)] for b, h in streams]
        r2 = [_dot(jnp.concatenate([sols[p][:, LA_DV:].astype(BF16), qgs[p]], axis=0),
                   S[n].astype(BF16)) for n, p in enumerate(ps)]
        vn = [(sols[p][:, :LA_DV] - r2[n][:C]).astype(BF16) for n, p in enumerate(ps)]
        S = [S[n] * egls[p] + _dot_tn(kds[p], vn[n]) for n, p in enumerate(ps)]
        rs = slice(c * C, (c + 1) * C)
        for n, ((b, h), p) in enumerate(zip(streams, ps)):
            o = r2[n][C:] + _dot(intras[p], vn[n])
            zg = _silu(z_ref[b, rs, h * LA_DV:(h + 1) * LA_DV])
            o_ref[b, rs, h * LA_DV:(h + 1) * LA_DV] = (
                _rms(o, nw_ref[...], NORM_EPS) * zg).astype(BF16)
    for n in range(len(streams)):
        state[n] = S[n]


def _gdn(la, avec, dtvec, nw, B, T, TB, C):
    nT = T // TB
    W = 2 * LA_QK + LA_V
    la3 = la.reshape(B, T, la.shape[1])
    kern = functools.partial(_gdn_kernel, TB=TB, C=C)
    out = pl.pallas_call(
        kern,
        grid=(nT,),
        in_specs=[
            pl.BlockSpec((B, TB, W), lambda t: (0, t, 0)),
            pl.BlockSpec((B, TB, LA_V), lambda t: (0, t, W // LA_V)),
            pl.BlockSpec((B, TB, BA_PAD), lambda t: (0, t, (W + LA_V) // BA_PAD)),
            pl.BlockSpec((1, BA_PAD), lambda t: (0, 0)),
            pl.BlockSpec((1, BA_PAD), lambda t: (0, 0)),
            pl.BlockSpec((1, LA_DV), lambda t: (0, 0)),
        ],
        out_specs=pl.BlockSpec((B, TB, LA_V), lambda t: (0, t, 0)),
        out_shape=jax.ShapeDtypeStruct((B, T, LA_V), BF16),
        scratch_shapes=[
            pltpu.VMEM((B * LA_HEADS, LA_DK, LA_DV), F32),
            pltpu.VMEM((B, TB, BA_PAD), F32),
            pltpu.VMEM((B, LANES, TB), F32),
            pltpu.VMEM((B, TB, BA_PAD), F32),
        ],
        compiler_params=pltpu.CompilerParams(
            dimension_semantics=("arbitrary",), vmem_limit_bytes=VMEM_LIMIT),
        name="gdn",
    )(la3, la3, la3, avec, dtvec, nw.reshape(1, LA_DV))
    return out.reshape(B * T, LA_V)


def _attn_kernel(lam_ref, tab_ref, q_ref, k_ref, vt_ref, nw_ref, o_ref, bias_s, *,
                 tq, lam_init):
    i = pl.program_id(2)
    heads = range(HEADS_PER_STEP)

    def cols(hh):
        return slice(hh * LANES, (hh + 1) * LANES)

    def vt(hh, keys):
        j, span = keys
        return jnp.concatenate([vt_ref[j + u, hh * LANES:(hh + 1) * LANES, :]
                                for u in range(span)], axis=1)

    @pl.when(i == 0)
    def _():
        for hh in heads:
            for e in range(2):
                x = jnp.broadcast_to(tab_ref[hh, e], (tq, 2 * tq))
                bias_s[hh, e] = pltpu.roll(x, 0, 1, stride=1, stride_axis=0)[:, :tq]

    q2t = []
    for hh in heads:
        qt = q_ref[hh * LANES:(hh + 1) * LANES, :].astype(F32)
        sub = lax.broadcasted_iota(jnp.int32, qt.shape, 0)
        q2t.append(jnp.concatenate([jnp.where(sub < DIFF_DH, qt, 0.0),
                                    jnp.where(sub >= DIFF_DH, qt, 0.0)], axis=1).astype(BF16))

    def scores(hh, keys):
        j, span = keys
        kj = k_ref[0, pl.ds(pl.multiple_of(j * tq, tq), span * tq), cols(hh)]
        return _dot(kj, q2t[hh])

    def biased(s, bias):
        return jnp.concatenate([s[:, :tq] + bias, s[:, tq:] + bias], axis=1)

    def update(hh, carry, s, keys):
        m, l, acc = carry
        m_new = jnp.maximum(m, jnp.max(s, axis=0, keepdims=True))
        alpha = jnp.exp2(m - m_new)
        p = jnp.exp2(s - m_new)
        l = alpha * l + jnp.sum(p, axis=0, keepdims=True)
        acc = alpha * acc + _dot(vt(hh, keys), p.astype(BF16))
        return m_new, l, acc

    def group_update(carries, blocks):
        ms = [c[0] for c in carries]
        ls = [c[1] for c in carries]
        accs = [c[2] for c in carries]
        m_news = list(ms)
        for u in range(len(blocks[0])):
            for hh in heads:
                j, thunk, off = blocks[hh][u]
                s = thunk()
                ref = ms[hh] if off is None else ms[hh] + off
                p = jnp.exp2(s - ref)
                ls[hh] = ls[hh] + jnp.sum(p, axis=0, keepdims=True)
                accs[hh] = accs[hh] + _dot(vt(hh, j), p.astype(BF16))
                smax = jnp.max(s, axis=0, keepdims=True)
                m_news[hh] = jnp.maximum(m_news[hh], smax if off is None else smax - off)
        fast = []
        lead = None
        for hh in heads:
            alpha = jnp.exp2(ms[hh] - m_news[hh])
            fast.append((m_news[hh], alpha * ls[hh], alpha * accs[hh]))
            lead_h = jnp.max(m_news[hh] - ms[hh])
            lead = lead_h if lead is None else jnp.maximum(lead, lead_h)

        def redo():
            out = []
            for hh in heads:
                c = carries[hh]
                for j, thunk, off in blocks[hh]:
                    s = thunk()
                    c = update(hh, c, s if off is None else s - off, j)
                out.append(c)
            return tuple(out)

        return lax.cond(lead <= MAX_LEAD, lambda: tuple(fast), redo)

    j_near = jnp.maximum(i - 1, 0)
    edge = LANES
    assert edge >= MAX_DISTANCE and tq >= 2 * edge

    def near_scores(hh):
        s = scores(hh, (j_near, 1))
        corner = bias_s[hh, 1, tq - edge:tq, 0:edge]
        bot = s[tq - edge:]
        bot = jnp.concatenate([bot[:, :edge] + corner, bot[:, edge:tq],
                               bot[:, tq:tq + edge] + corner, bot[:, tq + edge:]], axis=1)
        return jnp.concatenate([s[:tq - edge], bot], axis=0)

    no_near = jnp.where(i == 0, -NEG_BIG, 0.0)
    carries, tail = [], []
    for hh in heads:
        s_diag = biased(scores(hh, (i, 1)), bias_s[hh, 0])
        m0 = jnp.max(s_diag[:DIFF_DH], axis=0, keepdims=True)
        carries.append((m0, jnp.zeros((1, 2 * tq), F32), jnp.zeros((DIFF_DV, 2 * tq), F32)))
        tail.append([((i, 1), functools.partial(lambda s: s, s_diag), None),
                     ((j_near, 1), functools.partial(near_scores, hh), no_near)])
    carries = group_update(tuple(carries), tail)
    n_far = jnp.maximum(i - 1, 0)
    done = 0
    for width in FAR_UNROLLS:
        def body(g, c, width=width, done=done):
            span = min(width, FAR_SPAN)
            ranges = [(done + width * g + u, span) for u in range(0, width, span)]
            return group_update(c, [[(r, functools.partial(scores, hh, r), None) for r in ranges]
                                    for hh in heads])
        n_groups = (n_far - done) // width
        carries = lax.fori_loop(0, n_groups, body, carries)
        done = done + n_groups * width

    lam = (jnp.exp(jnp.sum(lam_ref[0:1, :] * lam_ref[1:2, :], axis=-1, keepdims=True))
           - jnp.exp(jnp.sum(lam_ref[2:3, :] * lam_ref[3:4, :], axis=-1, keepdims=True)) + lam_init)
    for hh in heads:
        m, l, acc = carries[hh]
        inv_l = 1.0 / l
        ot = acc[:, :tq] * inv_l[:, :tq] - lam * (acc[:, tq:] * inv_l[:, tq:])
        gain = jnp.concatenate([nw_ref[...]] * (tq // LANES), axis=1)
        ms = jnp.mean(ot * ot, axis=0, keepdims=True)
        o_ref[hh * LANES:(hh + 1) * LANES, :] = (
            ot * lax.rsqrt(ms + DIFF_NORM_EPS) * gain * (1.0 - lam_init)).astype(BF16)


def _attn(qt, vt, k, lams, bias_tab, nw, B, T, tq, lam_init):
    assert T % tq == 0 and DIFF_HEADS % HEADS_PER_STEP == 0 and vt.shape[2] == tq
    nq = T // tq
    hw = HEADS_PER_STEP * LANES
    k3 = k.reshape(B, T, k.shape[1])
    kern = functools.partial(_attn_kernel, tq=tq, lam_init=lam_init)
    return pl.pallas_call(
        kern,
        grid=(B, DIFF_HEADS // HEADS_PER_STEP, nq),
        in_specs=[
            pl.BlockSpec((4, DIFF_DH), lambda b, g, i: (0, 0)),
            pl.BlockSpec((HEADS_PER_STEP, 2, 1, 2 * tq), lambda b, g, i: (g, 0, 0, 0)),
            pl.BlockSpec((hw, tq), lambda b, g, i: (g, b * nq + i)),
            pl.BlockSpec((1, T, hw), lambda b, g, i: (b, 0, g)),
            pl.BlockSpec((nq, hw, tq), lambda b, g, i: (b, g, 0)),
            pl.BlockSpec((DIFF_DV, LANES), lambda b, g, i: (0, 0)),
        ],
        out_specs=pl.BlockSpec((hw, tq), lambda b, g, i: (g, b * nq + i)),
        out_shape=jax.ShapeDtypeStruct((DIFF_V, B * T), BF16),
        scratch_shapes=[
            pltpu.VMEM((HEADS_PER_STEP, 2, tq, tq), F32),
        ],
        compiler_params=pltpu.CompilerParams(
            dimension_semantics=("parallel", "parallel", "arbitrary"),
            vmem_limit_bytes=VMEM_LIMIT),
        name="diff_attn",
    )(lams, bias_tab, qt, k3, vt, jnp.broadcast_to(nw.astype(F32)[:, None], (DIFF_DV, LANES)))


def _t5_bucket(rel):
    n = jnp.maximum(rel, 0)
    max_exact = NUM_BUCKETS // 2
    nf = jnp.maximum(n, 1).astype(F32)
    large = max_exact + (jnp.log(nf / max_exact) / math.log(MAX_DISTANCE / max_exact)
                         * (NUM_BUCKETS - max_exact)).astype(jnp.int32)
    large = jnp.minimum(large, NUM_BUCKETS - 1)
    return jnp.where(n < max_exact, n, large)


def _bias_tables(rel_bias, tq):
    assert tq + 1 >= MAX_DISTANCE
    m = jnp.arange(2 * tq)
    far = rel_bias[NUM_BUCKETS - 1].astype(F32)
    tabs = []
    for d in (0, tq):
        rel = jnp.where(m < tq, d + m, d + m - 2 * tq)
        b = (rel_bias[_t5_bucket(rel)].astype(F32) - far) * math.log2(math.e)
        tabs.append(jnp.where((rel >= 0)[:, None], b, NEG_BIG).T)
    return jnp.stack(tabs, axis=1)[:, :, None, :]


def _ffn_kernel(x_ref, ola_ref, od_ref, wout_ref, nw_ref, wgu_ref, wd_ref, fnw_ref,
                out_ref, *, final_norm):
    d_ff = wd_ref.shape[0]
    y = (x_ref[...] + _dot(ola_ref[...], wout_ref[0:LA_V, :])
         + _dot_tn(od_ref[...], wout_ref[LA_V:LA_V + DIFF_V, :]))
    h = _rms(y, nw_ref[...], NORM_EPS).astype(BF16)
    gate = _dot(h, wgu_ref[:, 0:d_ff])
    up = _dot(h, wgu_ref[:, d_ff:2 * d_ff])
    y = y + _dot((_silu(gate) * up).astype(BF16), wd_ref[...])
    if final_norm:
        y = _rms(y, fnw_ref[...], NORM_EPS)
    out_ref[...] = y


def _ffn(xf, o_la, o_d, wout, nw, wgu, wdn, fnw, layer, tm, final_norm):
    M, D = xf.shape
    kern = functools.partial(_ffn_kernel, final_norm=final_norm)
    resident = dict(pipeline_mode=pl.Buffered(1))

    def layer_slab(w):
        return pl.BlockSpec((None,) + w.shape[1:], lambda i: (layer, 0, 0), **resident)

    return pl.pallas_call(
        kern,
        grid=(M // tm,),
        in_specs=[
            pl.BlockSpec((tm, D), lambda i: (i, 0)),
            pl.BlockSpec((tm, LA_V), lambda i: (i, 0)),
            pl.BlockSpec((DIFF_V, tm), lambda i: (0, i)),
            layer_slab(wout),
            pl.BlockSpec((1, D), lambda i: (0, 0)),
            layer_slab(wgu),
            layer_slab(wdn),
            pl.BlockSpec((1, D), lambda i: (0, 0)),
        ],
        out_specs=pl.BlockSpec((tm, D), lambda i: (i, 0)),
        out_shape=jax.ShapeDtypeStruct((M, D), F32),
        compiler_params=pltpu.CompilerParams(
            dimension_semantics=("parallel",), vmem_limit_bytes=VMEM_LIMIT),
        name="out_proj_ffn",
    )(xf, o_la, o_d, wout, nw.reshape(1, D), wgu, wdn, fnw.reshape(1, D))


def _pick(n, pref):
    return pref if n % pref == 0 else n


def kernel(x, attn_norm_w, w_in, conv_w, a_log, dt_bias, la_norm_w, lambda_q1, lambda_k1,
           lambda_q2, lambda_k2, diff_norm_w, rel_bias, w_out, ffn_norm_w, w_gate_up,
           w_down, final_norm_w):
    B, T, D = x.shape
    depth = w_in.shape[0]
    M = B * T
    tm_proj = _pick(M, 512)
    tm_ffn = _pick(M, 512)
    tb = _pick(T, 256)
    chunk = 128
    tq = _pick(T, 512)

    n_la = 2 * LA_QK + 2 * LA_V
    bias_tab = _bias_tables(rel_bias, tq)
    lane_pad = BA_PAD - 2 * LA_HEADS

    w_in_bf = lax.optimization_barrier(w_in.astype(BF16))
    w_out_bf = w_out.astype(BF16)
    w_gate_up_bf = w_gate_up.astype(BF16)
    w_down_bf = w_down.astype(BF16)
    n_qkv = 2 * LA_QK + LA_V

    xf = x.reshape(M, D)
    for l in range(depth):
        w = w_in_bf[l]
        wqkv = jnp.stack([w[:, s * CONV_SLAB:(s + 1) * CONV_SLAB]
                          for s in range(n_qkv // CONV_SLAB)])
        wzba = jnp.concatenate([w[:, n_qkv:n_la + 2 * LA_HEADS],
                                jnp.zeros((D, lane_pad), w.dtype)], axis=1)
        n_d = n_la + 2 * LA_HEADS
        wqvt = jnp.concatenate([w[:, n_d:n_d + DIFF_QK], w[:, n_d + 2 * DIFF_QK:]], axis=1).T
        wk = w[:, n_d + DIFF_QK:n_d + 2 * DIFF_QK]
        assert tm_proj == tq
        la, qt, vt, k = _in_proj(xf, attn_norm_w[l], wqkv, wzba, wqvt, wk, conv_w[l], tm_proj, T)

        avec = jnp.concatenate([jnp.zeros((LA_HEADS,), F32), a_log[l].astype(F32),
                                jnp.zeros((lane_pad,), F32)]).reshape(1, BA_PAD)
        dtvec = jnp.concatenate([jnp.zeros((LA_HEADS,), F32), dt_bias[l].astype(F32),
                                 jnp.zeros((lane_pad,), F32)]).reshape(1, BA_PAD)
        o_la = _gdn(la, avec, dtvec, la_norm_w[l], B, T, tb, chunk)

        lam_init = 0.8 - 0.6 * math.exp(-0.3 * l)
        lams = jnp.stack([lambda_q1[l], lambda_k1[l], lambda_q2[l], lambda_k2[l]]).astype(F32)
        o_d = _attn(qt, vt, k, lams, bias_tab, diff_norm_w[l], B, T, tq, lam_init)

        xf = _ffn(xf, o_la, o_d, w_out_bf, ffn_norm_w[l], w_gate_up_bf, w_down_bf, final_norm_w,
                  l, tm_ffn, final_norm=(l == depth - 1))
    return xf.reshape(B, T, D)
```

```python
import functools
import math

import jax
import jax.numpy as jnp
from jax import lax
from jax.experimental import pallas as pl
from jax.experimental.pallas import tpu as pltpu

LA_HEADS = 4
LA_DK = 128
LA_DV = 128
LA_QK = LA_HEADS * LA_DK
LA_V = LA_HEADS * LA_DV
CONV_K = 4
DIFF_HEADS = 4
DIFF_DH = 64
DIFF_DV = 2 * DIFF_DH
DIFF_QK = DIFF_HEADS * 2 * DIFF_DH
DIFF_V = DIFF_HEADS * DIFF_DV
NUM_BUCKETS = 32
MAX_DISTANCE = 128
NORM_EPS = 1e-6
DIFF_NORM_EPS = 1e-5
L2_EPS = 1e-6

LANES = 128
SUBLANES = 8
BA_PAD = LANES
LA_WIDTH = 2 * LA_QK + 2 * LA_V + BA_PAD
NEG_BIG = -1e30
FAR_UNROLLS = (4, 2, 1)
FAR_SPAN = 4
HEADS_PER_STEP = 2
CONV_SLAB = 512
MAX_LEAD = 40.0

VMEM_LIMIT = 52 * 1024 * 1024

F32 = jnp.float32
BF16 = jnp.bfloat16


def _dot(a, b, precision=None):
    return jnp.dot(a, b, preferred_element_type=F32, precision=precision)


def _dot_nt(a, b, precision=None):
    return lax.dot_general(a, b, (((1,), (1,)), ((), ())),
                           preferred_element_type=F32, precision=precision)


def _dot_tn(a, b, precision=None):
    return lax.dot_general(a, b, (((0,), (0,)), ((), ())),
                           preferred_element_type=F32, precision=precision)


def _dot_exact_lhs(a, b):
    a = a.astype(BF16)
    b_hi = b.astype(BF16)
    rest = b - b_hi.astype(F32)
    b_mid = rest.astype(BF16)
    b_lo = (rest - b_mid.astype(F32)).astype(BF16)
    return _dot(a, b_hi) + _dot(a, b_mid) + _dot(a, b_lo)


def _rms(x, w, eps):
    return x * lax.rsqrt(jnp.mean(x * x, axis=-1, keepdims=True) + eps) * w


def _silu(x):
    h = 0.5 * x
    return h + h * jnp.tanh(h)


def _softplus(x):
    return jnp.maximum(x, 0.0) + jnp.log1p(jnp.exp(-jnp.abs(x)))


def _in_proj_kernel(x_ref, nw_ref, wqkv_ref, wzba_ref, wqvt_ref, wk_ref, convw_ref,
                    la_ref, qt_ref, vt_ref, k_ref, xbuf, *, tiles_per_seq):
    i = pl.program_id(0)
    tm = x_ref.shape[0]
    W = 2 * LA_QK + LA_V

    n_slabs, _, ws = xbuf.shape

    @pl.when(i % tiles_per_seq == 0)
    def _():
        xbuf[:, 0:SUBLANES, :] = jnp.zeros((n_slabs, SUBLANES, ws), F32)

    h = _rms(x_ref[...], nw_ref[...], NORM_EPS).astype(BF16)
    for s in range(n_slabs):
        xbuf[s, SUBLANES:SUBLANES + tm, :] = _dot(h, wqkv_ref[s])
        c0 = s * ws
        conv = convw_ref[CONV_K - 1:CONV_K, c0:c0 + ws] * xbuf[s, SUBLANES:SUBLANES + tm, :]
        for k in range(CONV_K - 1):
            off = SUBLANES - (CONV_K - 1) + k
            conv = conv + convw_ref[k:k + 1, c0:c0 + ws] * xbuf[s, off:off + tm, :]
        xbuf[s, 0:SUBLANES, :] = xbuf[s, tm:tm + SUBLANES, :]
        y = _silu(conv)
        if c0 >= 2 * LA_QK:
            la_ref[:, c0:c0 + ws] = y
            continue
        scale = LA_DK ** -0.5 if c0 < LA_QK else 1.0
        for hd in range(ws // LA_DK):
            yh = y[:, hd * LA_DK:(hd + 1) * LA_DK]
            yn = yh * lax.rsqrt(jnp.sum(yh * yh, -1, keepdims=True) + L2_EPS)
            la_ref[:, c0 + hd * LA_DK:c0 + (hd + 1) * LA_DK] = yn * scale if scale != 1.0 else yn
    qvt = _dot_nt(wqvt_ref[...], h)
    qt_ref[...] = (qvt[:DIFF_QK] * (DIFF_DH ** -0.5 * math.log2(math.e))).astype(BF16)
    vt_ref[0] = qvt[DIFF_QK:].astype(BF16)
    k_ref[...] = _dot(h, wk_ref[...]).astype(BF16)
    la_ref[:, W:] = _dot(h, wzba_ref[...])


def _in_proj(xf, nw, wqkv, wzba, wqvt, wk, convw, tm, T):
    M, D = xf.shape
    assert T % tm == 0
    kern = functools.partial(_in_proj_kernel, tiles_per_seq=T // tm)
    return pl.pallas_call(
        kern,
        grid=(M // tm,),
        in_specs=[
            pl.BlockSpec((tm, D), lambda i: (i, 0)),
            pl.BlockSpec((1, D), lambda i: (0, 0)),
            pl.BlockSpec(wqkv.shape, lambda i: (0, 0, 0)),
            pl.BlockSpec(wzba.shape, lambda i: (0, 0)),
            pl.BlockSpec(wqvt.shape, lambda i: (0, 0)),
            pl.BlockSpec(wk.shape, lambda i: (0, 0)),
            pl.BlockSpec(convw.shape, lambda i: (0, 0)),
        ],
        out_specs=[
            pl.BlockSpec((tm, LA_WIDTH), lambda i: (i, 0)),
            pl.BlockSpec((DIFF_QK, tm), lambda i: (0, i)),
            pl.BlockSpec((1, DIFF_V, tm), lambda i: (i, 0, 0)),
            pl.BlockSpec((tm, wk.shape[1]), lambda i: (i, 0)),
        ],
        out_shape=[
            jax.ShapeDtypeStruct((M, LA_WIDTH), F32),
            jax.ShapeDtypeStruct((DIFF_QK, M), BF16),
            jax.ShapeDtypeStruct((M // tm, DIFF_V, tm), BF16),
            jax.ShapeDtypeStruct((M, wk.shape[1]), BF16),
        ],
        scratch_shapes=[pltpu.VMEM((wqkv.shape[0], tm + SUBLANES, wqkv.shape[2]), F32)],
        compiler_params=pltpu.CompilerParams(
            dimension_semantics=("arbitrary",), vmem_limit_bytes=VMEM_LIMIT),
        name="in_proj",
    )(xf, nw.reshape(1, D), wqkv, wzba, wqvt, wk, convw)


def _unit_lower_inverse_minus_eye(Ls, C):
    row = lax.broadcasted_iota(jnp.int32, (C, C), 0)
    col = lax.broadcasted_iota(jnp.int32, (C, C), 1)
    xs = None
    s = 1
    while s < C:
        same_pair = (row // (2 * s)) == (col // (2 * s))
        low_left = same_pair & ((row // s) % 2 == 1) & ((col // s) % 2 == 0)
        offs = [jnp.where(low_left, L, 0.0) for L in Ls]
        if xs is None:
            xs = [-m for m in offs]
        else:
            xb = [x.astype(BF16) for x in xs]
            ys = [m + _dot(m.astype(BF16), x) for m, x in zip(offs, xb)]
            xs = [x - y - _dot(x16, y.astype(BF16)) for x, x16, y in zip(xs, xb, ys)]
        s *= 2
    return xs


def _gdn_kernel(qkv_ref, z_ref, ba_ref, avec_ref, dtvec_ref, nw_ref,
                o_ref, state, g_s, gt_s, beta_s, *, TB, C):
    t = pl.program_id(0)
    B = qkv_ref.shape[0]
    nC = TB // C

    @pl.when(t == 0)
    def _():
        state[...] = jnp.zeros(state.shape, F32)

    row = lax.broadcasted_iota(jnp.int32, (C, C), 0)
    col = lax.broadcasted_iota(jnp.int32, (C, C), 1)
    causal = row >= col
    strict = row > col
    tril = causal.astype(F32)
    ba_lane = lax.broadcasted_iota(jnp.int32, (1, BA_PAD), 1)
    is_g = (ba_lane >= LA_HEADS) & (ba_lane < 2 * LA_HEADS)

    for b in range(B):
        ba = ba_ref[b]
        beta_s[b] = jax.nn.sigmoid(ba)
        g_raw = -jnp.where(is_g, jnp.exp(avec_ref[...]), 0.0) * _softplus(ba + dtvec_ref[...])
        for c in range(nC):
            g_s[b, c * C:(c + 1) * C, :] = _dot_exact_lhs(tril, g_raw[c * C:(c + 1) * C, :])
        for j in range(TB // LANES):
            gt_s[b, :, j * LANES:(j + 1) * LANES] = g_s[b, j * LANES:(j + 1) * LANES, :].T

    probs = [(b, c, h) for b in range(B) for c in range(nC) for h in range(LA_HEADS)]
    index = {p: n for n, p in enumerate(probs)}
    Ls, intras, rhss, qgs, kds, egls = [], [], [], [], [], []
    for b, c, h in probs:
        rs = slice(c * C, (c + 1) * C)
        qn = qkv_ref[b, rs, h * LA_DK:(h + 1) * LA_DK]
        kn = qkv_ref[b, rs, LA_QK + h * LA_DK:LA_QK + (h + 1) * LA_DK]
        v = qkv_ref[b, rs, 2 * LA_QK + h * LA_DV:2 * LA_QK + (h + 1) * LA_DV]
        gb = jnp.broadcast_to(g_s[b, rs, LA_HEADS + h:LA_HEADS + h + 1], (C, LANES))
        beta = jnp.broadcast_to(beta_s[b, rs, h:h + 1], (C, LANES))
        g_row = gt_s[b, LA_HEADS + h:LA_HEADS + h + 1, rs]
        gdiff = gb[:, :C] - g_row
        decay = jnp.where(causal, jnp.exp(jnp.where(causal, gdiff, 0.0)), 0.0)
        kb = kn * beta
        kq = _dot_nt(jnp.concatenate([kb, qn], axis=0).astype(BF16), kn.astype(BF16))
        Ls.append(jnp.where(strict, kq[:C] * decay, 0.0))
        intras.append((kq[C:] * decay).astype(BF16))
        eg = jnp.exp(gb)
        rhss.append(jnp.concatenate([v * beta, kb * eg], axis=1))
        qgs.append((qn * eg).astype(BF16))
        g_last = gb[C - 1:C, :]
        kds.append((kn * jnp.exp(g_last - gb)).astype(BF16))
        egls.append(jnp.exp(g_last))

    xs = _unit_lower_inverse_minus_eye(Ls, C)
    sols = [r + _dot(x.astype(BF16), r.astype(BF16)) for x, r in zip(xs, rhss)]

    streams = [(b, h) for b in range(B) for h in range(LA_HEADS)]
    S = [state[b * LA_HEADS + h] for b, h in streams]
    for c in range(nC):
        ps = [index[(b, c, h)] for b, h in streams]
        r2 = [_dot(jnp.concatenate([sols[p][:, LA_DV:].astype(BF16), qgs[p]], axis=0),
                   S[n].astype(BF16)) for n, p in enumerate(ps)]
        vn = [(sols[p][:, :LA_DV] - r2[n][:C]).astype(BF16) for n, p in enumerate(ps)]
        S = [S[n] * egls[p] + _dot_tn(kds[p], vn[n]) for n, p in enumerate(ps)]
        rs = slice(c * C, (c + 1) * C)
        for n, ((b, h), p) in enumerate(zip(streams, ps)):
            o = r2[n][C:] + _dot(intras[p], vn[n])
            zg = _silu(z_ref[b, rs, h * LA_DV:(h + 1) * LA_DV])
            o_ref[b, rs, h * LA_DV:(h + 1) * LA_DV] = (
                _rms(o, nw_ref[...], NORM_EPS) * zg).astype(BF16)
    for n in range(len(streams)):
        state[n] = S[n]


def _gdn(la, avec, dtvec, nw, B, T, TB, C):
    nT = T // TB
    W = 2 * LA_QK + LA_V
    la3 = la.reshape(B, T, la.shape[1])
    kern = functools.partial(_gdn_kernel, TB=TB, C=C)
    out = pl.pallas_call(
        kern,
        grid=(nT,),
        in_specs=[
            pl.BlockSpec((B, TB, W), lambda t: (0, t, 0)),
            pl.BlockSpec((B, TB, LA_V), lambda t: (0, t, W // LA_V)),
            pl.BlockSpec((B, TB, BA_PAD), lambda t: (0, t, (W + LA_V) // BA_PAD)),
            pl.BlockSpec((1, BA_PAD), lambda t: (0, 0)),
            pl.BlockSpec((1, BA_PAD), lambda t: (0, 0)),
            pl.BlockSpec((1, LA_DV), lambda t: (0, 0)),
        ],
        out_specs=pl.BlockSpec((B, TB, LA_V), lambda t: (0, t, 0)),
        out_shape=jax.ShapeDtypeStruct((B, T, LA_V), BF16),
        scratch_shapes=[
            pltpu.VMEM((B * LA_HEADS, LA_DK, LA_DV), F32),
            pltpu.VMEM((B, TB, BA_PAD), F32),
            pltpu.VMEM((B, LANES, TB), F32),
            pltpu.VMEM((B, TB, BA_PAD), F32),
        ],
        compiler_params=pltpu.CompilerParams(
            dimension_semantics=("arbitrary",), vmem_limit_bytes=VMEM_LIMIT),
        name="gdn",
    )(la3, la3, la3, avec, dtvec, nw.reshape(1, LA_DV))
    return out.reshape(B * T, LA_V)


def _attn_kernel(lam_ref, tab_ref, q_ref, k_ref, vt_ref, nw_ref, o_ref, bias_s, *,
                 tq, lam_init):
    i = pl.program_id(2)
    heads = range(HEADS_PER_STEP)

    def cols(hh):
        return slice(hh * LANES, (hh + 1) * LANES)

    def vt(hh, keys):
        j, span = keys
        return jnp.concatenate([vt_ref[j + u, hh * LANES:(hh + 1) * LANES, :]
                                for u in range(span)], axis=1)

    @pl.when(i == 0)
    def _():
        for hh in heads:
            for e in range(2):
                x = jnp.broadcast_to(tab_ref[hh, e], (tq, 2 * tq))
                bias_s[hh, e] = pltpu.roll(x, 0, 1, stride=1, stride_axis=0)[:, :tq]

    q2t = []
    for hh in heads:
        qt = q_ref[hh * LANES:(hh + 1) * LANES, :].astype(F32)
        sub = lax.broadcasted_iota(jnp.int32, qt.shape, 0)
        q2t.append(jnp.concatenate([jnp.where(sub < DIFF_DH, qt, 0.0),
                                    jnp.where(sub >= DIFF_DH, qt, 0.0)], axis=1).astype(BF16))

    def scores(hh, keys):
        j, span = keys
        kj = k_ref[0, pl.ds(pl.multiple_of(j * tq, tq), span * tq), cols(hh)]
        return _dot(kj, q2t[hh])

    def biased(s, bias):
        return jnp.concatenate([s[:, :tq] + bias, s[:, tq:] + bias], axis=1)

    def update(hh, carry, s, keys):
        m, l, acc = carry
        m_new = jnp.maximum(m, jnp.max(s, axis=0, keepdims=True))
        alpha = jnp.exp2(m - m_new)
        p = jnp.exp2(s - m_new)
        l = alpha * l + jnp.sum(p, axis=0, keepdims=True)
        acc = alpha * acc + _dot(vt(hh, keys), p.astype(BF16))
        return m_new, l, acc

    def group_update(carries, blocks):
        ms = [c[0] for c in carries]
        ls = [c[1] for c in carries]
        accs = [c[2] for c in carries]
        m_news = list(ms)
        for u in range(len(blocks[0])):
            for hh in heads:
                j, thunk, off = blocks[hh][u]
                s = thunk()
                ref = ms[hh] if off is None else ms[hh] + off
                p = jnp.exp2(s - ref)
                ls[hh] = ls[hh] + jnp.sum(p, axis=0, keepdims=True)
                accs[hh] = accs[hh] + _dot(vt(hh, j), p.astype(BF16))
                smax = jnp.max(s, axis=0, keepdims=True)
                m_news[hh] = jnp.maximum(m_news[hh], smax if off is None else smax - off)
        fast = []
        lead = None
        for hh in heads:
            alpha = jnp.exp2(ms[hh] - m_news[hh])
            fast.append((m_news[hh], alpha * ls[hh], alpha * accs[hh]))
            lead_h = jnp.max(m_news[hh] - ms[hh])
            lead = lead_h if lead is None else jnp.maximum(lead, lead_h)

        def redo():
            out = []
            for hh in heads:
                c = carries[hh]
                for j, thunk, off in blocks[hh]:
                    s = thunk()
                    c = update(hh, c, s if off is None else s - off, j)
                out.append(c)
            return tuple(out)

        return lax.cond(lead <= MAX_LEAD, lambda: tuple(fast), redo)

    j_near = jnp.maximum(i - 1, 0)
    edge = LANES
    assert edge >= MAX_DISTANCE and tq >= 2 * edge

    def near_scores(hh):
        s = scores(hh, (j_near, 1))
        corner = bias_s[hh, 1, tq - edge:tq, 0:edge]
        bot = s[tq - edge:]
        bot = jnp.concatenate([bot[:, :edge] + corner, bot[:, edge:tq],
                               bot[:, tq:tq + edge] + corner, bot[:, tq + edge:]], axis=1)
        return jnp.concatenate([s[:tq - edge], bot], axis=0)

    no_near = jnp.where(i == 0, -NEG_BIG, 0.0)
    carries, tail = [], []
    for hh in heads:
        s_diag = biased(scores(hh, (i, 1)), bias_s[hh, 0])
        m0 = jnp.max(s_diag[:DIFF_DH], axis=0, keepdims=True)
        carries.append((m0, jnp.zeros((1, 2 * tq), F32), jnp.zeros((DIFF_DV, 2 * tq), F32)))
        tail.append([((i, 1), functools.partial(lambda s: s, s_diag), None),
                     ((j_near, 1), functools.partial(near_scores, hh), no_near)])
    carries = group_update(tuple(carries), tail)
    n_far = jnp.maximum(i - 1, 0)
    done = 0
    for width in FAR_UNROLLS:
        def body(g, c, width=width, done=done):
            span = min(width, FAR_SPAN)
            ranges = [(done + width * g + u, span) for u in range(0, width, span)]
            return group_update(c, [[(r, functools.partial(scores, hh, r), None) for r in ranges]
                                    for hh in heads])
        n_groups = (n_far - done) // width
        carries = lax.fori_loop(0, n_groups, body, carries)
        done = done + n_groups * width

    lam = (jnp.exp(jnp.sum(lam_ref[0:1, :] * lam_ref[1:2, :], axis=-1, keepdims=True))
           - jnp.exp(jnp.sum(lam_ref[2:3, :] * lam_ref[3:4, :], axis=-1, keepdims=True)) + lam_init)
    for hh in heads:
        m, l, acc = carries[hh]
        inv_l = 1.0 / l
        ot = acc[:, :tq] * inv_l[:, :tq] - lam * (acc[:, tq:] * inv_l[:, tq:])
        gain = jnp.concatenate([nw_ref[...]] * (tq // LANES), axis=1)
        ms = jnp.mean(ot * ot, axis=0, keepdims=True)
        o_ref[hh * LANES:(hh + 1) * LANES, :] = (
            ot * lax.rsqrt(ms + DIFF_NORM_EPS) * gain * (1.0 - lam_init)).astype(BF16)


def _attn(qt, vt, k, lams, bias_tab, nw, B, T, tq, lam_init):
    assert T % tq == 0 and DIFF_HEADS % HEADS_PER_STEP == 0 and vt.shape[2] == tq
    nq = T // tq
    hw = HEADS_PER_STEP * LANES
    k3 = k.reshape(B, T, k.shape[1])
    kern = functools.partial(_attn_kernel, tq=tq, lam_init=lam_init)
    return pl.pallas_call(
        kern,
        grid=(B, DIFF_HEADS // HEADS_PER_STEP, nq),
        in_specs=[
            pl.BlockSpec((4, DIFF_DH), lambda b, g, i: (0, 0)),
            pl.BlockSpec((HEADS_PER_STEP, 2, 1, 2 * tq), lambda b, g, i: (g, 0, 0, 0)),
            pl.BlockSpec((hw, tq), lambda b, g, i: (g, b * nq + i)),
            pl.BlockSpec((1, T, hw), lambda b, g, i: (b, 0, g)),
            pl.BlockSpec((nq, hw, tq), lambda b, g, i: (b, g, 0)),
            pl.BlockSpec((DIFF_DV, LANES), lambda b, g, i: (0, 0)),
        ],
        out_specs=pl.BlockSpec((hw, tq), lambda b, g, i: (g, b * nq + i)),
        out_shape=jax.ShapeDtypeStruct((DIFF_V, B * T), BF16),
        scratch_shapes=[
            pltpu.VMEM((HEADS_PER_STEP, 2, tq, tq), F32),
        ],
        compiler_params=pltpu.CompilerParams(
            dimension_semantics=("parallel", "parallel", "arbitrary"),
            vmem_limit_bytes=VMEM_LIMIT),
        name="diff_attn",
    )(lams, bias_tab, qt, k3, vt, jnp.broadcast_to(nw.astype(F32)[:, None], (DIFF_DV, LANES)))


def _t5_bucket(rel):
    n = jnp.maximum(rel, 0)
    max_exact = NUM_BUCKETS // 2
    nf = jnp.maximum(n, 1).astype(F32)
    large = max_exact + (jnp.log(nf / max_exact) / math.log(MAX_DISTANCE / max_exact)
                         * (NUM_BUCKETS - max_exact)).astype(jnp.int32)
    large = jnp.minimum(large, NUM_BUCKETS - 1)
    return jnp.where(n < max_exact, n, large)


def _bias_tables(rel_bias, tq):
    assert tq + 1 >= MAX_DISTANCE
    m = jnp.arange(2 * tq)
    far = rel_bias[NUM_BUCKETS - 1].astype(F32)
    tabs = []
    for d in (0, tq):
        rel = jnp.where(m < tq, d + m, d + m - 2 * tq)
        b = (rel_bias[_t5_bucket(rel)].astype(F32) - far) * math.log2(math.e)
        tabs.append(jnp.where((rel >= 0)[:, None], b, NEG_BIG).T)
    return jnp.stack(tabs, axis=1)[:, :, None, :]


def _ffn_kernel(x_ref, ola_ref, od_ref, wout_ref, nw_ref, wgu_ref, wd_ref, fnw_ref,
                out_ref, *, final_norm):
    d_ff = wd_ref.shape[0]
    y = (x_ref[...] + _dot(ola_ref[...], wout_ref[0:LA_V, :])
         + _dot_tn(od_ref[...], wout_ref[LA_V:LA_V + DIFF_V, :]))
    h = _rms(y, nw_ref[...], NORM_EPS).astype(BF16)
    gate = _dot(h, wgu_ref[:, 0:d_ff])
    up = _dot(h, wgu_ref[:, d_ff:2 * d_ff])
    y = y + _dot((_silu(gate) * up).astype(BF16), wd_ref[...])
    if final_norm:
        y = _rms(y, fnw_ref[...], NORM_EPS)
    out_ref[...] = y


def _ffn(xf, o_la, o_d, wout, nw, wgu, wdn, fnw, layer, tm, final_norm):
    M, D = xf.shape
    kern = functools.partial(_ffn_kernel, final_norm=final_norm)
    resident = dict(pipeline_mode=pl.Buffered(1))

    def layer_slab(w):
        return pl.BlockSpec((None,) + w.shape[1:], lambda i: (layer, 0, 0), **resident)

    return pl.pallas_call(
        kern,
        grid=(M // tm,),
        in_specs=[
            pl.BlockSpec((tm, D), lambda i: (i, 0)),
            pl.BlockSpec((tm, LA_V), lambda i: (i, 0)),
            pl.BlockSpec((DIFF_V, tm), lambda i: (0, i)),
            layer_slab(wout),
            pl.BlockSpec((1, D), lambda i: (0, 0)),
            layer_slab(wgu),
            layer_slab(wdn),
            pl.BlockSpec((1, D), lambda i: (0, 0)),
        ],
        out_specs=pl.BlockSpec((tm, D), lambda i: (i, 0)),
        out_shape=jax.ShapeDtypeStruct((M, D), F32),
        compiler_params=pltpu.CompilerParams(
            dimension_semantics=("parallel",), vmem_limit_bytes=VMEM_LIMIT),
        name="out_proj_ffn",
    )(xf, o_la, o_d, wout, nw.reshape(1, D), wgu, wdn, fnw.reshape(1, D))


def _pick(n, pref):
    return pref if n % pref == 0 else n


def kernel(x, attn_norm_w, w_in, conv_w, a_log, dt_bias, la_norm_w, lambda_q1, lambda_k1,
           lambda_q2, lambda_k2, diff_norm_w, rel_bias, w_out, ffn_norm_w, w_gate_up,
           w_down, final_norm_w):
    B, T, D = x.shape
    depth = w_in.shape[0]
    M = B * T
    tm_proj = _pick(M, 512)
    tm_ffn = _pick(M, 512)
    tb = _pick(T, 256)
    chunk = 128
    tq = _pick(T, 512)

    n_la = 2 * LA_QK + 2 * LA_V
    bias_tab = _bias_tables(rel_bias, tq)
    lane_pad = BA_PAD - 2 * LA_HEADS

    w_in_bf = lax.optimization_barrier(w_in.astype(BF16))
    w_out_bf = w_out.astype(BF16)
    w_gate_up_bf = w_gate_up.astype(BF16)
    w_down_bf = w_down.astype(BF16)
    n_qkv = 2 * LA_QK + LA_V

    xf = x.reshape(M, D)
    for l in range(depth):
        w = w_in_bf[l]
        wqkv = jnp.stack([w[:, s * CONV_SLAB:(s + 1) * CONV_SLAB]
                          for s in range(n_qkv // CONV_SLAB)])
        wzba = jnp.concatenate([w[:, n_qkv:n_la + 2 * LA_HEADS],
                                jnp.zeros((D, lane_pad), w.dtype)], axis=1)
        n_d = n_la + 2 * LA_HEADS
        wqvt = jnp.concatenate([w[:, n_d:n_d + DIFF_QK], w[:, n_d + 2 * DIFF_QK:]], axis=1).T
        wk = w[:, n_d + DIFF_QK:n_d + 2 * DIFF_QK]
        assert tm_proj == tq
        la, qt, vt, k = _in_proj(xf, attn_norm_w[l], wqkv, wzba, wqvt, wk, conv_w[l], tm_proj, T)

        avec = jnp.concatenate([jnp.zeros((LA_HEADS,), F32), a_log[l].astype(F32),
                                jnp.zeros((lane_pad,), F32)]).reshape(1, BA_PAD)
        dtvec = jnp.concatenate([jnp.zeros((LA_HEADS,), F32), dt_bias[l].astype(F32),
                                 jnp.zeros((lane_pad,), F32)]).reshape(1, BA_PAD)
        o_la = _gdn(la, avec, dtvec, la_norm_w[l], B, T, tb, chunk)

        lam_init = 0.8 - 0.6 * math.exp(-0.3 * l)
        lams = jnp.stack([lambda_q1[l], lambda_k1[l], lambda_q2[l], lambda_k2[l]]).astype(F32)
        o_d = _attn(qt, vt, k, lams, bias_tab, diff_norm_w[l], B, T, tq, lam_init)

        xf = _ffn(xf, o_la, o_d, w_out_bf, ffn_norm_w[l], w_gate_up_bf, w_down_bf, final_norm_w,
                  l, tm_ffn, final_norm=(l == depth - 1))
    return xf.reshape(B, T, D)
```

```python
import functools
import math

import jax
import jax.numpy as jnp
from jax import lax
from jax.experimental import pallas as pl
from jax.experimental.pallas import tpu as pltpu

LA_HEADS = 4
LA_DK = 128
LA_DV = 128
LA_QK = LA_HEADS * LA_DK
LA_V = LA_HEADS * LA_DV
CONV_K = 4
DIFF_HEADS = 4
DIFF_DH = 64
DIFF_DV = 2 * DIFF_DH
DIFF_QK = DIFF_HEADS * 2 * DIFF_DH
DIFF_V = DIFF_HEADS * DIFF_DV
NUM_BUCKETS = 32
MAX_DISTANCE = 128
NORM_EPS = 1e-6
DIFF_NORM_EPS = 1e-5
L2_EPS = 1e-6

LANES = 128
SUBLANES = 8
BA_PAD = LANES
LA_WIDTH = 2 * LA_QK + 2 * LA_V + BA_PAD
NEG_BIG = -1e30
FAR_UNROLLS = (4, 2, 1)
FAR_SPAN = 4
HEADS_PER_STEP = 2
CONV_SLAB = 512
MAX_LEAD = 40.0

VMEM_LIMIT = 52 * 1024 * 1024

F32 = jnp.float32
BF16 = jnp.bfloat16
HIGHEST = lax.Precision.HIGHEST


def _dot(a, b, precision=None):
    return jnp.dot(a, b, preferred_element_type=F32, precision=precision)


def _dot_nt(a, b, precision=None):
    return lax.dot_general(a, b, (((1,), (1,)), ((), ())),
                           preferred_element_type=F32, precision=precision)


def _dot_tn(a, b, precision=None):
    return lax.dot_general(a, b, (((0,), (0,)), ((), ())),
                           preferred_element_type=F32, precision=precision)


def _rms(x, w, eps):
    return x * lax.rsqrt(jnp.mean(x * x, axis=-1, keepdims=True) + eps) * w


def _silu(x):
    h = 0.5 * x
    return h + h * jnp.tanh(h)


def _softplus(x):
    return jnp.maximum(x, 0.0) + jnp.log1p(jnp.exp(-jnp.abs(x)))


def _in_proj_kernel(x_ref, nw_ref, wqkv_ref, wzba_ref, wqvt_ref, wk_ref, convw_ref,
                    la_ref, qt_ref, vt_ref, k_ref, xbuf, *, tiles_per_seq):
    i = pl.program_id(0)
    tm = x_ref.shape[0]
    W = 2 * LA_QK + LA_V

    n_slabs, _, ws = xbuf.shape

    @pl.when(i % tiles_per_seq == 0)
    def _():
        xbuf[:, 0:SUBLANES, :] = jnp.zeros((n_slabs, SUBLANES, ws), F32)

    h = _rms(x_ref[...], nw_ref[...], NORM_EPS).astype(BF16)
    for s in range(n_slabs):
        xbuf[s, SUBLANES:SUBLANES + tm, :] = _dot(h, wqkv_ref[s])
        c0 = s * ws
        conv = convw_ref[CONV_K - 1:CONV_K, c0:c0 + ws] * xbuf[s, SUBLANES:SUBLANES + tm, :]
        for k in range(CONV_K - 1):
            off = SUBLANES - (CONV_K - 1) + k
            conv = conv + convw_ref[k:k + 1, c0:c0 + ws] * xbuf[s, off:off + tm, :]
        xbuf[s, 0:SUBLANES, :] = xbuf[s, tm:tm + SUBLANES, :]
        y = _silu(conv)
        if c0 >= 2 * LA_QK:
            la_ref[:, c0:c0 + ws] = y
            continue
        scale = LA_DK ** -0.5 if c0 < LA_QK else 1.0
        for hd in range(ws // LA_DK):
            yh = y[:, hd * LA_DK:(hd + 1) * LA_DK]
            yn = yh * lax.rsqrt(jnp.sum(yh * yh, -1, keepdims=True) + L2_EPS)
            la_ref[:, c0 + hd * LA_DK:c0 + (hd + 1) * LA_DK] = yn * scale if scale != 1.0 else yn
    qvt = _dot_nt(wqvt_ref[...], h)
    qt_ref[...] = (qvt[:DIFF_QK] * (DIFF_DH ** -0.5 * math.log2(math.e))).astype(BF16)
    vt_ref[0] = qvt[DIFF_QK:].astype(BF16)
    k_ref[...] = _dot(h, wk_ref[...]).astype(BF16)
    la_ref[:, W:] = _dot(h, wzba_ref[...])


def _in_proj(xf, nw, wqkv, wzba, wqvt, wk, convw, tm, T):
    M, D = xf.shape
    assert T % tm == 0
    kern = functools.partial(_in_proj_kernel, tiles_per_seq=T // tm)
    return pl.pallas_call(
        kern,
        grid=(M // tm,),
        in_specs=[
            pl.BlockSpec((tm, D), lambda i: (i, 0)),
            pl.BlockSpec((1, D), lambda i: (0, 0)),
            pl.BlockSpec(wqkv.shape, lambda i: (0, 0, 0)),
            pl.BlockSpec(wzba.shape, lambda i: (0, 0)),
            pl.BlockSpec(wqvt.shape, lambda i: (0, 0)),
            pl.BlockSpec(wk.shape, lambda i: (0, 0)),
            pl.BlockSpec(convw.shape, lambda i: (0, 0)),
        ],
        out_specs=[
            pl.BlockSpec((tm, LA_WIDTH), lambda i: (i, 0)),
            pl.BlockSpec((DIFF_QK, tm), lambda i: (0, i)),
            pl.BlockSpec((1, DIFF_V, tm), lambda i: (i, 0, 0)),
            pl.BlockSpec((tm, wk.shape[1]), lambda i: (i, 0)),
        ],
        out_shape=[
            jax.ShapeDtypeStruct((M, LA_WIDTH), F32),
            jax.ShapeDtypeStruct((DIFF_QK, M), BF16),
            jax.ShapeDtypeStruct((M // tm, DIFF_V, tm), BF16),
            jax.ShapeDtypeStruct((M, wk.shape[1]), BF16),
        ],
        scratch_shapes=[pltpu.VMEM((wqkv.shape[0], tm + SUBLANES, wqkv.shape[2]), F32)],
        compiler_params=pltpu.CompilerParams(
            dimension_semantics=("arbitrary",), vmem_limit_bytes=VMEM_LIMIT),
        name="in_proj",
    )(xf, nw.reshape(1, D), wqkv, wzba, wqvt, wk, convw)


def _unit_lower_inverse_minus_eye(Ls, C):
    row = lax.broadcasted_iota(jnp.int32, (C, C), 0)
    col = lax.broadcasted_iota(jnp.int32, (C, C), 1)
    xs = None
    s = 1
    while s < C:
        same_pair = (row // (2 * s)) == (col // (2 * s))
        low_left = same_pair & ((row // s) % 2 == 1) & ((col // s) % 2 == 0)
        offs = [jnp.where(low_left, L, 0.0) for L in Ls]
        if xs is None:
            xs = [-m for m in offs]
        else:
            xb = [x.astype(BF16) for x in xs]
            ys = [m + _dot(m.astype(BF16), x) for m, x in zip(offs, xb)]
            xs = [x - y - _dot(x16, y.astype(BF16)) for x, x16, y in zip(xs, xb, ys)]
        s *= 2
    return xs


def _gdn_kernel(qkv_ref, z_ref, ba_ref, avec_ref, dtvec_ref, nw_ref,
                o_ref, state, g_s, gt_s, beta_s, *, TB, C):
    t = pl.program_id(0)
    B = qkv_ref.shape[0]
    nC = TB // C

    @pl.when(t == 0)
    def _():
        state[...] = jnp.zeros(state.shape, F32)

    row = lax.broadcasted_iota(jnp.int32, (C, C), 0)
    col = lax.broadcasted_iota(jnp.int32, (C, C), 1)
    causal = row >= col
    strict = row > col
    tril = causal.astype(F32)
    ba_lane = lax.broadcasted_iota(jnp.int32, (1, BA_PAD), 1)
    is_g = (ba_lane >= LA_HEADS) & (ba_lane < 2 * LA_HEADS)

    for b in range(B):
        ba = ba_ref[b]
        beta_s[b] = jax.nn.sigmoid(ba)
        g_raw = -jnp.where(is_g, jnp.exp(avec_ref[...]), 0.0) * _softplus(ba + dtvec_ref[...])
        for c in range(nC):
            g_s[b, c * C:(c + 1) * C, :] = _dot(tril, g_raw[c * C:(c + 1) * C, :], HIGHEST)
        for j in range(TB // LANES):
            gt_s[b, :, j * LANES:(j + 1) * LANES] = g_s[b, j * LANES:(j + 1) * LANES, :].T

    probs = [(b, c, h) for b in range(B) for c in range(nC) for h in range(LA_HEADS)]
    index = {p: n for n, p in enumerate(probs)}
    Ls, intras, rhss, qgs, kds, egls = [], [], [], [], [], []
    for b, c, h in probs:
        rs = slice(c * C, (c + 1) * C)
        qn = qkv_ref[b, rs, h * LA_DK:(h + 1) * LA_DK]
        kn = qkv_ref[b, rs, LA_QK + h * LA_DK:LA_QK + (h + 1) * LA_DK]
        v = qkv_ref[b, rs, 2 * LA_QK + h * LA_DV:2 * LA_QK + (h + 1) * LA_DV]
        gb = jnp.broadcast_to(g_s[b, rs, LA_HEADS + h:LA_HEADS + h + 1], (C, LANES))
        beta = jnp.broadcast_to(beta_s[b, rs, h:h + 1], (C, LANES))
        g_row = gt_s[b, LA_HEADS + h:LA_HEADS + h + 1, rs]
        gdiff = gb[:, :C] - g_row
        decay = jnp.where(causal, jnp.exp(jnp.where(causal, gdiff, 0.0)), 0.0)
        kb = kn * beta
        kq = _dot_nt(jnp.concatenate([kb, qn], axis=0).astype(BF16), kn.astype(BF16))
        Ls.append(jnp.where(strict, kq[:C] * decay, 0.0))
        intras.append((kq[C:] * decay).astype(BF16))
        eg = jnp.exp(gb)
        rhss.append(jnp.concatenate([v * beta, kb * eg], axis=1))
        qgs.append((qn * eg).astype(BF16))
        g_last = gb[C - 1:C, :]
        kds.append((kn * jnp.exp(g_last - gb)).astype(BF16))
        egls.append(jnp.exp(g_last))

    xs = _unit_lower_inverse_minus_eye(Ls, C)
    sols = [r + _dot(x.astype(BF16), r.astype(BF16)) for x, r in zip(xs, rhss)]

    streams = [(b, h) for b in range(B) for h in range(LA_HEADS)]
    S = [state[b * LA_HEADS + h] for b, h in streams]
    for c in range(nC):
        ps = [index[(b, c, h)] for b, h in streams]
        r2 = [_dot(jnp.concatenate([sols[p][:, LA_DV:].astype(BF16), qgs[p]], axis=0),
                   S[n].astype(BF16)) for n, p in enumerate(ps)]
        vn = [(sols[p][:, :LA_DV] - r2[n][:C]).astype(BF16) for n, p in enumerate(ps)]
        S = [S[n] * egls[p] + _dot_tn(kds[p], vn[n]) for n, p in enumerate(ps)]
        rs = slice(c * C, (c + 1) * C)
        for n, ((b, h), p) in enumerate(zip(streams, ps)):
            o = r2[n][C:] + _dot(intras[p], vn[n])
            zg = _silu(z_ref[b, rs, h * LA_DV:(h + 1) * LA_DV])
            o_ref[b, rs, h * LA_DV:(h + 1) * LA_DV] = (
                _rms(o, nw_ref[...], NORM_EPS) * zg).astype(BF16)
    for n in range(len(streams)):
        state[n] = S[n]


def _gdn(la, avec, dtvec, nw, B, T, TB, C):
    nT = T // TB
    W = 2 * LA_QK + LA_V
    la3 = la.reshape(B, T, la.shape[1])
    kern = functools.partial(_gdn_kernel, TB=TB, C=C)
    out = pl.pallas_call(
        kern,
        grid=(nT,),
        in_specs=[
            pl.BlockSpec((B, TB, W), lambda t: (0, t, 0)),
            pl.BlockSpec((B, TB, LA_V), lambda t: (0, t, W // LA_V)),
            pl.BlockSpec((B, TB, BA_PAD), lambda t: (0, t, (W + LA_V) // BA_PAD)),
            pl.BlockSpec((1, BA_PAD), lambda t: (0, 0)),
            pl.BlockSpec((1, BA_PAD), lambda t: (0, 0)),
            pl.BlockSpec((1, LA_DV), lambda t: (0, 0)),
        ],
        out_specs=pl.BlockSpec((B, TB, LA_V), lambda t: (0, t, 0)),
        out_shape=jax.ShapeDtypeStruct((B, T, LA_V), BF16),
        scratch_shapes=[
            pltpu.VMEM((B * LA_HEADS, LA_DK, LA_DV), F32),
            pltpu.VMEM((B, TB, BA_PAD), F32),
            pltpu.VMEM((B, LANES, TB), F32),
            pltpu.VMEM((B, TB, BA_PAD), F32),
        ],
        compiler_params=pltpu.CompilerParams(
            dimension_semantics=("arbitrary",), vmem_limit_bytes=VMEM_LIMIT),
        name="gdn",
    )(la3, la3, la3, avec, dtvec, nw.reshape(1, LA_DV))
    return out.reshape(B * T, LA_V)


def _attn_kernel(lam_ref, tab_ref, q_ref, k_ref, vt_ref, nw_ref, o_ref, bias_s, *,
                 tq, lam_init):
    i = pl.program_id(2)
    heads = range(HEADS_PER_STEP)

    def cols(hh):
        return slice(hh * LANES, (hh + 1) * LANES)

    def vt(hh, keys):
        j, span = keys
        return jnp.concatenate([vt_ref[j + u, hh * LANES:(hh + 1) * LANES, :]
                                for u in range(span)], axis=1)

    @pl.when(i == 0)
    def _():
        for hh in heads:
            for e in range(2):
                x = jnp.broadcast_to(tab_ref[hh, e], (tq, 2 * tq))
                bias_s[hh, e] = pltpu.roll(x, 0, 1, stride=1, stride_axis=0)[:, :tq]

    q2t = []
    for hh in heads:
        qt = q_ref[hh * LANES:(hh + 1) * LANES, :].astype(F32)
        sub = lax.broadcasted_iota(jnp.int32, qt.shape, 0)
        q2t.append(jnp.concatenate([jnp.where(sub < DIFF_DH, qt, 0.0),
                                    jnp.where(sub >= DIFF_DH, qt, 0.0)], axis=1).astype(BF16))

    def scores(hh, keys):
        j, span = keys
        kj = k_ref[0, pl.ds(pl.multiple_of(j * tq, tq), span * tq), cols(hh)]
        return _dot(kj, q2t[hh])

    def biased(s, bias):
        return jnp.concatenate([s[:, :tq] + bias, s[:, tq:] + bias], axis=1)

    def update(hh, carry, s, keys):
        m, l, acc = carry
        m_new = jnp.maximum(m, jnp.max(s, axis=0, keepdims=True))
        alpha = jnp.exp2(m - m_new)
        p = jnp.exp2(s - m_new)
        l = alpha * l + jnp.sum(p, axis=0, keepdims=True)
        acc = alpha * acc + _dot(vt(hh, keys), p.astype(BF16))
        return m_new, l, acc

    def group_update(carries, blocks):
        ms = [c[0] for c in carries]
        ls = [c[1] for c in carries]
        accs = [c[2] for c in carries]
        m_news = list(ms)
        for u in range(len(blocks[0])):
            for hh in heads:
                j, thunk, off = blocks[hh][u]
                s = thunk()
                ref = ms[hh] if off is None else ms[hh] + off
                p = jnp.exp2(s - ref).astype(BF16)
                vt_j = vt(hh, j)
                ones = jnp.ones((2 * SUBLANES, vt_j.shape[1]), BF16)
                pv = _dot(jnp.concatenate([vt_j, ones], axis=0), p)
                ls[hh] = ls[hh] + pv[DIFF_DV:DIFF_DV + 1]
                accs[hh] = accs[hh] + pv[:DIFF_DV]
                smax = jnp.max(s, axis=0, keepdims=True)
                m_news[hh] = jnp.maximum(m_news[hh], smax if off is None else smax - off)
        fast = []
        lead = None
        for hh in heads:
            alpha = jnp.exp2(ms[hh] - m_news[hh])
            fast.append((m_news[hh], alpha * ls[hh], alpha * accs[hh]))
            lead_h = jnp.max(m_news[hh] - ms[hh])
            lead = lead_h if lead is None else jnp.maximum(lead, lead_h)

        def redo():
            out = []
            for hh in heads:
                c = carries[hh]
                for j, thunk, off in blocks[hh]:
                    s = thunk()
                    c = update(hh, c, s if off is None else s - off, j)
                out.append(c)
            return tuple(out)

        return lax.cond(lead <= MAX_LEAD, lambda: tuple(fast), redo)

    j_near = jnp.maximum(i - 1, 0)
    edge = LANES
    assert edge >= MAX_DISTANCE and tq >= 2 * edge

    def near_scores(hh):
        s = scores(hh, (j_near, 1))
        corner = bias_s[hh, 1, tq - edge:tq, 0:edge]
        bot = s[tq - edge:]
        bot = jnp.concatenate([bot[:, :edge] + corner, bot[:, edge:tq],
                               bot[:, tq:tq + edge] + corner, bot[:, tq + edge:]], axis=1)
        return jnp.concatenate([s[:tq - edge], bot], axis=0)

    no_near = jnp.where(i == 0, -NEG_BIG, 0.0)
    carries, tail = [], []
    for hh in heads:
        s_diag = biased(scores(hh, (i, 1)), bias_s[hh, 0])
        m0 = jnp.max(s_diag[:DIFF_DH], axis=0, keepdims=True)
        carries.append((m0, jnp.zeros((1, 2 * tq), F32), jnp.zeros((DIFF_DV, 2 * tq), F32)))
        tail.append([((i, 1), functools.partial(lambda s: s, s_diag), None),
                     ((j_near, 1), functools.partial(near_scores, hh), no_near)])
    carries = group_update(tuple(carries), tail)
    n_far = jnp.maximum(i - 1, 0)
    done = 0
    for width in FAR_UNROLLS:
        def body(g, c, width=width, done=done):
            span = min(width, FAR_SPAN)
            ranges = [(done + width * g + u, span) for u in range(0, width, span)]
            return group_update(c, [[(r, functools.partial(scores, hh, r), None) for r in ranges]
                                    for hh in heads])
        n_groups = (n_far - done) // width
        carries = lax.fori_loop(0, n_groups, body, carries)
        done = done + n_groups * width

    lam = (jnp.exp(jnp.sum(lam_ref[0:1, :] * lam_ref[1:2, :], axis=-1, keepdims=True))
           - jnp.exp(jnp.sum(lam_ref[2:3, :] * lam_ref[3:4, :], axis=-1, keepdims=True)) + lam_init)
    for hh in heads:
        m, l, acc = carries[hh]
        inv_l = 1.0 / l
        ot = acc[:, :tq] * inv_l[:, :tq] - lam * (acc[:, tq:] * inv_l[:, tq:])
        gain = jnp.concatenate([nw_ref[...]] * (tq // LANES), axis=1)
        ms = jnp.mean(ot * ot, axis=0, keepdims=True)
        o_ref[hh * LANES:(hh + 1) * LANES, :] = (
            ot * lax.rsqrt(ms + DIFF_NORM_EPS) * gain * (1.0 - lam_init)).astype(BF16)


def _attn(qt, vt, k, lams, bias_tab, nw, B, T, tq, lam_init):
    assert T % tq == 0 and DIFF_HEADS % HEADS_PER_STEP == 0 and vt.shape[2] == tq
    nq = T // tq
    hw = HEADS_PER_STEP * LANES
    k3 = k.reshape(B, T, k.shape[1])
    kern = functools.partial(_attn_kernel, tq=tq, lam_init=lam_init)
    return pl.pallas_call(
        kern,
        grid=(B, DIFF_HEADS // HEADS_PER_STEP, nq),
        in_specs=[
            pl.BlockSpec((4, DIFF_DH), lambda b, g, i: (0, 0)),
            pl.BlockSpec((HEADS_PER_STEP, 2, 1, 2 * tq), lambda b, g, i: (g, 0, 0, 0)),
            pl.BlockSpec((hw, tq), lambda b, g, i: (g, b * nq + i)),
            pl.BlockSpec((1, T, hw), lambda b, g, i: (b, 0, g)),
            pl.BlockSpec((nq, hw, tq), lambda b, g, i: (b, g, 0)),
            pl.BlockSpec((DIFF_DV, LANES), lambda b, g, i: (0, 0)),
        ],
        out_specs=pl.BlockSpec((hw, tq), lambda b, g, i: (g, b * nq + i)),
        out_shape=jax.ShapeDtypeStruct((DIFF_V, B * T), BF16),
        scratch_shapes=[
            pltpu.VMEM((HEADS_PER_STEP, 2, tq, tq), F32),
        ],
        compiler_params=pltpu.CompilerParams(
            dimension_semantics=("parallel", "parallel", "arbitrary"),
            vmem_limit_bytes=VMEM_LIMIT),
        name="diff_attn",
    )(lams, bias_tab, qt, k3, vt, jnp.broadcast_to(nw.astype(F32)[:, None], (DIFF_DV, LANES)))


def _t5_bucket(rel):
    n = jnp.maximum(rel, 0)
    max_exact = NUM_BUCKETS // 2
    nf = jnp.maximum(n, 1).astype(F32)
    large = max_exact + (jnp.log(nf / max_exact) / math.log(MAX_DISTANCE / max_exact)
                         * (NUM_BUCKETS - max_exact)).astype(jnp.int32)
    large = jnp.minimum(large, NUM_BUCKETS - 1)
    return jnp.where(n < max_exact, n, large)


def _bias_tables(rel_bias, tq):
    assert tq + 1 >= MAX_DISTANCE
    m = jnp.arange(2 * tq)
    far = rel_bias[NUM_BUCKETS - 1].astype(F32)
    tabs = []
    for d in (0, tq):
        rel = jnp.where(m < tq, d + m, d + m - 2 * tq)
        b = (rel_bias[_t5_bucket(rel)].astype(F32) - far) * math.log2(math.e)
        tabs.append(jnp.where((rel >= 0)[:, None], b, NEG_BIG).T)
    return jnp.stack(tabs, axis=1)[:, :, None, :]


def _ffn_kernel(x_ref, ola_ref, od_ref, wout_ref, nw_ref, wgu_ref, wd_ref, fnw_ref,
                out_ref, *, final_norm):
    d_ff = wd_ref.shape[0]
    y = (x_ref[...] + _dot(ola_ref[...], wout_ref[0:LA_V, :])
         + _dot_tn(od_ref[...], wout_ref[LA_V:LA_V + DIFF_V, :]))
    h = _rms(y, nw_ref[...], NORM_EPS).astype(BF16)
    gate = _dot(h, wgu_ref[:, 0:d_ff])
    up = _dot(h, wgu_ref[:, d_ff:2 * d_ff])
    y = y + _dot((_silu(gate) * up).astype(BF16), wd_ref[...])
    if final_norm:
        y = _rms(y, fnw_ref[...], NORM_EPS)
    out_ref[...] = y


def _ffn(xf, o_la, o_d, wout, nw, wgu, wdn, fnw, layer, tm, final_norm):
    M, D = xf.shape
    kern = functools.partial(_ffn_kernel, final_norm=final_norm)
    resident = dict(pipeline_mode=pl.Buffered(1))

    def layer_slab(w):
        return pl.BlockSpec((None,) + w.shape[1:], lambda i: (layer, 0, 0), **resident)

    return pl.pallas_call(
        kern,
        grid=(M // tm,),
        in_specs=[
            pl.BlockSpec((tm, D), lambda i: (i, 0)),
            pl.BlockSpec((tm, LA_V), lambda i: (i, 0)),
            pl.BlockSpec((DIFF_V, tm), lambda i: (0, i)),
            layer_slab(wout),
            pl.BlockSpec((1, D), lambda i: (0, 0)),
            layer_slab(wgu),
            layer_slab(wdn),
            pl.BlockSpec((1, D), lambda i: (0, 0)),
        ],
        out_specs=pl.BlockSpec((tm, D), lambda i: (i, 0)),
        out_shape=jax.ShapeDtypeStruct((M, D), F32),
        compiler_params=pltpu.CompilerParams(
            dimension_semantics=("parallel",), vmem_limit_bytes=VMEM_LIMIT),
        name="out_proj_ffn",
    )(xf, o_la, o_d, wout, nw.reshape(1, D), wgu, wdn, fnw.reshape(1, D))


def _pick(n, pref):
    return pref if n % pref == 0 else n


def kernel(x, attn_norm_w, w_in, conv_w, a_log, dt_bias, la_norm_w, lambda_q1, lambda_k1,
           lambda_q2, lambda_k2, diff_norm_w, rel_bias, w_out, ffn_norm_w, w_gate_up,
           w_down, final_norm_w):
    B, T, D = x.shape
    depth = w_in.shape[0]
    M = B * T
    tm_proj = _pick(M, 512)
    tm_ffn = _pick(M, 512)
    tb = _pick(T, 256)
    chunk = 128
    tq = _pick(T, 512)

    n_la = 2 * LA_QK + 2 * LA_V
    bias_tab = _bias_tables(rel_bias, tq)
    lane_pad = BA_PAD - 2 * LA_HEADS

    w_in_bf = lax.optimization_barrier(w_in.astype(BF16))
    w_out_bf = w_out.astype(BF16)
    w_gate_up_bf = w_gate_up.astype(BF16)
    w_down_bf = w_down.astype(BF16)
    n_qkv = 2 * LA_QK + LA_V

    xf = x.reshape(M, D)
    for l in range(depth):
        w = w_in_bf[l]
        wqkv = jnp.stack([w[:, s * CONV_SLAB:(s + 1) * CONV_SLAB]
                          for s in range(n_qkv // CONV_SLAB)])
        wzba = jnp.concatenate([w[:, n_qkv:n_la + 2 * LA_HEADS],
                                jnp.zeros((D, lane_pad), w.dtype)], axis=1)
        n_d = n_la + 2 * LA_HEADS
        wqvt = jnp.concatenate([w[:, n_d:n_d + DIFF_QK], w[:, n_d + 2 * DIFF_QK:]], axis=1).T
        wk = w[:, n_d + DIFF_QK:n_d + 2 * DIFF_QK]
        assert tm_proj == tq
        la, qt, vt, k = _in_proj(xf, attn_norm_w[l], wqkv, wzba, wqvt, wk, conv_w[l], tm_proj, T)

        avec = jnp.concatenate([jnp.zeros((LA_HEADS,), F32), a_log[l].astype(F32),
                                jnp.zeros((lane_pad,), F32)]).reshape(1, BA_PAD)
        dtvec = jnp.concatenate([jnp.zeros((LA_HEADS,), F32), dt_bias[l].astype(F32),
                                 jnp.zeros((lane_pad,), F32)]).reshape(1, BA_PAD)
        o_la = _gdn(la, avec, dtvec, la_norm_w[l], B, T, tb, chunk)

        lam_init = 0.8 - 0.6 * math.exp(-0.3 * l)
        lams = jnp.stack([lambda_q1[l], lambda_k1[l], lambda_q2[l], lambda_k2[l]]).astype(F32)
        o_d = _attn(qt, vt, k, lams, bias_tab, diff_norm_w[l], B, T, tq, lam_init)

        xf = _ffn(xf, o_la, o_d, w_out_bf, ffn_norm_w[l], w_gate_up_bf, w_down_bf, final_norm_w,
                  l, tm_ffn, final_norm=(l == depth - 1))
    return xf.reshape(B, T, D)
```
